```python
import math
import jax, jax.numpy as jnp
from jax import lax
import numpy as np

D_MODEL = 2048
BATCH = 1
SEQ = 16384
DEPTH = 1

S5_WIDTH = D_MODEL // 4
S5_GROUP = 16
S5_GROUPS = S5_WIDTH // S5_GROUP
S5_STATE = 64
LRU_WIDTH = 3 * D_MODEL // 4
LRU_HEADS = 12
LRU_HEAD_DIM = LRU_WIDTH // LRU_HEADS
CONV_WIDTH = 4
LRU_C = 8.0
IN_COLS = S5_WIDTH + 2 * LRU_WIDTH + 2 * D_MODEL
MOE_GROUPS = 4
EXPERTS_PER_GROUP = 8
N_EXPERTS = MOE_GROUPS * EXPERTS_PER_GROUP
TOP_K = 2
EXPERT_FF = D_MODEL // 8
NORM_EPS = 1e-6

kernel_name = 'hybrid_s5_rglru_hmoe_block'


def rms_norm(x, w):
    x32 = x.astype(jnp.float32)
    y = x32 * lax.rsqrt(jnp.mean(x32 * x32, axis=-1, keepdims=True) + NORM_EPS)
    return (y * w.astype(jnp.float32)).astype(x.dtype)


def _linear_combine(left, right):
    a_l, b_l = left
    a_r, b_r = right
    return a_r * a_l, a_r * b_l + b_r


def s5_branch(u, lam_re, lam_im, log_step, b_re, b_im, c_re, c_im, d_skip, w_glu, b_glu):
    bsz, seqlen, _ = u.shape
    f32 = jnp.float32
    u32 = u.astype(f32).reshape(bsz, seqlen, S5_GROUPS, S5_GROUP)
    lam = lax.complex(lam_re.astype(f32), lam_im.astype(f32))
    step = jnp.exp(log_step.astype(f32))[:, None]
    lam_bar = jnp.exp(lam * step)
    b_c = lax.complex(b_re.astype(f32), b_im.astype(f32))
    b_bar = ((lam_bar - 1.0) / lam)[..., None] * b_c
    bu = jnp.einsum('gpc,blgc->blgp', b_bar, u32.astype(jnp.complex64))
    a = jnp.broadcast_to(lam_bar, bu.shape)
    _, states = lax.associative_scan(_linear_combine, (a, bu), axis=1)
    c_c = lax.complex(c_re.astype(f32), c_im.astype(f32))
    y = jnp.real(jnp.einsum('gcp,blgp->blgc', c_c, states))
    y = y + d_skip.astype(f32).reshape(S5_GROUPS, S5_GROUP) * u32
    y = jax.nn.gelu(y.reshape(bsz, seqlen, S5_WIDTH))
    glu = jax.nn.sigmoid(y @ w_glu.astype(f32) + b_glu.astype(f32))
    return y * glu


def block_diag_linear(x, w, b):
    bsz, seqlen, _ = x.shape
    xh = x.reshape(bsz, seqlen, LRU_HEADS, LRU_HEAD_DIM)
    y = jnp.einsum('blhi,hij->blhj', xh, w.astype(x.dtype))
    return y.reshape(bsz, seqlen, LRU_WIDTH) + b.astype(x.dtype)


def rglru_branch(xb, yb, conv_w, conv_b, w_a, b_a, w_x, b_x, lam):
    f32 = jnp.float32
    seqlen = xb.shape[1]
    x32 = xb.astype(f32)
    xp = jnp.pad(x32, ((0, 0), (CONV_WIDTH - 1, 0), (0, 0)))
    xc = conv_b.astype(f32)
    for k in range(CONV_WIDTH):
        xc = xc + xp[:, k:k + seqlen] * conv_w[k].astype(f32)
    r = jax.nn.sigmoid(block_diag_linear(xc, w_a.astype(f32), b_a.astype(f32)))
    i = jax.nn.sigmoid(block_diag_linear(xc, w_x.astype(f32), b_x.astype(f32)))
    log_a = -LRU_C * r * jax.nn.softplus(-lam.astype(f32))
    a = jnp.exp(log_a)
    mult = jnp.sqrt(-jnp.expm1(2.0 * log_a))
    pos = jnp.arange(seqlen)[None, :, None]
    mult = jnp.where(pos == 0, 1.0, mult)
    _, h = lax.associative_scan(_linear_combine, (a, xc * i * mult), axis=1)
    return h * jax.nn.gelu(yb.astype(f32))


def mixer_sublayer(x, norm_w, w_in, b_gate, s5_lam_re, s5_lam_im, s5_log_step, s5_b_re, s5_b_im,
                   s5_c_re, s5_c_im, s5_d, s5_w_glu, s5_b_glu, lru_conv_w, lru_conv_b, lru_w_a,
                   lru_b_a, lru_w_x, lru_b_x, lru_lambda, w_proj_s5, w_proj_lru, w_out):
    h = rms_norm(x, norm_w)
    proj = h @ w_in
    o1 = S5_WIDTH
    o2 = o1 + LRU_WIDTH
    o3 = o2 + LRU_WIDTH
    u_s5 = proj[..., :o1]
    x_lru = proj[..., o1:o2]
    y_lru = proj[..., o2:o3]
    gates = jax.nn.sigmoid((proj[..., o3:] + b_gate).astype(jnp.float32))
    g_s5 = gates[..., :D_MODEL]
    g_lru = gates[..., D_MODEL:]
    out_s5 = s5_branch(u_s5, s5_lam_re, s5_lam_im, s5_log_step, s5_b_re, s5_b_im,
                       s5_c_re, s5_c_im, s5_d, s5_w_glu, s5_b_glu)
    out_lru = rglru_branch(x_lru, y_lru, lru_conv_w, lru_conv_b, lru_w_a, lru_b_a,
                           lru_w_x, lru_b_x, lru_lambda)
    br_s5 = out_s5.astype(x.dtype) @ w_proj_s5
    br_lru = out_lru.astype(x.dtype) @ w_proj_lru
    merged = (g_s5 * br_s5.astype(jnp.float32) + g_lru * br_lru.astype(jnp.float32)).astype(x.dtype)
    return x + merged @ w_out


def moe_sublayer(x, norm_w, w_rg, b_rg, w_re, b_re, w_e_gate, w_e_up, w_e_down):
    bsz, seqlen, d = x.shape
    f32 = jnp.float32
    ht = rms_norm(x, norm_w).reshape(bsz * seqlen, d)
    h32 = ht.astype(f32)
    g_prob = jax.nn.softmax(h32 @ w_rg.astype(f32) + b_rg.astype(f32), axis=-1)
    g_top, g_idx = lax.top_k(g_prob, 1)
    e_logits = (h32 @ w_re.astype(f32) + b_re.astype(f32)).reshape(-1, MOE_GROUPS, EXPERTS_PER_GROUP)
    e_sel = jnp.take_along_axis(e_logits, g_idx[:, :, None], axis=1)[:, 0]
    e_prob = jax.nn.softmax(e_sel, axis=-1)
    e_top, e_idx = lax.top_k(e_prob, TOP_K)
    e_top = e_top / jnp.sum(e_top, axis=-1, keepdims=True)
    weights = g_top * e_top
    expert_id = g_idx * EXPERTS_PER_GROUP + e_idx
    combine = jnp.sum(jax.nn.one_hot(expert_id, N_EXPERTS, dtype=f32) * weights[..., None], axis=1)
    gate = jnp.einsum('td,edf->tef', ht, w_e_gate)
    up = jnp.einsum('td,edf->tef', ht, w_e_up)
    act = jax.nn.silu(gate) * up * combine[:, :, None].astype(ht.dtype)
    out = jnp.einsum('tef,efd->td', act, w_e_down)
    return x + out.reshape(bsz, seqlen, d)


def setup_inputs(seed: int = 0) -> dict:
    key = jax.random.key(seed)
    ks = jax.random.split(key, 40)
    f32 = jnp.float32

    def nrm(k, shape, scale):
        return jax.random.normal(k, shape, f32) * scale

    G, P, GS = S5_GROUPS, S5_STATE, S5_GROUP
    x = nrm(ks[0], (BATCH, SEQ, D_MODEL), 1.0)
    norm_mix_w = 1.0 + nrm(ks[1], (DEPTH, D_MODEL), 0.02)
    w_in = nrm(ks[2], (DEPTH, D_MODEL, IN_COLS), D_MODEL ** -0.5)
    b_gate = nrm(ks[3], (DEPTH, 2 * D_MODEL), 0.02)
    s5_lam_re = -0.5 + nrm(ks[4], (DEPTH, G, P), 0.01)
    s5_lam_im = math.pi * jnp.arange(P, dtype=f32) + nrm(ks[5], (DEPTH, G, P), 0.01)
    s5_log_step = jax.random.uniform(ks[6], (DEPTH, G), f32, math.log(1e-3), math.log(1e-1))
    s5_b_re = nrm(ks[7], (DEPTH, G, P, GS), (2.0 * GS) ** -0.5)
    s5_b_im = nrm(ks[8], (DEPTH, G, P, GS), (2.0 * GS) ** -0.5)
    s5_c_re = nrm(ks[9], (DEPTH, G, GS, P), (2.0 * P) ** -0.5)
    s5_c_im = nrm(ks[10], (DEPTH, G, GS, P), (2.0 * P) ** -0.5)
    s5_d = nrm(ks[11], (DEPTH, S5_WIDTH), 1.0)
    s5_w_glu = nrm(ks[12], (DEPTH, S5_WIDTH, S5_WIDTH), S5_WIDTH ** -0.5)
    s5_b_glu = nrm(ks[13], (DEPTH, S5_WIDTH), 0.02)
    lru_conv_w = nrm(ks[14], (DEPTH, CONV_WIDTH, LRU_WIDTH), CONV_WIDTH ** -0.5)
    lru_conv_b = nrm(ks[15], (DEPTH, LRU_WIDTH), 0.02)
    lru_w_a = nrm(ks[16], (DEPTH, LRU_HEADS, LRU_HEAD_DIM, LRU_HEAD_DIM), LRU_HEAD_DIM ** -0.5)
    lru_b_a = nrm(ks[17], (DEPTH, LRU_WIDTH), 0.02)
    lru_w_x = nrm(ks[18], (DEPTH, LRU_HEADS, LRU_HEAD_DIM, LRU_HEAD_DIM), LRU_HEAD_DIM ** -0.5)
    lru_b_x = nrm(ks[19], (DEPTH, LRU_WIDTH), 0.02)
    a0 = jax.random.uniform(ks[20], (DEPTH, LRU_WIDTH), f32, 0.9, 0.999)
    s = a0 ** (1.0 / LRU_C)
    lru_lambda = jnp.log(s) - jnp.log1p(-s)
    w_proj_s5 = nrm(ks[21], (DEPTH, S5_WIDTH, D_MODEL), S5_WIDTH ** -0.5)
    w_proj_lru = nrm(ks[22], (DEPTH, LRU_WIDTH, D_MODEL), LRU_WIDTH ** -0.5)
    w_out = nrm(ks[23], (DEPTH, D_MODEL, D_MODEL), D_MODEL ** -0.5)
    norm_ffn_w = 1.0 + nrm(ks[24], (DEPTH, D_MODEL), 0.02)
    w_router_group = nrm(ks[25], (DEPTH, D_MODEL, MOE_GROUPS), D_MODEL ** -0.5)
    b_router_group = nrm(ks[26], (DEPTH, MOE_GROUPS), 0.01)
    w_router_expert = nrm(ks[27], (DEPTH, D_MODEL, N_EXPERTS), D_MODEL ** -0.5)
    b_router_expert = nrm(ks[28], (DEPTH, N_EXPERTS), 0.01)
    w_e_gate = nrm(ks[29], (DEPTH, N_EXPERTS, D_MODEL, EXPERT_FF), D_MODEL ** -0.5)
    w_e_up = nrm(ks[30], (DEPTH, N_EXPERTS, D_MODEL, EXPERT_FF), D_MODEL ** -0.5)
    w_e_down = nrm(ks[31], (DEPTH, N_EXPERTS, EXPERT_FF, D_MODEL), EXPERT_FF ** -0.5)
    norm_final_w = 1.0 + nrm(ks[32], (D_MODEL,), 0.02)
    return {'x': x, 'norm_mix_w': norm_mix_w, 'w_in': w_in, 'b_gate': b_gate,
            's5_lam_re': s5_lam_re, 's5_lam_im': s5_lam_im, 's5_log_step': s5_log_step,
            's5_b_re': s5_b_re, 's5_b_im': s5_b_im, 's5_c_re': s5_c_re, 's5_c_im': s5_c_im,
            's5_d': s5_d, 's5_w_glu': s5_w_glu, 's5_b_glu': s5_b_glu,
            'lru_conv_w': lru_conv_w, 'lru_conv_b': lru_conv_b, 'lru_w_a': lru_w_a, 'lru_b_a': lru_b_a,
            'lru_w_x': lru_w_x, 'lru_b_x': lru_b_x, 'lru_lambda': lru_lambda,
            'w_proj_s5': w_proj_s5, 'w_proj_lru': w_proj_lru, 'w_out': w_out,
            'norm_ffn_w': norm_ffn_w, 'w_router_group': w_router_group, 'b_router_group': b_router_group,
            'w_router_expert': w_router_expert, 'b_router_expert': b_router_expert,
            'w_e_gate': w_e_gate, 'w_e_up': w_e_up, 'w_e_down': w_e_down, 'norm_final_w': norm_final_w}


def reference(x, norm_mix_w, w_in, b_gate, s5_lam_re, s5_lam_im, s5_log_step, s5_b_re, s5_b_im,
              s5_c_re, s5_c_im, s5_d, s5_w_glu, s5_b_glu, lru_conv_w, lru_conv_b, lru_w_a, lru_b_a,
              lru_w_x, lru_b_x, lru_lambda, w_proj_s5, w_proj_lru, w_out, norm_ffn_w,
              w_router_group, b_router_group, w_router_expert, b_router_expert,
              w_e_gate, w_e_up, w_e_down, norm_final_w):
    for i in range(DEPTH):
        x = mixer_sublayer(x, norm_mix_w[i], w_in[i], b_gate[i], s5_lam_re[i], s5_lam_im[i],
                           s5_log_step[i], s5_b_re[i], s5_b_im[i], s5_c_re[i], s5_c_im[i], s5_d[i],
                           s5_w_glu[i], s5_b_glu[i], lru_conv_w[i], lru_conv_b[i], lru_w_a[i],
                           lru_b_a[i], lru_w_x[i], lru_b_x[i], lru_lambda[i], w_proj_s5[i],
                           w_proj_lru[i], w_out[i])
        x = moe_sublayer(x, norm_ffn_w[i], w_router_group[i], b_router_group[i],
                         w_router_expert[i], b_router_expert[i], w_e_gate[i], w_e_up[i], w_e_down[i])
    return rms_norm(x, norm_final_w)
```

```python
import functools

import jax
import jax.numpy as jnp
from jax import lax
from jax.experimental import pallas as pl
from jax.experimental.pallas import tpu as pltpu

F32 = jnp.float32
BF16 = jnp.bfloat16
HIGHEST = lax.Precision.HIGHEST

NORM_EPS = 1e-6
LRU_C = 8.0
S5_GROUP = 16
S5_CHUNK = 16
CONV_WIDTH = 4
SUBLANES = 8
LANES = 128
VMEM_LIMIT = 56 * 1024 * 1024


def _params(n_axes, vmem=VMEM_LIMIT):
    return pltpu.CompilerParams(dimension_semantics=("arbitrary",) * n_axes,
                                vmem_limit_bytes=vmem)


def _rms(x, w):
    ms = jnp.mean(x * x, axis=-1, keepdims=True)
    return x * lax.rsqrt(ms + NORM_EPS) * w


def _in_proj_kernel(x_ref, nw_ref, w_ref, h_ref, u_ref, xl_ref, yl_ref, *, n_u, n_x):
    j = pl.program_id(1)

    @pl.when(j == 0)
    def _():
        h_ref[...] = _rms(x_ref[...], nw_ref[...]).astype(BF16)

    p = jnp.dot(h_ref[...], w_ref[...], preferred_element_type=F32)

    @pl.when(j < n_u)
    def _():
        u_ref[...] = p

    @pl.when(jnp.logical_and(j >= n_u, j < n_u + n_x))
    def _():
        xl_ref[...] = p

    @pl.when(j >= n_u + n_x)
    def _():
        yl_ref[...] = p


def _in_proj(x, norm_w, w, s5_w, lru_w, tm=1024, tn=512):
    seq, d = x.shape
    n_u, n_x = s5_w // tn, lru_w // tn
    n_tiles = w.shape[1] // tn
    kern = functools.partial(_in_proj_kernel, n_u=n_u, n_x=n_x)
    return pl.pallas_call(
        kern,
        grid=(seq // tm, n_tiles),
        in_specs=[
            pl.BlockSpec((tm, d), lambda i, j: (i, 0)),
            pl.BlockSpec((1, d), lambda i, j: (0, 0)),
            pl.BlockSpec((d, tn), lambda i, j: (0, j)),
        ],
        out_specs=[
            pl.BlockSpec((tm, d), lambda i, j: (i, 0)),
            pl.BlockSpec((tm, tn), lambda i, j: (i, jnp.minimum(j, n_u - 1))),
            pl.BlockSpec((tm, tn), lambda i, j: (i, jnp.clip(j - n_u, 0, n_x - 1))),
            pl.BlockSpec((tm, tn), lambda i, j: (i, jnp.clip(j - n_u - n_x, 0, n_x - 1))),
        ],
        out_shape=[
            jax.ShapeDtypeStruct((seq, d), BF16),
            jax.ShapeDtypeStruct((seq, s5_w), F32),
            jax.ShapeDtypeStruct((seq, lru_w), F32),
            jax.ShapeDtypeStruct((seq, lru_w), F32),
        ],
        compiler_params=_params(2),
        name="in_proj",
    )(x, norm_w, w)


def _s5_kernel(u_ref, t_ref, e_ref, m_ref, a_ref, d_ref, y_ref, e_scr, s_scr, *, pairs):
    n_chunks = u_ref.shape[0]
    cw = 2 * S5_GROUP * S5_CHUNK
    sw = 2 * LANES
    for p in range(pairs):
        e_scr[:, p * sw:(p + 1) * sw] = jnp.dot(
            u_ref[:, p * cw:(p + 1) * cw], e_ref[p], precision=HIGHEST,
            preferred_element_type=F32)

    row = lax.broadcasted_iota(jnp.int32, (SUBLANES, LANES), 0)

    def body(t, carry):
        r0 = pl.multiple_of(t * SUBLANES, SUBLANES)
        new = []
        for p in range(pairs):
            c_re, c_im = carry[2 * p], carry[2 * p + 1]
            re_cols = slice(p * sw, p * sw + LANES)
            im_cols = slice(p * sw + LANES, (p + 1) * sw)
            x_re = e_scr[pl.ds(r0, SUBLANES), re_cols]
            x_im = e_scr[pl.ds(r0, SUBLANES), im_cols]
            for s in (1, 2, 4):
                p_re = a_ref[p, s - 1:s, :]
                p_im = a_ref[p, SUBLANES + s - 1:SUBLANES + s, :]
                keep = row >= s
                s_re = jnp.where(keep, pltpu.roll(x_re, s, 0), 0.0)
                s_im = jnp.where(keep, pltpu.roll(x_im, s, 0), 0.0)
                x_re, x_im = (x_re + p_re * s_re - p_im * s_im,
                              x_im + p_re * s_im + p_im * s_re)
            t_re = a_ref[p, 0:SUBLANES, :]
            t_im = a_ref[p, SUBLANES:2 * SUBLANES, :]
            x_re, x_im = (x_re + t_re * c_re - t_im * c_im,
                          x_im + t_re * c_im + t_im * c_re)
            s_scr[pl.ds(r0, SUBLANES), re_cols] = jnp.where(row >= 1, pltpu.roll(x_re, 1, 0), c_re)
            s_scr[pl.ds(r0, SUBLANES), im_cols] = jnp.where(row >= 1, pltpu.roll(x_im, 1, 0), c_im)
            new += [x_re[SUBLANES - 1:SUBLANES, :], x_im[SUBLANES - 1:SUBLANES, :]]
        return tuple(new)

    zero = jnp.zeros((1, LANES), F32)
    lax.fori_loop(0, n_chunks // SUBLANES, body, (zero,) * (2 * pairs))

    for p in range(pairs):
        up = u_ref[:, p * cw:(p + 1) * cw]
        y = jnp.dot(up, t_ref[p], precision=HIGHEST, preferred_element_type=F32)
        y = y + jnp.dot(s_scr[:, p * sw:(p + 1) * sw], m_ref[p], precision=HIGHEST,
                        preferred_element_type=F32)
        y = y + d_ref[:, p * cw:(p + 1) * cw] * up
        y_ref[:, p * cw:(p + 1) * cw] = jax.nn.gelu(y)


def _s5(ut, t_pair, e_pair, m_pair, a_pair, d_t, pairs=2):
    n_chunks, width = ut.shape
    cw = 2 * S5_GROUP * S5_CHUNK
    sw = 2 * LANES
    n_pairs = t_pair.shape[0]
    kern = functools.partial(_s5_kernel, pairs=pairs)
    return pl.pallas_call(
        kern,
        grid=(n_pairs // pairs,),
        in_specs=[
            pl.BlockSpec((n_chunks, pairs * cw), lambda q: (0, q)),
            pl.BlockSpec((pairs, cw, cw), lambda q: (q, 0, 0)),
            pl.BlockSpec((pairs, cw, sw), lambda q: (q, 0, 0)),
            pl.BlockSpec((pairs, sw, cw), lambda q: (q, 0, 0)),
            pl.BlockSpec((pairs, 2 * SUBLANES, LANES), lambda q: (q, 0, 0)),
            pl.BlockSpec((1, pairs * cw), lambda q: (0, q)),
        ],
        out_specs=pl.BlockSpec((n_chunks, pairs * cw), lambda q: (0, q)),
        out_shape=jax.ShapeDtypeStruct((n_chunks, width), F32),
        scratch_shapes=[
            pltpu.VMEM((n_chunks, pairs * sw), F32),
            pltpu.VMEM((n_chunks, pairs * sw), F32),
        ],
        compiler_params=_params(1),
        name="s5",
    )(ut, t_pair, e_pair, m_pair, a_pair, d_t)


def _s5_matrices(lam_re, lam_im, log_step, b_re, b_im, c_re, c_im, d_skip):
    g, p = lam_re.shape
    gs = S5_GROUP
    tc = S5_CHUNK
    lam = lax.complex(lam_re.astype(F32), lam_im.astype(F32))
    step = jnp.exp(log_step.astype(F32))[:, None]
    lam_bar = jnp.exp(lam * step)
    b_bar = ((lam_bar - 1.0) / lam)[..., None] * lax.complex(b_re.astype(F32), b_im.astype(F32))
    c_c = lax.complex(c_re.astype(F32), c_im.astype(F32))
    pows = [jnp.ones_like(lam_bar)]
    for _ in range(tc):
        pows.append(pows[-1] * lam_bar)
    pows = jnp.stack(pows)

    kmat = jnp.real(jnp.einsum('gop,tgp,gpc->tgoc', c_c, pows[:tc], b_bar, precision=HIGHEST))
    tt = jnp.arange(tc)
    diff = tt[None, :] - tt[:, None]
    kg = kmat[jnp.clip(diff, 0, tc - 1)]
    kg = jnp.where((diff >= 0)[:, :, None, None, None], kg, 0.0)
    t_mat = kg.transpose(2, 0, 4, 1, 3).reshape(g, tc * gs, tc * gs)

    e_c = pows[:tc][::-1][:, :, :, None] * b_bar[None]
    e_c = e_c.transpose(1, 0, 3, 2).reshape(g, tc * gs, p)
    m_c = c_c.transpose(0, 2, 1)[:, :, None, :] * pows[1:].transpose(1, 2, 0)[:, :, :, None]
    m_c = m_c.reshape(g, p, tc * gs)
    a_c = pows[tc]

    n_pairs = g // 2
    cw = 2 * gs * tc
    z = jnp.zeros((n_pairs, tc * gs, tc * gs), F32)
    t0, t1 = t_mat[0::2], t_mat[1::2]
    t_pair = jnp.concatenate([jnp.concatenate([t0, z], axis=2),
                              jnp.concatenate([z, t1], axis=2)], axis=1)
    ze = jnp.zeros((n_pairs, tc * gs, p), F32)
    er, ei = jnp.real(e_c), jnp.imag(e_c)
    e_pair = jnp.concatenate([
        jnp.concatenate([er[0::2], ze, ei[0::2], ze], axis=2),
        jnp.concatenate([ze, er[1::2], ze, ei[1::2]], axis=2)], axis=1)
    zm = jnp.zeros((n_pairs, p, tc * gs), F32)
    mr, mi = jnp.real(m_c), -jnp.imag(m_c)
    m_pair = jnp.concatenate([
        jnp.concatenate([mr[0::2], zm], axis=2),
        jnp.concatenate([zm, mr[1::2]], axis=2),
        jnp.concatenate([mi[0::2], zm], axis=2),
        jnp.concatenate([zm, mi[1::2]], axis=2)], axis=1)
    a_pows = [a_c]
    for _ in range(SUBLANES - 1):
        a_pows.append(a_pows[-1] * a_c)
    a_pows = jnp.stack(a_pows, axis=1)
    ar, ai = jnp.real(a_pows), jnp.imag(a_pows)
    a_pair = jnp.concatenate([jnp.concatenate([ar[0::2], ar[1::2]], axis=2),
                              jnp.concatenate([ai[0::2], ai[1::2]], axis=2)], axis=1)
    d_t = jnp.broadcast_to(d_skip.astype(F32).reshape(g, 1, gs), (g, tc, gs)).reshape(1, g * tc * gs)
    del cw
    return t_pair, e_pair, m_pair, a_pair, d_t


def _rglru_kernel(xl_ref, yl_ref, cw_ref, cb_ref, w_ref, ba_ref, bx_ref, nls_ref, o_ref,
                  xpad, a_scr, b_scr, h_scr, *, heads, head_dim):
    i = pl.program_id(0)
    tm, width = xl_ref.shape
    hist = SUBLANES

    @pl.when(i == 0)
    def _():
        xpad[0:hist, :] = jnp.zeros((hist, width), F32)
        h_scr[...] = jnp.zeros_like(h_scr)

    @pl.when(i > 0)
    def _():
        xpad[0:hist, :] = xpad[tm:tm + hist, :]

    xpad[hist:hist + tm, :] = xl_ref[...]

    first = jnp.logical_and(lax.broadcasted_iota(jnp.int32, (tm, head_dim), 0) == 0, i == 0)
    for hd in range(heads):
        cs = slice(hd * head_dim, (hd + 1) * head_dim)
        xc = cb_ref[:, cs]
        for k in range(CONV_WIDTH):
            off = hist - (CONV_WIDTH - 1) + k
            xc = xc + xpad[off:off + tm, cs] * cw_ref[k:k + 1, cs]
        g = jnp.dot(xc.astype(BF16), w_ref[hd], preferred_element_type=F32)
        r = jax.nn.sigmoid(g[:, :head_dim] + ba_ref[:, cs])
        gi = jax.nn.sigmoid(g[:, head_dim:] + bx_ref[:, cs])
        log_a = nls_ref[:, cs] * r
        a = jnp.exp(log_a)
        mult = jnp.sqrt(-jnp.tanh(log_a) * (a * a + 1.0))
        mult = jnp.where(first, 1.0, mult)
        a_scr[:, cs] = a
        b_scr[:, cs] = xc * gi * mult

    row = lax.broadcasted_iota(jnp.int32, (SUBLANES, width), 0)

    def body(t, h_prev):
        r0 = pl.multiple_of(t * SUBLANES, SUBLANES)
        a = a_scr[pl.ds(r0, SUBLANES), :]
        b = b_scr[pl.ds(r0, SUBLANES), :]
        for s in (1, 2, 4):
            keep = row >= s
            b = b + a * jnp.where(keep, pltpu.roll(b, s, 0), 0.0)
            a = a * jnp.where(keep, pltpu.roll(a, s, 0), 1.0)
        h = b + a * h_prev
        y = yl_ref[pl.ds(r0, SUBLANES), :]
        o_ref[pl.ds(r0, SUBLANES), :] = (h * jax.nn.gelu(y)).astype(o_ref.dtype)
        return h[SUBLANES - 1:SUBLANES, :]

    h_scr[...] = lax.fori_loop(0, tm // SUBLANES, body, h_scr[...])


def _rglru(xl, yl, conv_w, conv_b, w_cat, b_a, b_x, nls, heads, head_dim, tm=512):
    seq, width = xl.shape
    kern = functools.partial(_rglru_kernel, heads=heads, head_dim=head_dim)
    row = lambda i: (0, 0)
    return pl.pallas_call(
        kern,
        grid=(seq // tm,),
        in_specs=[
            pl.BlockSpec((tm, width), lambda i: (i, 0)),
            pl.BlockSpec((tm, width), lambda i: (i, 0)),
            pl.BlockSpec((CONV_WIDTH, width), row),
            pl.BlockSpec((1, width), row),
            pl.BlockSpec((heads, head_dim, 2 * head_dim), lambda i: (0, 0, 0)),
            pl.BlockSpec((1, width), row),
            pl.BlockSpec((1, width), row),
            pl.BlockSpec((1, width), row),
        ],
        out_specs=pl.BlockSpec((tm, width), lambda i: (i, 0)),
        out_shape=jax.ShapeDtypeStruct((seq, width), BF16),
        scratch_shapes=[
            pltpu.VMEM((tm + 2 * SUBLANES, width), F32),
            pltpu.VMEM((tm, width), F32),
            pltpu.VMEM((tm, width), F32),
            pltpu.VMEM((1, width), F32),
        ],
        compiler_params=_params(1),
        name="rglru",
    )(xl, yl, conv_w, conv_b, w_cat, b_a, b_x, nls)


def _merge_kernel(h_ref, y_ref, ol_ref, wglu_ref, bglu_ref, wgs_ref, wgl_ref, bgs_ref, bgl_ref,
                  wps_ref, wpl_ref, o_ref, os_scr):
    j = pl.program_id(1)

    @pl.when(j == 0)
    def _():
        y = y_ref[...]
        glu = jax.nn.sigmoid(
            jnp.dot(y.astype(BF16), wglu_ref[...], preferred_element_type=F32) + bglu_ref[...])
        os_scr[...] = (y * glu).astype(BF16)

    h = h_ref[...]
    g_s5 = jax.nn.sigmoid(jnp.dot(h, wgs_ref[...], preferred_element_type=F32) + bgs_ref[...])
    g_lru = jax.nn.sigmoid(jnp.dot(h, wgl_ref[...], preferred_element_type=F32) + bgl_ref[...])
    br_s5 = jnp.dot(os_scr[...], wps_ref[...], preferred_element_type=F32)
    br_lru = jnp.dot(ol_ref[...], wpl_ref[...], preferred_element_type=F32)
    o_ref[...] = (g_s5 * br_s5 + g_lru * br_lru).astype(o_ref.dtype)


def _merge(h, y_s5, out_lru, w_glu, b_glu, wg_s5, wg_lru, bg_s5, bg_lru, wp_s5, wp_lru,
           tm=1024, tn=512):
    seq, d = h.shape
    s5_w, lru_w = y_s5.shape[1], out_lru.shape[1]
    const = lambda i, j: (0, 0)
    return pl.pallas_call(
        _merge_kernel,
        grid=(seq // tm, d // tn),
        in_specs=[
            pl.BlockSpec((tm, d), lambda i, j: (i, 0)),
            pl.BlockSpec((tm, s5_w), lambda i, j: (i, 0)),
            pl.BlockSpec((tm, lru_w), lambda i, j: (i, 0)),
            pl.BlockSpec((s5_w, s5_w), const),
            pl.BlockSpec((1, s5_w), const),
            pl.BlockSpec((d, tn), lambda i, j: (0, j)),
            pl.BlockSpec((d, tn), lambda i, j: (0, j)),
            pl.BlockSpec((1, tn), lambda i, j: (0, j)),
            pl.BlockSpec((1, tn), lambda i, j: (0, j)),
            pl.BlockSpec((s5_w, tn), lambda i, j: (0, j)),
            pl.BlockSpec((lru_w, tn), lambda i, j: (0, j)),
        ],
        out_specs=pl.BlockSpec((tm, tn), lambda i, j: (i, j)),
        out_shape=jax.ShapeDtypeStruct((seq, d), BF16),
        scratch_shapes=[pltpu.VMEM((tm, s5_w), BF16)],
        compiler_params=_params(2),
        name="merge",
    )(h, y_s5, out_lru, w_glu, b_glu, wg_s5, wg_lru, bg_s5, bg_lru, wp_s5, wp_lru)


def _out_proj_kernel(m_ref, w_ref, x_ref, o_ref):
    o_ref[...] = x_ref[...] + jnp.dot(m_ref[...], w_ref[...], preferred_element_type=F32)


def _out_proj(merged, w_out, x, tm=1024, tn=512):
    seq, d = x.shape
    return pl.pallas_call(
        _out_proj_kernel,
        grid=(seq // tm, d // tn),
        in_specs=[
            pl.BlockSpec((tm, d), lambda i, j: (i, 0)),
            pl.BlockSpec((d, tn), lambda i, j: (0, j)),
            pl.BlockSpec((tm, tn), lambda i, j: (i, j)),
        ],
        out_specs=pl.BlockSpec((tm, tn), lambda i, j: (i, j)),
        out_shape=jax.ShapeDtypeStruct((seq, d), F32),
        compiler_params=_params(2),
        name="out_proj",
    )(merged, w_out, x)


def _route(logits, n_experts, n_groups):
    per_group = n_experts // n_groups
    lane = lax.broadcasted_iota(jnp.int32, logits.shape, 1)
    big = jnp.int32(LANES)
    neg = jnp.float32(-jnp.inf)
    is_g = jnp.logical_and(lane >= n_experts, lane < n_experts + n_groups)
    lg = jnp.where(is_g, logits, neg)
    g_max = jnp.max(lg, axis=-1, keepdims=True)
    g_lane = jnp.min(jnp.where(lg == g_max, lane, big), axis=-1, keepdims=True)
    g_top = 1.0 / jnp.sum(jnp.where(is_g, jnp.exp(lg - g_max), 0.0), axis=-1, keepdims=True)
    g_idx = g_lane - n_experts
    in_group = jnp.logical_and(lane >= g_idx * per_group, lane < (g_idx + 1) * per_group)
    le = jnp.where(in_group, logits, neg)
    m1 = jnp.max(le, axis=-1, keepdims=True)
    i1 = jnp.min(jnp.where(le == m1, lane, big), axis=-1, keepdims=True)
    le2 = jnp.where(lane == i1, neg, le)
    m2 = jnp.max(le2, axis=-1, keepdims=True)
    i2 = jnp.min(jnp.where(le2 == m2, lane, big), axis=-1, keepdims=True)
    r = jnp.exp(m2 - m1)
    w1 = g_top / (1.0 + r)
    w2 = g_top * r / (1.0 + r)
    return jnp.where(lane == i1, w1, 0.0) + jnp.where(lane == i2, w2, 0.0)


def _moe_kernel(x_ref, nw_ref, wr_ref, br_ref, wg_ref, wu_ref, wd_ref, fw_ref, o_ref,
                ht_scr, comb_scr, acc_scr, *, n_experts, n_groups, final_norm):
    e = pl.program_id(1)

    @pl.when(e == 0)
    def _():
        ht = _rms(x_ref[...], nw_ref[...])
        ht_scr[...] = ht.astype(BF16)
        logits = jnp.dot(ht, wr_ref[...], precision=HIGHEST,
                         preferred_element_type=F32) + br_ref[...]
        comb_scr[...] = _route(logits, n_experts, n_groups)
        acc_scr[...] = jnp.zeros_like(acc_scr)

    ht = ht_scr[...]
    gate = jnp.dot(ht, wg_ref[0], preferred_element_type=F32)
    up = jnp.dot(ht, wu_ref[0], preferred_element_type=F32)
    lane = lax.broadcasted_iota(jnp.int32, comb_scr.shape, 1)
    c_e = jnp.sum(jnp.where(lane == e, comb_scr[...], 0.0), axis=-1, keepdims=True)
    act = (jax.nn.silu(gate) * up * c_e).astype(BF16)
    acc_scr[...] += jnp.dot(act, wd_ref[0], preferred_element_type=F32)

    @pl.when(e == n_experts - 1)
    def _():
        out = x_ref[...] + acc_scr[...]
        o_ref[...] = _rms(out, fw_ref[...]) if final_norm else out


def _moe(x2, norm_w, w_router, b_router, w_gate, w_up, w_down, final_w, n_groups, final_norm,
         tm=512):
    seq, d = x2.shape
    n_experts, _, ff = w_gate.shape
    kern = functools.partial(_moe_kernel, n_experts=n_experts, n_groups=n_groups,
                             final_norm=final_norm)
    const = lambda i, e: (0, 0)
    return pl.pallas_call(
        kern,
        grid=(seq // tm, n_experts),
        in_specs=[
            pl.BlockSpec((tm, d), lambda i, e: (i, 0)),
            pl.BlockSpec((1, d), const),
            pl.BlockSpec((d, LANES), const),
            pl.BlockSpec((1, LANES), const),
            pl.BlockSpec((1, d, ff), lambda i, e: (e, 0, 0)),
            pl.BlockSpec((1, d, ff), lambda i, e: (e, 0, 0)),
            pl.BlockSpec((1, ff, d), lambda i, e: (e, 0, 0)),
            pl.BlockSpec((1, d), const),
        ],
        out_specs=pl.BlockSpec((tm, d), lambda i, e: (i, 0)),
        out_shape=jax.ShapeDtypeStruct((seq, d), F32),
        scratch_shapes=[
            pltpu.VMEM((tm, d), BF16),
            pltpu.VMEM((tm, LANES), F32),
            pltpu.VMEM((tm, d), F32),
        ],
        compiler_params=_params(2),
        name="moe",
    )(x2, norm_w, w_router, b_router, w_gate, w_up, w_down, final_w)


def kernel(x, norm_mix_w, w_in, b_gate, s5_lam_re, s5_lam_im, s5_log_step, s5_b_re, s5_b_im, s5_c_re, s5_c_im, s5_d, s5_w_glu, s5_b_glu, lru_conv_w, lru_conv_b, lru_w_a, lru_b_a, lru_w_x, lru_b_x, lru_lambda, w_proj_s5, w_proj_lru, w_out, norm_ffn_w, w_router_group, b_router_group, w_router_expert, b_router_expert, w_e_gate, w_e_up, w_e_down, norm_final_w):
    depth = w_in.shape[0]
    bsz, seq, d = x.shape
    s5_w = s5_w_glu.shape[-1]
    lru_w = lru_conv_b.shape[-1]
    heads, head_dim = lru_w_a.shape[1], lru_w_a.shape[2]
    n_groups = w_router_group.shape[-1]
    n_experts = w_router_expert.shape[-1]
    s5_groups = s5_w // S5_GROUP
    n_chunks = seq // S5_CHUNK
    o3 = s5_w + 2 * lru_w

    outs = []
    for b in range(bsz):
        xb = x[b]
        for l in range(depth):
            row = lambda v: v.astype(F32).reshape(1, -1)
            h, u, xl, yl = _in_proj(xb, row(norm_mix_w[l]), w_in[l][:, :o3].astype(BF16), s5_w, lru_w)

            t_pair, e_pair, m_pair, a_pair, d_t = _s5_matrices(
                s5_lam_re[l], s5_lam_im[l], s5_log_step[l], s5_b_re[l], s5_b_im[l],
                s5_c_re[l], s5_c_im[l], s5_d[l])
            ut = u.reshape(n_chunks, S5_CHUNK, s5_groups, S5_GROUP).transpose(0, 2, 1, 3)
            yt = _s5(ut.reshape(n_chunks, S5_CHUNK * s5_w), t_pair, e_pair, m_pair, a_pair, d_t)
            y_s5 = yt.reshape(n_chunks, s5_groups, S5_CHUNK, S5_GROUP).transpose(0, 2, 1, 3)
            y_s5 = y_s5.reshape(seq, s5_w)

            w_cat = jnp.concatenate([lru_w_a[l], lru_w_x[l]], axis=-1).astype(BF16)
            nls = -LRU_C * jax.nn.softplus(-lru_lambda[l].astype(F32))
            out_lru = _rglru(xl, yl, lru_conv_w[l].astype(F32), row(lru_conv_b[l]), w_cat,
                             row(lru_b_a[l]), row(lru_b_x[l]), row(nls), heads, head_dim)

            wg = w_in[l][:, o3:].astype(BF16)
            bg = b_gate[l].astype(F32)
            merged = _merge(h, y_s5, out_lru, s5_w_glu[l].astype(BF16), row(s5_b_glu[l]),
                            wg[:, :d], wg[:, d:], row(bg[:d]), row(bg[d:]),
                            w_proj_s5[l].astype(BF16), w_proj_lru[l].astype(BF16))
            x2 = _out_proj(merged, w_out[l].astype(BF16), xb)

            pad = LANES - n_experts - n_groups
            w_router = jnp.concatenate([w_router_expert[l].astype(F32), w_router_group[l].astype(F32),
                                        jnp.zeros((d, pad), F32)], axis=1)
            b_router = jnp.concatenate([b_router_expert[l].astype(F32), b_router_group[l].astype(F32),
                                        jnp.zeros((pad,), F32)]).reshape(1, LANES)
            xb = _moe(x2, row(norm_ffn_w[l]), w_router, b_router, w_e_gate[l].astype(BF16),
                      w_e_up[l].astype(BF16), w_e_down[l].astype(BF16), row(norm_final_w),
                      n_groups, final_norm=(l == depth - 1))
        outs.append(xb)
    return jnp.stack(outs)
```

```python
import functools

import jax
import jax.numpy as jnp
from jax import lax
from jax.experimental import pallas as pl
from jax.experimental.pallas import tpu as pltpu

F32 = jnp.float32
BF16 = jnp.bfloat16
HIGHEST = lax.Precision.HIGHEST

NORM_EPS = 1e-6
LRU_C = 8.0
S5_GROUP = 16
CONV_WIDTH = 4
SUBLANES = 8
LANES = 128
S5_CHUNK = SUBLANES
S5_SET = LANES // S5_GROUP
VMEM_LIMIT = 56 * 1024 * 1024


def _params(n_axes, vmem=VMEM_LIMIT):
    return pltpu.CompilerParams(dimension_semantics=("arbitrary",) * n_axes,
                                vmem_limit_bytes=vmem)


def _rms(x, w):
    ms = jnp.mean(x * x, axis=-1, keepdims=True)
    return x * lax.rsqrt(ms + NORM_EPS) * w


def _in_proj_kernel(x_ref, nw_ref, w_ref, h_ref, u_ref, xl_ref, yl_ref, *, n_u, n_x):
    j = pl.program_id(1)

    @pl.when(j == 0)
    def _():
        h_ref[...] = _rms(x_ref[...], nw_ref[...]).astype(BF16)

    p = jnp.dot(h_ref[...], w_ref[...], preferred_element_type=F32)

    @pl.when(j < n_u)
    def _():
        u_ref[...] = p

    @pl.when(jnp.logical_and(j >= n_u, j < n_u + n_x))
    def _():
        xl_ref[...] = p

    @pl.when(j >= n_u + n_x)
    def _():
        yl_ref[...] = p


def _in_proj(x, norm_w, w, s5_w, lru_w, tm=1024, tn=512):
    seq, d = x.shape
    n_u, n_x = s5_w // tn, lru_w // tn
    kern = functools.partial(_in_proj_kernel, n_u=n_u, n_x=n_x)
    return pl.pallas_call(
        kern,
        grid=(seq // tm, n_u + 2 * n_x),
        in_specs=[
            pl.BlockSpec((tm, d), lambda i, j: (i, 0)),
            pl.BlockSpec((1, d), lambda i, j: (0, 0)),
            pl.BlockSpec((d, tn), lambda i, j: (0, j)),
        ],
        out_specs=[
            pl.BlockSpec((tm, d), lambda i, j: (i, 0)),
            pl.BlockSpec((tm, tn), lambda i, j: (i, jnp.minimum(j, n_u - 1))),
            pl.BlockSpec((tm, tn), lambda i, j: (i, jnp.clip(j - n_u, 0, n_x - 1))),
            pl.BlockSpec((tm, tn), lambda i, j: (i, jnp.clip(j - n_u - n_x, 0, n_x - 1))),
        ],
        out_shape=[
            jax.ShapeDtypeStruct((seq, d), BF16),
            jax.ShapeDtypeStruct((seq, s5_w), F32),
            jax.ShapeDtypeStruct((seq, lru_w), F32),
            jax.ShapeDtypeStruct((seq, lru_w), F32),
        ],
        compiler_params=_params(2),
        name="in_proj",
    )(x, norm_w, w)


def _s5_kernel(u_ref, t_ref, e_ref, m_ref, a_ref, d_ref, y_ref, x_scr, e_scr, s_scr, c_scr):
    tt = pl.program_id(1)
    n_chunks = x_scr.shape[0]
    half = e_scr.shape[1] // 2
    n_col = half // LANES

    @pl.when(tt == 0)
    def _():
        c_scr[...] = jnp.zeros_like(c_scr)

    for j in range(S5_CHUNK):
        x_scr[:, j * LANES:(j + 1) * LANES] = (
            u_ref[pl.ds(j, n_chunks, stride=S5_CHUNK), :].astype(BF16))

    e_scr[...] = jnp.dot(x_scr[...], e_ref[0], preferred_element_type=F32)

    row = lax.broadcasted_iota(jnp.int32, (SUBLANES, LANES), 0)

    def body(t, carry):
        r0 = pl.multiple_of(t * SUBLANES, SUBLANES)
        new = []
        for q in range(n_col):
            c_re, c_im = carry[2 * q], carry[2 * q + 1]
            re_cols = slice(q * LANES, (q + 1) * LANES)
            im_cols = slice(half + q * LANES, half + (q + 1) * LANES)
            x_re = e_scr[pl.ds(r0, SUBLANES), re_cols]
            x_im = e_scr[pl.ds(r0, SUBLANES), im_cols]
            for s in (1, 2, 4):
                p_re = a_ref[0, s - 1:s, re_cols]
                p_im = a_ref[0, SUBLANES + s - 1:SUBLANES + s, re_cols]
                keep = row >= s
                s_re = jnp.where(keep, pltpu.roll(x_re, s, 0), 0.0)
                s_im = jnp.where(keep, pltpu.roll(x_im, s, 0), 0.0)
                x_re, x_im = (x_re + p_re * s_re - p_im * s_im,
                              x_im + p_re * s_im + p_im * s_re)
            t_re = a_ref[0, 0:SUBLANES, re_cols]
            t_im = a_ref[0, SUBLANES:2 * SUBLANES, re_cols]
            x_re, x_im = (x_re + t_re * c_re - t_im * c_im,
                          x_im + t_re * c_im + t_im * c_re)
            s_scr[pl.ds(r0, SUBLANES), re_cols] = jnp.where(row >= 1, pltpu.roll(x_re, 1, 0), c_re)
            s_scr[pl.ds(r0, SUBLANES), im_cols] = jnp.where(row >= 1, pltpu.roll(x_im, 1, 0), c_im)
            new += [x_re[SUBLANES - 1:SUBLANES, :], x_im[SUBLANES - 1:SUBLANES, :]]
        return tuple(new)

    init = []
    for q in range(n_col):
        init += [c_scr[0:1, q * LANES:(q + 1) * LANES],
                 c_scr[0:1, half + q * LANES:half + (q + 1) * LANES]]
    last = lax.fori_loop(0, n_chunks // SUBLANES, body, tuple(init))
    for q in range(n_col):
        c_scr[0:1, q * LANES:(q + 1) * LANES] = last[2 * q]
        c_scr[0:1, half + q * LANES:half + (q + 1) * LANES] = last[2 * q + 1]

    y = jnp.dot(x_scr[...], t_ref[0], preferred_element_type=F32)
    y = y + jnp.dot(s_scr[...].astype(BF16), m_ref[0], preferred_element_type=F32)
    for j in range(S5_CHUNK):
        rows = pl.ds(j, n_chunks, stride=S5_CHUNK)
        yj = y[:, j * LANES:(j + 1) * LANES] + d_ref[...] * u_ref[rows, :]
        y_ref[rows, :] = jax.nn.gelu(yj)


def _s5(u, t_set, e_set, m_set, a_set, d_skip, tile=4096):
    seq, width = u.shape
    n_sets = width // LANES
    n_chunks = tile // S5_CHUNK
    kdim = S5_CHUNK * LANES
    sdim = e_set.shape[2]
    return pl.pallas_call(
        _s5_kernel,
        grid=(n_sets, seq // tile),
        in_specs=[
            pl.BlockSpec((tile, LANES), lambda s, t: (t, s)),
            pl.BlockSpec((1, kdim, kdim), lambda s, t: (s, 0, 0)),
            pl.BlockSpec((1, kdim, sdim), lambda s, t: (s, 0, 0)),
            pl.BlockSpec((1, sdim, kdim), lambda s, t: (s, 0, 0)),
            pl.BlockSpec((1, 2 * SUBLANES, sdim // 2), lambda s, t: (s, 0, 0)),
            pl.BlockSpec((1, LANES), lambda s, t: (0, s)),
        ],
        out_specs=pl.BlockSpec((tile, LANES), lambda s, t: (t, s)),
        out_shape=jax.ShapeDtypeStruct((seq, width), F32),
        scratch_shapes=[
            pltpu.VMEM((n_chunks, kdim), BF16),
            pltpu.VMEM((n_chunks, sdim), F32),
            pltpu.VMEM((n_chunks, sdim), F32),
            pltpu.VMEM((SUBLANES, sdim), F32),
        ],
        compiler_params=_params(2),
        name="s5",
    )(u, t_set, e_set, m_set, a_set, d_skip)


def _cmul(a, b):
    return a[0] * b[0] - a[1] * b[1], a[0] * b[1] + a[1] * b[0]


def _s5_matrices(lam_re, lam_im, log_step, b_re, b_im, c_re, c_im):
    g, p = lam_re.shape
    gs, tc, ns = S5_GROUP, S5_CHUNK, S5_SET
    n_sets = g // ns
    lam = (lam_re.astype(F32), lam_im.astype(F32))
    step = jnp.exp(log_step.astype(F32))[:, None]
    mag = jnp.exp(lam[0] * step)
    lam_bar = (mag * jnp.cos(lam[1] * step), mag * jnp.sin(lam[1] * step))
    den = lam[0] * lam[0] + lam[1] * lam[1]
    coef = _cmul((lam_bar[0] - 1.0, lam_bar[1]), (lam[0] / den, -lam[1] / den))
    b_c = (b_re.astype(F32), b_im.astype(F32))
    b_bar = _cmul((coef[0][..., None], coef[1][..., None]), b_c)
    c_c = (c_re.astype(F32), c_im.astype(F32))

    pows = [(jnp.ones_like(lam_bar[0]), jnp.zeros_like(lam_bar[0]))]
    for _ in range(tc):
        pows.append(_cmul(pows[-1], lam_bar))
    pw = (jnp.stack([q[0] for q in pows]), jnp.stack([q[1] for q in pows]))

    w = _cmul((pw[0][:tc, :, :, None], pw[1][:tc, :, :, None]), (b_bar[0][None], b_bar[1][None]))
    kmat = (jnp.einsum('gop,tgpc->tgoc', c_c[0], w[0], precision=HIGHEST)
            - jnp.einsum('gop,tgpc->tgoc', c_c[1], w[1], precision=HIGHEST))
    eye = jnp.eye(ns, dtype=F32)
    kmat = kmat.reshape(tc, n_sets, ns, gs, gs)
    t_blocks = []
    for j in range(tc):
        kj = jnp.concatenate([jnp.zeros_like(kmat[:j]), kmat[:tc - j]], axis=0)
        t_blocks.append(jnp.einsum('tsgoc,gh->sgctho', kj, eye))
    t_set = jnp.stack(t_blocks, axis=1).reshape(n_sets, tc * ns * gs, tc * ns * gs)

    def e_half(x):
        x = x[::-1].reshape(tc, n_sets, ns, p, gs)
        return jnp.einsum('jsgpc,gh->sjgchp', x, eye).reshape(n_sets, tc * ns * gs, ns * p)
    e_set = jnp.concatenate([e_half(w[0]), e_half(w[1])], axis=2)

    m = _cmul((c_c[0].transpose(0, 2, 1)[None], c_c[1].transpose(0, 2, 1)[None]),
              (pw[0][1:, :, :, None], pw[1][1:, :, :, None]))

    def m_half(x):
        x = x.reshape(tc, n_sets, ns, p, gs)
        return jnp.einsum('tsgpo,gh->sgptho', x, eye).reshape(n_sets, ns * p, tc * ns * gs)
    m_set = jnp.concatenate([m_half(m[0]), m_half(-m[1])], axis=1)

    a_c = (pw[0][tc], pw[1][tc])
    a_pows = [a_c]
    for _ in range(SUBLANES - 1):
        a_pows.append(_cmul(a_pows[-1], a_c))
    a_set = jnp.concatenate([jnp.stack([q[0] for q in a_pows]), jnp.stack([q[1] for q in a_pows])])
    a_set = a_set.reshape(2 * SUBLANES, n_sets, ns * p).transpose(1, 0, 2)
    return t_set.astype(BF16), e_set.astype(BF16), m_set.astype(BF16), a_set


def _rglru_kernel(xl_ref, yl_ref, cw_ref, cb_ref, w_ref, ba_ref, bx_ref, nls_ref, o_ref,
                  xpad, a_scr, b_scr, h_scr, *, heads, head_dim):
    i = pl.program_id(0)
    tm, width = xl_ref.shape
    hist = SUBLANES

    @pl.when(i == 0)
    def _():
        xpad[0:hist, :] = jnp.zeros((hist, width), F32)
        h_scr[...] = jnp.zeros_like(h_scr)

    @pl.when(i > 0)
    def _():
        xpad[0:hist, :] = xpad[tm:tm + hist, :]

    xpad[hist:hist + tm, :] = xl_ref[...]

    first = jnp.logical_and(lax.broadcasted_iota(jnp.int32, (tm, head_dim), 0) == 0, i == 0)
    for hd in range(heads):
        cs = slice(hd * head_dim, (hd + 1) * head_dim)
        xc = cb_ref[:, cs]
        for k in range(CONV_WIDTH):
            off = hist - (CONV_WIDTH - 1) + k
            xc = xc + xpad[off:off + tm, cs] * cw_ref[k:k + 1, cs]
        g = jnp.dot(xc.astype(BF16), w_ref[hd], preferred_element_type=F32)
        r = jax.nn.sigmoid(g[:, :head_dim] + ba_ref[:, cs])
        gi = jax.nn.sigmoid(g[:, head_dim:] + bx_ref[:, cs])
        log_a = nls_ref[:, cs] * r
        a = jnp.exp(log_a)
        mult = jnp.sqrt(-jnp.tanh(log_a) * (a * a + 1.0))
        mult = jnp.where(first, 1.0, mult)
        a_scr[:, cs] = a
        b_scr[:, cs] = xc * gi * mult

    row = lax.broadcasted_iota(jnp.int32, (SUBLANES, width), 0)

    def body(t, h_prev):
        r0 = pl.multiple_of(t * SUBLANES, SUBLANES)
        a = a_scr[pl.ds(r0, SUBLANES), :]
        b = b_scr[pl.ds(r0, SUBLANES), :]
        for s in (1, 2, 4):
            keep = row >= s
            b = b + a * jnp.where(keep, pltpu.roll(b, s, 0), 0.0)
            a = a * jnp.where(keep, pltpu.roll(a, s, 0), 1.0)
        h = b + a * h_prev
        y = yl_ref[pl.ds(r0, SUBLANES), :]
        o_ref[pl.ds(r0, SUBLANES), :] = (h * jax.nn.gelu(y)).astype(o_ref.dtype)
        return h[SUBLANES - 1:SUBLANES, :]

    h_scr[...] = lax.fori_loop(0, tm // SUBLANES, body, h_scr[...])


def _rglru(xl, yl, conv_w, conv_b, w_cat, b_a, b_x, nls, heads, head_dim, tm=512):
    seq, width = xl.shape
    kern = functools.partial(_rglru_kernel, heads=heads, head_dim=head_dim)
    row = lambda i: (0, 0)
    return pl.pallas_call(
        kern,
        grid=(seq // tm,),
        in_specs=[
            pl.BlockSpec((tm, width), lambda i: (i, 0)),
            pl.BlockSpec((tm, width), lambda i: (i, 0)),
            pl.BlockSpec((CONV_WIDTH, width), row),
            pl.BlockSpec((1, width), row),
            pl.BlockSpec((heads, head_dim, 2 * head_dim), lambda i: (0, 0, 0)),
            pl.BlockSpec((1, width), row),
            pl.BlockSpec((1, width), row),
            pl.BlockSpec((1, width), row),
        ],
        out_specs=pl.BlockSpec((tm, width), lambda i: (i, 0)),
        out_shape=jax.ShapeDtypeStruct((seq, width), BF16),
        scratch_shapes=[
            pltpu.VMEM((tm + 2 * SUBLANES, width), F32),
            pltpu.VMEM((tm, width), F32),
            pltpu.VMEM((tm, width), F32),
            pltpu.VMEM((1, width), F32),
        ],
        compiler_params=_params(1),
        name="rglru",
    )(xl, yl, conv_w, conv_b, w_cat, b_a, b_x, nls)


def _merge_kernel(h_ref, y_ref, ol_ref, wglu_ref, bglu_ref, wgs_ref, wgl_ref, bgs_ref, bgl_ref,
                  wps_ref, wpl_ref, o_ref, os_scr):
    j = pl.program_id(1)

    @pl.when(j == 0)
    def _():
        y = y_ref[...]
        glu = jax.nn.sigmoid(
            jnp.dot(y.astype(BF16), wglu_ref[...], preferred_element_type=F32) + bglu_ref[...])
        os_scr[...] = (y * glu).astype(BF16)

    h = h_ref[...]
    g_s5 = jax.nn.sigmoid(jnp.dot(h, wgs_ref[...], preferred_element_type=F32) + bgs_ref[...])
    g_lru = jax.nn.sigmoid(jnp.dot(h, wgl_ref[...], preferred_element_type=F32) + bgl_ref[...])
    br_s5 = jnp.dot(os_scr[...], wps_ref[...], preferred_element_type=F32)
    br_lru = jnp.dot(ol_ref[...], wpl_ref[...], preferred_element_type=F32)
    o_ref[...] = (g_s5 * br_s5 + g_lru * br_lru).astype(o_ref.dtype)


def _merge(h, y_s5, out_lru, w_glu, b_glu, w_in, gate_col, b_gate, wp_s5, wp_lru,
           tm=1024, tn=512):
    seq, d = h.shape
    s5_w, lru_w = y_s5.shape[1], out_lru.shape[1]
    const = lambda i, j: (0, 0)
    c0, nd = gate_col // tn, d // tn
    return pl.pallas_call(
        _merge_kernel,
        grid=(seq // tm, nd),
        in_specs=[
            pl.BlockSpec((tm, d), lambda i, j: (i, 0)),
            pl.BlockSpec((tm, s5_w), lambda i, j: (i, 0)),
            pl.BlockSpec((tm, lru_w), lambda i, j: (i, 0)),
            pl.BlockSpec((s5_w, s5_w), const),
            pl.BlockSpec((1, s5_w), const),
            pl.BlockSpec((d, tn), lambda i, j: (0, c0 + j)),
            pl.BlockSpec((d, tn), lambda i, j: (0, c0 + nd + j)),
            pl.BlockSpec((1, tn), lambda i, j: (0, j)),
            pl.BlockSpec((1, tn), lambda i, j: (0, nd + j)),
            pl.BlockSpec((s5_w, tn), lambda i, j: (0, j)),
            pl.BlockSpec((lru_w, tn), lambda i, j: (0, j)),
        ],
        out_specs=pl.BlockSpec((tm, tn), lambda i, j: (i, j)),
        out_shape=jax.ShapeDtypeStruct((seq, d), BF16),
        scratch_shapes=[pltpu.VMEM((tm, s5_w), BF16)],
        compiler_params=_params(2),
        name="merge",
    )(h, y_s5, out_lru, w_glu, b_glu, w_in, w_in, b_gate, b_gate, wp_s5, wp_lru)


def _out_proj_kernel(m_ref, w_ref, x_ref, o_ref):
    o_ref[...] = x_ref[...] + jnp.dot(m_ref[...], w_ref[...], preferred_element_type=F32)


def _out_proj(merged, w_out, x, tm=1024, tn=512):
    seq, d = x.shape
    return pl.pallas_call(
        _out_proj_kernel,
        grid=(seq // tm, d // tn),
        in_specs=[
            pl.BlockSpec((tm, d), lambda i, j: (i, 0)),
            pl.BlockSpec((d, tn), lambda i, j: (0, j)),
            pl.BlockSpec((tm, tn), lambda i, j: (i, j)),
        ],
        out_specs=pl.BlockSpec((tm, tn), lambda i, j: (i, j)),
        out_shape=jax.ShapeDtypeStruct((seq, d), F32),
        compiler_params=_params(2),
        name="out_proj",
    )(merged, w_out, x)


def _route(logits, n_experts, n_groups):
    per_group = n_experts // n_groups
    lane = lax.broadcasted_iota(jnp.int32, logits.shape, 1)
    big = jnp.int32(LANES)
    neg = jnp.float32(-jnp.inf)
    is_g = jnp.logical_and(lane >= n_experts, lane < n_experts + n_groups)
    lg = jnp.where(is_g, logits, neg)
    g_max = jnp.max(lg, axis=-1, keepdims=True)
    g_lane = jnp.min(jnp.where(lg == g_max, lane, big), axis=-1, keepdims=True)
    g_top = 1.0 / jnp.sum(jnp.where(is_g, jnp.exp(lg - g_max), 0.0), axis=-1, keepdims=True)
    g_idx = g_lane - n_experts
    in_group = jnp.logical_and(lane >= g_idx * per_group, lane < (g_idx + 1) * per_group)
    le = jnp.where(in_group, logits, neg)
    m1 = jnp.max(le, axis=-1, keepdims=True)
    i1 = jnp.min(jnp.where(le == m1, lane, big), axis=-1, keepdims=True)
    le2 = jnp.where(lane == i1, neg, le)
    m2 = jnp.max(le2, axis=-1, keepdims=True)
    i2 = jnp.min(jnp.where(le2 == m2, lane, big), axis=-1, keepdims=True)
    r = jnp.exp(m2 - m1)
    w1 = g_top / (1.0 + r)
    w2 = g_top * r / (1.0 + r)
    return jnp.where(lane == i1, w1, 0.0) + jnp.where(lane == i2, w2, 0.0)


def _moe_kernel(x_ref, nw_ref, wr_ref, br_ref, wg_ref, wu_ref, wd_ref, fw_ref, o_ref,
                ht_scr, comb_scr, acc_scr, *, n_experts, n_groups, final_norm):
    e = pl.program_id(1)

    @pl.when(e == 0)
    def _():
        ht = _rms(x_ref[...], nw_ref[...])
        ht_scr[...] = ht.astype(BF16)
        logits = jnp.dot(ht, wr_ref[...], precision=HIGHEST,
                         preferred_element_type=F32) + br_ref[...]
        comb_scr[...] = _route(logits, n_experts, n_groups)
        acc_scr[...] = jnp.zeros_like(acc_scr)

    ht = ht_scr[...]
    gate = jnp.dot(ht, wg_ref[0], preferred_element_type=F32)
    up = jnp.dot(ht, wu_ref[0], preferred_element_type=F32)
    lane = lax.broadcasted_iota(jnp.int32, comb_scr.shape, 1)
    c_e = jnp.sum(jnp.where(lane == e, comb_scr[...], 0.0), axis=-1, keepdims=True)
    act = (jax.nn.silu(gate) * up * c_e).astype(BF16)
    acc_scr[...] += jnp.dot(act, wd_ref[0], preferred_element_type=F32)

    @pl.when(e == n_experts - 1)
    def _():
        out = x_ref[...] + acc_scr[...]
        o_ref[...] = _rms(out, fw_ref[...]) if final_norm else out


def _moe(x2, norm_w, w_router, b_router, w_gate, w_up, w_down, final_w, n_groups, final_norm,
         tm=512):
    seq, d = x2.shape
    n_experts, _, ff = w_gate.shape
    kern = functools.partial(_moe_kernel, n_experts=n_experts, n_groups=n_groups,
                             final_norm=final_norm)
    const = lambda i, e: (0, 0)
    return pl.pallas_call(
        kern,
        grid=(seq // tm, n_experts),
        in_specs=[
            pl.BlockSpec((tm, d), lambda i, e: (i, 0)),
            pl.BlockSpec((1, d), const),
            pl.BlockSpec((d, LANES), const),
            pl.BlockSpec((1, LANES), const),
            pl.BlockSpec((1, d, ff), lambda i, e: (e, 0, 0)),
            pl.BlockSpec((1, d, ff), lambda i, e: (e, 0, 0)),
            pl.BlockSpec((1, ff, d), lambda i, e: (e, 0, 0)),
            pl.BlockSpec((1, d), const),
        ],
        out_specs=pl.BlockSpec((tm, d), lambda i, e: (i, 0)),
        out_shape=jax.ShapeDtypeStruct((seq, d), F32),
        scratch_shapes=[
            pltpu.VMEM((tm, d), BF16),
            pltpu.VMEM((tm, LANES), F32),
            pltpu.VMEM((tm, d), F32),
        ],
        compiler_params=_params(2),
        name="moe",
    )(x2, norm_w, w_router, b_router, w_gate, w_up, w_down, final_w)


def kernel(x, norm_mix_w, w_in, b_gate, s5_lam_re, s5_lam_im, s5_log_step, s5_b_re, s5_b_im, s5_c_re, s5_c_im, s5_d, s5_w_glu, s5_b_glu, lru_conv_w, lru_conv_b, lru_w_a, lru_b_a, lru_w_x, lru_b_x, lru_lambda, w_proj_s5, w_proj_lru, w_out, norm_ffn_w, w_router_group, b_router_group, w_router_expert, b_router_expert, w_e_gate, w_e_up, w_e_down, norm_final_w):
    depth = w_in.shape[0]
    bsz, seq, d = x.shape
    s5_w = s5_w_glu.shape[-1]
    lru_w = lru_conv_b.shape[-1]
    heads, head_dim = lru_w_a.shape[1], lru_w_a.shape[2]
    n_groups = w_router_group.shape[-1]
    n_experts = w_router_expert.shape[-1]
    o3 = s5_w + 2 * lru_w

    outs = []
    for b in range(bsz):
        xb = x[b]
        for l in range(depth):
            row = lambda v: v.astype(F32).reshape(1, -1)
            w_in_l = w_in[l].astype(BF16)
            h, u, xl, yl = _in_proj(xb, row(norm_mix_w[l]), w_in_l, s5_w, lru_w)

            t_set, e_set, m_set, a_set = _s5_matrices(
                s5_lam_re[l], s5_lam_im[l], s5_log_step[l], s5_b_re[l], s5_b_im[l],
                s5_c_re[l], s5_c_im[l])
            y_s5 = _s5(u, t_set, e_set, m_set, a_set, row(s5_d[l]))

            w_cat = jnp.concatenate([lru_w_a[l], lru_w_x[l]], axis=-1).astype(BF16)
            nls = -LRU_C * jax.nn.softplus(-lru_lambda[l].astype(F32))
            out_lru = _rglru(xl, yl, lru_conv_w[l].astype(F32), row(lru_conv_b[l]), w_cat,
                             row(lru_b_a[l]), row(lru_b_x[l]), row(nls), heads, head_dim)

            merged = _merge(h, y_s5, out_lru, s5_w_glu[l].astype(BF16), row(s5_b_glu[l]),
                            w_in_l, o3, row(b_gate[l]),
                            w_proj_s5[l].astype(BF16), w_proj_lru[l].astype(BF16))
            x2 = _out_proj(merged, w_out[l].astype(BF16), xb)

            pad = LANES - n_experts - n_groups
            w_router = jnp.concatenate([w_router_expert[l].astype(F32), w_router_group[l].astype(F32),
                                        jnp.zeros((d, pad), F32)], axis=1)
            b_router = jnp.concatenate([b_router_expert[l].astype(F32), b_router_group[l].astype(F32),
                                        jnp.zeros((pad,), F32)]).reshape(1, LANES)
            xb = _moe(x2, row(norm_ffn_w[l]), w_router, b_router, w_e_gate[l].astype(BF16),
                      w_e_up[l].astype(BF16), w_e_down[l].astype(BF16), row(norm_final_w),
                      n_groups, final_norm=(l == depth - 1))
        outs.append(xb)
    return jnp.stack(outs)
```

```python
import functools

import jax
import jax.numpy as jnp
from jax import lax
from jax.experimental import pallas as pl
from jax.experimental.pallas import tpu as pltpu

F32 = jnp.float32
BF16 = jnp.bfloat16
HIGHEST = lax.Precision.HIGHEST

NORM_EPS = 1e-6
LRU_C = 8.0
S5_GROUP = 16
CONV_WIDTH = 4
SUBLANES = 8
LANES = 128
S5_CHUNK = SUBLANES
S5_SET = LANES // S5_GROUP
VMEM_LIMIT = 56 * 1024 * 1024
MOE_BLOCK = 256
COMBINE_TILE = 256


def _params(n_axes, vmem=VMEM_LIMIT):
    return pltpu.CompilerParams(dimension_semantics=("arbitrary",) * n_axes,
                                vmem_limit_bytes=vmem)


def _rms(x, w):
    ms = jnp.mean(x * x, axis=-1, keepdims=True)
    return x * lax.rsqrt(ms + NORM_EPS) * w


def _in_proj_kernel(x_ref, nw_ref, w_ref, h_ref, u_ref, xl_ref, yl_ref, *, n_u, n_x):
    j = pl.program_id(1)

    @pl.when(j == 0)
    def _():
        h_ref[...] = _rms(x_ref[...], nw_ref[...]).astype(BF16)

    p = jnp.dot(h_ref[...], w_ref[...], preferred_element_type=F32)

    @pl.when(j < n_u)
    def _():
        u_ref[...] = p

    @pl.when(jnp.logical_and(j >= n_u, j < n_u + n_x))
    def _():
        xl_ref[...] = p

    @pl.when(j >= n_u + n_x)
    def _():
        yl_ref[...] = p


def _in_proj(x, norm_w, w, s5_w, lru_w, tm=1024, tn=512):
    seq, d = x.shape
    n_u, n_x = s5_w // tn, lru_w // tn
    kern = functools.partial(_in_proj_kernel, n_u=n_u, n_x=n_x)
    return pl.pallas_call(
        kern,
        grid=(seq // tm, n_u + 2 * n_x),
        in_specs=[
            pl.BlockSpec((tm, d), lambda i, j: (i, 0)),
            pl.BlockSpec((1, d), lambda i, j: (0, 0)),
            pl.BlockSpec((d, tn), lambda i, j: (0, j)),
        ],
        out_specs=[
            pl.BlockSpec((tm, d), lambda i, j: (i, 0)),
            pl.BlockSpec((tm, tn), lambda i, j: (i, jnp.minimum(j, n_u - 1))),
            pl.BlockSpec((tm, tn), lambda i, j: (i, jnp.clip(j - n_u, 0, n_x - 1))),
            pl.BlockSpec((tm, tn), lambda i, j: (i, jnp.clip(j - n_u - n_x, 0, n_x - 1))),
        ],
        out_shape=[
            jax.ShapeDtypeStruct((seq, d), BF16),
            jax.ShapeDtypeStruct((seq, s5_w), F32),
            jax.ShapeDtypeStruct((seq, lru_w), F32),
            jax.ShapeDtypeStruct((seq, lru_w), F32),
        ],
        compiler_params=_params(2),
        name="in_proj",
    )(x, norm_w, w)


def _s5_kernel(u_ref, t_ref, e_ref, m_ref, a_ref, d_ref, y_ref, x_scr, e_scr, s_scr, c_scr):
    tt = pl.program_id(1)
    n_chunks = x_scr.shape[0]
    half = e_scr.shape[1] // 2
    n_col = half // LANES

    @pl.when(tt == 0)
    def _():
        c_scr[...] = jnp.zeros_like(c_scr)

    for j in range(S5_CHUNK):
        x_scr[:, j * LANES:(j + 1) * LANES] = (
            u_ref[pl.ds(j, n_chunks, stride=S5_CHUNK), :].astype(BF16))

    e_scr[...] = jnp.dot(x_scr[...], e_ref[0], preferred_element_type=F32)

    row = lax.broadcasted_iota(jnp.int32, (SUBLANES, LANES), 0)

    def body(t, carry):
        r0 = pl.multiple_of(t * SUBLANES, SUBLANES)
        new = []
        for q in range(n_col):
            c_re, c_im = carry[2 * q], carry[2 * q + 1]
            re_cols = slice(q * LANES, (q + 1) * LANES)
            im_cols = slice(half + q * LANES, half + (q + 1) * LANES)
            x_re = e_scr[pl.ds(r0, SUBLANES), re_cols]
            x_im = e_scr[pl.ds(r0, SUBLANES), im_cols]
            for s in (1, 2, 4):
                p_re = a_ref[0, s - 1:s, re_cols]
                p_im = a_ref[0, SUBLANES + s - 1:SUBLANES + s, re_cols]
                keep = row >= s
                s_re = jnp.where(keep, pltpu.roll(x_re, s, 0), 0.0)
                s_im = jnp.where(keep, pltpu.roll(x_im, s, 0), 0.0)
                x_re, x_im = (x_re + p_re * s_re - p_im * s_im,
                              x_im + p_re * s_im + p_im * s_re)
            t_re = a_ref[0, 0:SUBLANES, re_cols]
            t_im = a_ref[0, SUBLANES:2 * SUBLANES, re_cols]
            x_re, x_im = (x_re + t_re * c_re - t_im * c_im,
                          x_im + t_re * c_im + t_im * c_re)
            s_scr[pl.ds(r0, SUBLANES), re_cols] = jnp.where(row >= 1, pltpu.roll(x_re, 1, 0), c_re)
            s_scr[pl.ds(r0, SUBLANES), im_cols] = jnp.where(row >= 1, pltpu.roll(x_im, 1, 0), c_im)
            new += [x_re[SUBLANES - 1:SUBLANES, :], x_im[SUBLANES - 1:SUBLANES, :]]
        return tuple(new)

    init = []
    for q in range(n_col):
        init += [c_scr[0:1, q * LANES:(q + 1) * LANES],
                 c_scr[0:1, half + q * LANES:half + (q + 1) * LANES]]
    last = lax.fori_loop(0, n_chunks // SUBLANES, body, tuple(init))
    for q in range(n_col):
        c_scr[0:1, q * LANES:(q + 1) * LANES] = last[2 * q]
        c_scr[0:1, half + q * LANES:half + (q + 1) * LANES] = last[2 * q + 1]

    y = jnp.dot(x_scr[...], t_ref[0], preferred_element_type=F32)
    y = y + jnp.dot(s_scr[...].astype(BF16), m_ref[0], preferred_element_type=F32)
    for j in range(S5_CHUNK):
        rows = pl.ds(j, n_chunks, stride=S5_CHUNK)
        yj = y[:, j * LANES:(j + 1) * LANES] + d_ref[...] * u_ref[rows, :]
        y_ref[rows, :] = jax.nn.gelu(yj)


def _s5(u, t_set, e_set, m_set, a_set, d_skip, tile=4096):
    seq, width = u.shape
    n_sets = width // LANES
    n_chunks = tile // S5_CHUNK
    kdim = S5_CHUNK * LANES
    sdim = e_set.shape[2]
    return pl.pallas_call(
        _s5_kernel,
        grid=(n_sets, seq // tile),
        in_specs=[
            pl.BlockSpec((tile, LANES), lambda s, t: (t, s)),
            pl.BlockSpec((1, kdim, kdim), lambda s, t: (s, 0, 0)),
            pl.BlockSpec((1, kdim, sdim), lambda s, t: (s, 0, 0)),
            pl.BlockSpec((1, sdim, kdim), lambda s, t: (s, 0, 0)),
            pl.BlockSpec((1, 2 * SUBLANES, sdim // 2), lambda s, t: (s, 0, 0)),
            pl.BlockSpec((1, LANES), lambda s, t: (0, s)),
        ],
        out_specs=pl.BlockSpec((tile, LANES), lambda s, t: (t, s)),
        out_shape=jax.ShapeDtypeStruct((seq, width), F32),
        scratch_shapes=[
            pltpu.VMEM((n_chunks, kdim), BF16),
            pltpu.VMEM((n_chunks, sdim), F32),
            pltpu.VMEM((n_chunks, sdim), F32),
            pltpu.VMEM((SUBLANES, sdim), F32),
        ],
        compiler_params=_params(2),
        name="s5",
    )(u, t_set, e_set, m_set, a_set, d_skip)


def _cmul(a, b):
    return a[0] * b[0] - a[1] * b[1], a[0] * b[1] + a[1] * b[0]


def _s5_matrices(lam_re, lam_im, log_step, b_re, b_im, c_re, c_im):
    g, p = lam_re.shape
    gs, tc, ns = S5_GROUP, S5_CHUNK, S5_SET
    n_sets = g // ns
    lam = (lam_re.astype(F32), lam_im.astype(F32))
    step = jnp.exp(log_step.astype(F32))[:, None]
    mag = jnp.exp(lam[0] * step)
    lam_bar = (mag * jnp.cos(lam[1] * step), mag * jnp.sin(lam[1] * step))
    den = lam[0] * lam[0] + lam[1] * lam[1]
    coef = _cmul((lam_bar[0] - 1.0, lam_bar[1]), (lam[0] / den, -lam[1] / den))
    b_c = (b_re.astype(F32), b_im.astype(F32))
    b_bar = _cmul((coef[0][..., None], coef[1][..., None]), b_c)
    c_c = (c_re.astype(F32), c_im.astype(F32))

    pows = [(jnp.ones_like(lam_bar[0]), jnp.zeros_like(lam_bar[0]))]
    for _ in range(tc):
        pows.append(_cmul(pows[-1], lam_bar))
    pw = (jnp.stack([q[0] for q in pows]), jnp.stack([q[1] for q in pows]))

    w = _cmul((pw[0][:tc, :, :, None], pw[1][:tc, :, :, None]), (b_bar[0][None], b_bar[1][None]))
    kmat = (jnp.einsum('gop,tgpc->tgoc', c_c[0], w[0], precision=HIGHEST)
            - jnp.einsum('gop,tgpc->tgoc', c_c[1], w[1], precision=HIGHEST))
    eye = jnp.eye(ns, dtype=F32)
    kmat = kmat.reshape(tc, n_sets, ns, gs, gs)
    t_blocks = []
    for j in range(tc):
        kj = jnp.concatenate([jnp.zeros_like(kmat[:j]), kmat[:tc - j]], axis=0)
        t_blocks.append(jnp.einsum('tsgoc,gh->sgctho', kj, eye))
    t_set = jnp.stack(t_blocks, axis=1).reshape(n_sets, tc * ns * gs, tc * ns * gs)

    def e_half(x):
        x = x[::-1].reshape(tc, n_sets, ns, p, gs)
        return jnp.einsum('jsgpc,gh->sjgchp', x, eye).reshape(n_sets, tc * ns * gs, ns * p)
    e_set = jnp.concatenate([e_half(w[0]), e_half(w[1])], axis=2)

    m = _cmul((c_c[0].transpose(0, 2, 1)[None], c_c[1].transpose(0, 2, 1)[None]),
              (pw[0][1:, :, :, None], pw[1][1:, :, :, None]))

    def m_half(x):
        x = x.reshape(tc, n_sets, ns, p, gs)
        return jnp.einsum('tsgpo,gh->sgptho', x, eye).reshape(n_sets, ns * p, tc * ns * gs)
    m_set = jnp.concatenate([m_half(m[0]), m_half(-m[1])], axis=1)

    a_c = (pw[0][tc], pw[1][tc])
    a_pows = [a_c]
    for _ in range(SUBLANES - 1):
        a_pows.append(_cmul(a_pows[-1], a_c))
    a_set = jnp.concatenate([jnp.stack([q[0] for q in a_pows]), jnp.stack([q[1] for q in a_pows])])
    a_set = a_set.reshape(2 * SUBLANES, n_sets, ns * p).transpose(1, 0, 2)
    return t_set.astype(BF16), e_set.astype(BF16), m_set.astype(BF16), a_set


def _rglru_kernel(xl_ref, yl_ref, cw_ref, cb_ref, w_ref, ba_ref, bx_ref, nls_ref, o_ref,
                  xpad, a_scr, b_scr, h_scr, *, heads, head_dim):
    i = pl.program_id(0)
    tm, width = xl_ref.shape
    hist = SUBLANES

    @pl.when(i == 0)
    def _():
        xpad[0:hist, :] = jnp.zeros((hist, width), F32)
        h_scr[...] = jnp.zeros_like(h_scr)

    @pl.when(i > 0)
    def _():
        xpad[0:hist, :] = xpad[tm:tm + hist, :]

    xpad[hist:hist + tm, :] = xl_ref[...]

    first = jnp.logical_and(lax.broadcasted_iota(jnp.int32, (tm, head_dim), 0) == 0, i == 0)
    for hd in range(heads):
        cs = slice(hd * head_dim, (hd + 1) * head_dim)
        xc = cb_ref[:, cs]
        for k in range(CONV_WIDTH):
            off = hist - (CONV_WIDTH - 1) + k
            xc = xc + xpad[off:off + tm, cs] * cw_ref[k:k + 1, cs]
        g = jnp.dot(xc.astype(BF16), w_ref[hd], preferred_element_type=F32)
        r = jax.nn.sigmoid(g[:, :head_dim] + ba_ref[:, cs])
        gi = jax.nn.sigmoid(g[:, head_dim:] + bx_ref[:, cs])
        log_a = nls_ref[:, cs] * r
        a = jnp.exp(log_a)
        mult = jnp.sqrt(-jnp.tanh(log_a) * (a * a + 1.0))
        mult = jnp.where(first, 1.0, mult)
        a_scr[:, cs] = a
        b_scr[:, cs] = xc * gi * mult

    row = lax.broadcasted_iota(jnp.int32, (SUBLANES, width), 0)

    def body(t, h_prev):
        r0 = pl.multiple_of(t * SUBLANES, SUBLANES)
        a = a_scr[pl.ds(r0, SUBLANES), :]
        b = b_scr[pl.ds(r0, SUBLANES), :]
        for s in (1, 2, 4):
            keep = row >= s
            b = b + a * jnp.where(keep, pltpu.roll(b, s, 0), 0.0)
            a = a * jnp.where(keep, pltpu.roll(a, s, 0), 1.0)
        h = b + a * h_prev
        y = yl_ref[pl.ds(r0, SUBLANES), :]
        o_ref[pl.ds(r0, SUBLANES), :] = (h * jax.nn.gelu(y)).astype(o_ref.dtype)
        return h[SUBLANES - 1:SUBLANES, :]

    h_scr[...] = lax.fori_loop(0, tm // SUBLANES, body, h_scr[...])


def _rglru(xl, yl, conv_w, conv_b, w_cat, b_a, b_x, nls, heads, head_dim, tm=512):
    seq, width = xl.shape
    kern = functools.partial(_rglru_kernel, heads=heads, head_dim=head_dim)
    row = lambda i: (0, 0)
    return pl.pallas_call(
        kern,
        grid=(seq // tm,),
        in_specs=[
            pl.BlockSpec((tm, width), lambda i: (i, 0)),
            pl.BlockSpec((tm, width), lambda i: (i, 0)),
            pl.BlockSpec((CONV_WIDTH, width), row),
            pl.BlockSpec((1, width), row),
            pl.BlockSpec((heads, head_dim, 2 * head_dim), lambda i: (0, 0, 0)),
            pl.BlockSpec((1, width), row),
            pl.BlockSpec((1, width), row),
            pl.BlockSpec((1, width), row),
        ],
        out_specs=pl.BlockSpec((tm, width), lambda i: (i, 0)),
        out_shape=jax.ShapeDtypeStruct((seq, width), BF16),
        scratch_shapes=[
            pltpu.VMEM((tm + 2 * SUBLANES, width), F32),
            pltpu.VMEM((tm, width), F32),
            pltpu.VMEM((tm, width), F32),
            pltpu.VMEM((1, width), F32),
        ],
        compiler_params=_params(1),
        name="rglru",
    )(xl, yl, conv_w, conv_b, w_cat, b_a, b_x, nls)


def _merge_kernel(h_ref, y_ref, ol_ref, wglu_ref, bglu_ref, wgs_ref, wgl_ref, bgs_ref, bgl_ref,
                  wps_ref, wpl_ref, o_ref, os_scr):
    j = pl.program_id(1)

    @pl.when(j == 0)
    def _():
        y = y_ref[...]
        glu = jax.nn.sigmoid(
            jnp.dot(y.astype(BF16), wglu_ref[...], preferred_element_type=F32) + bglu_ref[...])
        os_scr[...] = (y * glu).astype(BF16)

    h = h_ref[...]
    g_s5 = jax.nn.sigmoid(jnp.dot(h, wgs_ref[...], preferred_element_type=F32) + bgs_ref[...])
    g_lru = jax.nn.sigmoid(jnp.dot(h, wgl_ref[...], preferred_element_type=F32) + bgl_ref[...])
    br_s5 = jnp.dot(os_scr[...], wps_ref[...], preferred_element_type=F32)
    br_lru = jnp.dot(ol_ref[...], wpl_ref[...], preferred_element_type=F32)
    o_ref[...] = (g_s5 * br_s5 + g_lru * br_lru).astype(o_ref.dtype)


def _merge(h, y_s5, out_lru, w_glu, b_glu, w_in, gate_col, b_gate, wp_s5, wp_lru,
           tm=1024, tn=512):
    seq, d = h.shape
    s5_w, lru_w = y_s5.shape[1], out_lru.shape[1]
    const = lambda i, j: (0, 0)
    c0, nd = gate_col // tn, d // tn
    return pl.pallas_call(
        _merge_kernel,
        grid=(seq // tm, nd),
        in_specs=[
            pl.BlockSpec((tm, d), lambda i, j: (i, 0)),
            pl.BlockSpec((tm, s5_w), lambda i, j: (i, 0)),
            pl.BlockSpec((tm, lru_w), lambda i, j: (i, 0)),
            pl.BlockSpec((s5_w, s5_w), const),
            pl.BlockSpec((1, s5_w), const),
            pl.BlockSpec((d, tn), lambda i, j: (0, c0 + j)),
            pl.BlockSpec((d, tn), lambda i, j: (0, c0 + nd + j)),
            pl.BlockSpec((1, tn), lambda i, j: (0, j)),
            pl.BlockSpec((1, tn), lambda i, j: (0, nd + j)),
            pl.BlockSpec((s5_w, tn), lambda i, j: (0, j)),
            pl.BlockSpec((lru_w, tn), lambda i, j: (0, j)),
        ],
        out_specs=pl.BlockSpec((tm, tn), lambda i, j: (i, j)),
        out_shape=jax.ShapeDtypeStruct((seq, d), BF16),
        scratch_shapes=[pltpu.VMEM((tm, s5_w), BF16)],
        compiler_params=_params(2),
        name="merge",
    )(h, y_s5, out_lru, w_glu, b_glu, w_in, w_in, b_gate, b_gate, wp_s5, wp_lru)


ROW_TILE_W = LANES
ROW_TILE_ROWS = 16


def _to_row_tiles(ref, base, n, x):
    for s in range(ROW_TILE_ROWS):
        ref[pl.ds(base + s, n, stride=ROW_TILE_ROWS), :] = x[:, s * ROW_TILE_W:(s + 1) * ROW_TILE_W]


def _from_row_tiles(ref, base, n, s):
    return ref[pl.ds(base + s, n, stride=ROW_TILE_ROWS), :]


def _route(logits, n_experts, n_groups):
    per_group = n_experts // n_groups
    lane = lax.broadcasted_iota(jnp.int32, logits.shape, 1)
    big = jnp.int32(LANES)
    neg = jnp.float32(-jnp.inf)
    is_g = jnp.logical_and(lane >= n_experts, lane < n_experts + n_groups)
    lg = jnp.where(is_g, logits, neg)
    g_max = jnp.max(lg, axis=-1, keepdims=True)
    g_lane = jnp.min(jnp.where(lg == g_max, lane, big), axis=-1, keepdims=True)
    g_top = 1.0 / jnp.sum(jnp.where(is_g, jnp.exp(lg - g_max), 0.0), axis=-1, keepdims=True)
    g_idx = g_lane - n_experts
    in_group = jnp.logical_and(lane >= g_idx * per_group, lane < (g_idx + 1) * per_group)
    le = jnp.where(in_group, logits, neg)
    m1 = jnp.max(le, axis=-1, keepdims=True)
    i1 = jnp.min(jnp.where(le == m1, lane, big), axis=-1, keepdims=True)
    le2 = jnp.where(lane == i1, neg, le)
    m2 = jnp.max(le2, axis=-1, keepdims=True)
    i2 = jnp.min(jnp.where(le2 == m2, lane, big), axis=-1, keepdims=True)
    r = jnp.exp(m2 - m1)
    w1 = g_top / (1.0 + r)
    w2 = g_top * r / (1.0 + r)
    return i1, i2, w1, w2


def _out_route_kernel(m_ref, w_ref, x_ref, nw_ref, wr_ref, br_ref, x2_ref, ht_ref, route_ref,
                      cnt_ref, cnt_scr, *, n_experts, n_groups):
    i = pl.program_id(0)
    tm = x_ref.shape[0]

    @pl.when(i == 0)
    def _():
        cnt_scr[...] = jnp.zeros_like(cnt_scr)

    x2 = x_ref[...] + jnp.dot(m_ref[...], w_ref[...], preferred_element_type=F32)
    x2_ref[...] = x2
    ht = _rms(x2, nw_ref[...])
    _to_row_tiles(ht_ref, 0, tm, ht)
    logits = jnp.dot(ht, wr_ref[...], precision=HIGHEST, preferred_element_type=F32) + br_ref[...]
    i1, i2, w1, w2 = _route(logits, n_experts, n_groups)

    lane = lax.broadcasted_iota(jnp.int32, (tm, LANES), 1)
    oh1 = (lane == i1).astype(F32)
    oh2 = (lane == i2).astype(F32)
    oh = oh1 + oh2
    earlier = (lax.broadcasted_iota(jnp.int32, (tm, tm), 0)
               > lax.broadcasted_iota(jnp.int32, (tm, tm), 1)).astype(BF16)
    before = jnp.dot(earlier, oh.astype(BF16), preferred_element_type=F32) + cnt_scr[0:1, :]
    r1 = jnp.sum(before * oh1, axis=-1, keepdims=True)
    r2 = jnp.sum(before * oh2, axis=-1, keepdims=True)
    cnt_scr[0:1, :] = cnt_scr[0:1, :] + jnp.sum(oh, axis=0, keepdims=True)
    cnt_ref[...] = cnt_scr[...]

    cols = [i1.astype(F32), i2.astype(F32), r1, r2, w1, w2]
    route = jnp.zeros((tm, LANES), F32)
    for k, v in enumerate(cols):
        route = jnp.where(lane == k, v, route)
    route_ref[...] = route


def _out_route(merged, w_out, x, norm_w, w_router, b_router, n_experts, n_groups, tm=512):
    seq, d = x.shape
    const = lambda i: (0, 0)
    kern = functools.partial(_out_route_kernel, n_experts=n_experts, n_groups=n_groups)
    return pl.pallas_call(
        kern,
        grid=(seq // tm,),
        in_specs=[
            pl.BlockSpec((tm, d), lambda i: (i, 0)),
            pl.BlockSpec((d, d), const),
            pl.BlockSpec((tm, d), lambda i: (i, 0)),
            pl.BlockSpec((1, d), const),
            pl.BlockSpec((d, LANES), const),
            pl.BlockSpec((1, LANES), const),
        ],
        out_specs=[
            pl.BlockSpec((tm, d), lambda i: (i, 0)),
            pl.BlockSpec((tm * ROW_TILE_ROWS, ROW_TILE_W), lambda i: (i, 0)),
            pl.BlockSpec((tm, LANES), lambda i: (i, 0)),
            pl.BlockSpec((SUBLANES, LANES), const),
        ],
        out_shape=[
            jax.ShapeDtypeStruct((seq, d), F32),
            jax.ShapeDtypeStruct((seq * ROW_TILE_ROWS, ROW_TILE_W), F32),
            jax.ShapeDtypeStruct((seq, LANES), F32),
            jax.ShapeDtypeStruct((SUBLANES, LANES), F32),
        ],
        scratch_shapes=[pltpu.VMEM((SUBLANES, LANES), F32)],
        compiler_params=_params(1),
        name="out_route",
    )(merged, w_out, x, norm_w, w_router, b_router)


GATHER_UNROLL = 8


def _start_row_gather(src_hbm, idx_ref, dst, sem, n_rows):
    def body(g, carry):
        for k in range(GATHER_UNROLL):
            r = g * GATHER_UNROLL + k
            tok = pl.multiple_of(idx_ref[0, 0, r] * ROW_TILE_ROWS, ROW_TILE_ROWS)
            dst_row = pl.multiple_of(r * ROW_TILE_ROWS, ROW_TILE_ROWS)
            pltpu.make_async_copy(src_hbm.at[pl.ds(tok, ROW_TILE_ROWS), :],
                                  dst.at[pl.ds(dst_row, ROW_TILE_ROWS), :], sem).start()
        return carry
    lax.fori_loop(0, n_rows // GATHER_UNROLL, body, 0)


def _wait_row_gather(src_hbm, dst, sem):
    pltpu.make_async_copy(src_hbm.at[pl.ds(0, dst.shape[0]), :], dst, sem).wait()


def _experts_kernel(be_ref, valid_ref, idx_ref, idx_next_ref, ht_hbm, wgu_ref, wd_ref, y_ref,
                    xbuf, xs_scr, sem, *, ff):
    b = pl.program_id(0)
    nb = pl.num_programs(0)
    bm = xs_scr.shape[0]
    slot = b % 2

    @pl.when(b == 0)
    def _():
        _start_row_gather(ht_hbm, idx_ref, xbuf.at[0], sem.at[0], bm)

    @pl.when(jnp.logical_and(b + 1 < nb, valid_ref[jnp.minimum(b + 1, nb - 1)] == 1))
    def _():
        _start_row_gather(ht_hbm, idx_next_ref, xbuf.at[1 - slot], sem.at[1 - slot], bm)

    @pl.when(valid_ref[b] == 1)
    def _():
        buf = xbuf.at[slot]
        _wait_row_gather(ht_hbm, buf, sem.at[slot])
        for s in range(ROW_TILE_ROWS):
            xs_scr[:, s * ROW_TILE_W:(s + 1) * ROW_TILE_W] = (
                _from_row_tiles(buf, 0, bm, s).astype(BF16))
        hid = jnp.dot(xs_scr[...], wgu_ref[0], preferred_element_type=F32)
        act = (jax.nn.silu(hid[:, :ff]) * hid[:, ff:]).astype(BF16)
        y = jnp.dot(act, wd_ref[0], preferred_element_type=F32)
        _to_row_tiles(y_ref, 0, bm, y)

    @pl.when(valid_ref[b] == 0)
    def _():
        y_ref[...] = jnp.zeros_like(y_ref)


def _experts(block_expert, valid, inv, ht_tiles, w_gu, w_d, bm):
    nb = block_expert.shape[0]
    n_experts, d, ff2 = w_gu.shape
    ff = ff2 // 2
    kern = functools.partial(_experts_kernel, ff=ff)
    grid_spec = pltpu.PrefetchScalarGridSpec(
        num_scalar_prefetch=2,
        grid=(nb,),
        in_specs=[
            pl.BlockSpec((1, 1, bm), lambda b, be, va: (b, 0, 0), memory_space=pltpu.SMEM),
            pl.BlockSpec((1, 1, bm), lambda b, be, va: (jnp.minimum(b + 1, nb - 1), 0, 0),
                         memory_space=pltpu.SMEM),
            pl.BlockSpec(memory_space=pl.ANY),
            pl.BlockSpec((1, d, ff2), lambda b, be, va: (be[b], 0, 0)),
            pl.BlockSpec((1, ff, d), lambda b, be, va: (be[b], 0, 0)),
        ],
        out_specs=pl.BlockSpec((bm * ROW_TILE_ROWS, ROW_TILE_W), lambda b, be, va: (b, 0)),
        scratch_shapes=[
            pltpu.VMEM((2, bm * ROW_TILE_ROWS, ROW_TILE_W), F32),
            pltpu.VMEM((bm, d), BF16),
            pltpu.SemaphoreType.DMA((2,)),
        ],
    )
    return pl.pallas_call(
        kern,
        grid_spec=grid_spec,
        out_shape=jax.ShapeDtypeStruct((nb * bm * ROW_TILE_ROWS, ROW_TILE_W), F32),
        compiler_params=_params(1),
        name="experts",
    )(block_expert, valid, inv, inv, ht_tiles, w_gu, w_d)


def _combine_kernel(idx_ref, idx_next_ref, y_hbm, x_ref, route_ref, fw_ref, o_ref, ybuf, sem,
                    *, final_norm):
    i = pl.program_id(0)
    nt = pl.num_programs(0)
    tm = x_ref.shape[0]
    slot = i % 2

    @pl.when(i == 0)
    def _():
        _start_row_gather(y_hbm, idx_ref, ybuf.at[0], sem.at[0], 2 * tm)

    @pl.when(i + 1 < nt)
    def _():
        _start_row_gather(y_hbm, idx_next_ref, ybuf.at[1 - slot], sem.at[1 - slot], 2 * tm)

    buf = ybuf.at[slot]
    _wait_row_gather(y_hbm, buf, sem.at[slot])
    w1 = route_ref[:, 4:5]
    w2 = route_ref[:, 5:6]
    ssq = jnp.zeros((tm, 1), F32)
    for s in range(ROW_TILE_ROWS):
        cols = slice(s * ROW_TILE_W, (s + 1) * ROW_TILE_W)
        v = (x_ref[:, cols] + w1 * _from_row_tiles(buf, 0, tm, s)
             + w2 * _from_row_tiles(buf, tm * ROW_TILE_ROWS, tm, s))
        o_ref[:, cols] = v
        ssq = ssq + jnp.sum(v * v, axis=-1, keepdims=True)
    if final_norm:
        scale = lax.rsqrt(ssq / o_ref.shape[1] + NORM_EPS)
        o_ref[...] = o_ref[...] * scale * fw_ref[...]


def _combine(pos, y_tiles, x2, route, final_w, final_norm, tm):
    seq, d = x2.shape
    nt = seq // tm
    kern = functools.partial(_combine_kernel, final_norm=final_norm)
    return pl.pallas_call(
        kern,
        grid=(nt,),
        in_specs=[
            pl.BlockSpec((1, 1, 2 * tm), lambda i: (i, 0, 0), memory_space=pltpu.SMEM),
            pl.BlockSpec((1, 1, 2 * tm), lambda i: (jnp.minimum(i + 1, nt - 1), 0, 0),
                         memory_space=pltpu.SMEM),
            pl.BlockSpec(memory_space=pl.ANY),
            pl.BlockSpec((tm, d), lambda i: (i, 0)),
            pl.BlockSpec((tm, LANES), lambda i: (i, 0)),
            pl.BlockSpec((1, d), lambda i: (0, 0)),
        ],
        out_specs=pl.BlockSpec((tm, d), lambda i: (i, 0)),
        out_shape=jax.ShapeDtypeStruct((seq, d), F32),
        scratch_shapes=[
            pltpu.VMEM((2, 2 * tm * ROW_TILE_ROWS, ROW_TILE_W), F32),
            pltpu.SemaphoreType.DMA((2,)),
        ],
        compiler_params=_params(1),
        name="combine",
    )(pos, pos, y_tiles, x2, route, final_w)


def _dispatch_plan(route, counts, n_experts, bm):
    seq = route.shape[0]
    nb = (2 * seq) // bm + n_experts
    e1, e2, r1, r2 = (route[:, k].astype(jnp.int32) for k in range(4))
    cnt = counts[0, :n_experts].astype(jnp.int32)
    cnt_pad = ((cnt + bm - 1) // bm) * bm
    ends = jnp.cumsum(cnt_pad)
    starts = ends - cnt_pad
    pos1 = starts[e1] + r1
    pos2 = starts[e2] + r2
    tok = jnp.arange(seq, dtype=jnp.int32)
    inv = jnp.zeros((nb * bm,), jnp.int32).at[pos1].set(tok).at[pos2].set(tok)
    blk_start = jnp.arange(nb, dtype=jnp.int32) * bm
    n_before = jnp.sum((ends[None, :] <= blk_start[:, None]).astype(jnp.int32), axis=1)
    block_expert = jnp.minimum(n_before, n_experts - 1)
    valid = (blk_start < ends[-1]).astype(jnp.int32)
    return pos1, pos2, inv.reshape(nb, 1, bm), block_expert, valid


def kernel(x, norm_mix_w, w_in, b_gate, s5_lam_re, s5_lam_im, s5_log_step, s5_b_re, s5_b_im, s5_c_re, s5_c_im, s5_d, s5_w_glu, s5_b_glu, lru_conv_w, lru_conv_b, lru_w_a, lru_b_a, lru_w_x, lru_b_x, lru_lambda, w_proj_s5, w_proj_lru, w_out, norm_ffn_w, w_router_group, b_router_group, w_router_expert, b_router_expert, w_e_gate, w_e_up, w_e_down, norm_final_w):
    depth = w_in.shape[0]
    bsz, seq, d = x.shape
    s5_w = s5_w_glu.shape[-1]
    lru_w = lru_conv_b.shape[-1]
    heads, head_dim = lru_w_a.shape[1], lru_w_a.shape[2]
    n_groups = w_router_group.shape[-1]
    n_experts = w_router_expert.shape[-1]
    o3 = s5_w + 2 * lru_w

    outs = []
    for b in range(bsz):
        xb = x[b]
        for l in range(depth):
            row = lambda v: v.astype(F32).reshape(1, -1)
            w_in_l = w_in[l].astype(BF16)
            h, u, xl, yl = _in_proj(xb, row(norm_mix_w[l]), w_in_l, s5_w, lru_w)

            t_set, e_set, m_set, a_set = _s5_matrices(
                s5_lam_re[l], s5_lam_im[l], s5_log_step[l], s5_b_re[l], s5_b_im[l],
                s5_c_re[l], s5_c_im[l])
            y_s5 = _s5(u, t_set, e_set, m_set, a_set, row(s5_d[l]))

            w_cat = jnp.concatenate([lru_w_a[l], lru_w_x[l]], axis=-1).astype(BF16)
            nls = -LRU_C * jax.nn.softplus(-lru_lambda[l].astype(F32))
            out_lru = _rglru(xl, yl, lru_conv_w[l].astype(F32), row(lru_conv_b[l]), w_cat,
                             row(lru_b_a[l]), row(lru_b_x[l]), row(nls), heads, head_dim)

            merged = _merge(h, y_s5, out_lru, s5_w_glu[l].astype(BF16), row(s5_b_glu[l]),
                            w_in_l, o3, row(b_gate[l]),
                            w_proj_s5[l].astype(BF16), w_proj_lru[l].astype(BF16))
            pad = LANES - n_experts - n_groups
            w_router = jnp.concatenate([w_router_expert[l].astype(F32), w_router_group[l].astype(F32),
                                        jnp.zeros((d, pad), F32)], axis=1)
            b_router = jnp.concatenate([b_router_expert[l].astype(F32), b_router_group[l].astype(F32),
                                        jnp.zeros((pad,), F32)]).reshape(1, LANES)
            x2, ht_tiles, route, counts = _out_route(
                merged, w_out[l].astype(BF16), xb, row(norm_ffn_w[l]), w_router, b_router,
                n_experts, n_groups)

            pos1, pos2, inv, block_expert, valid = _dispatch_plan(route, counts, n_experts,
                                                                  MOE_BLOCK)
            w_gu = jnp.concatenate([w_e_gate[l], w_e_up[l]], axis=-1).astype(BF16)
            y_tiles = _experts(block_expert, valid, inv, ht_tiles, w_gu, w_e_down[l].astype(BF16),
                               MOE_BLOCK)
            tc = COMBINE_TILE
            pos = jnp.concatenate([pos1.reshape(seq // tc, 1, tc), pos2.reshape(seq // tc, 1, tc)],
                                  axis=2)
            xb = _combine(pos, y_tiles, x2, route, row(norm_final_w),
                          final_norm=(l == depth - 1), tm=tc)
        outs.append(xb)
    return jnp.stack(outs)
```

```python
import functools

import jax
import jax.numpy as jnp
from jax import lax
from jax.experimental import pallas as pl
from jax.experimental.pallas import tpu as pltpu

F32 = jnp.float32
BF16 = jnp.bfloat16
HIGHEST = lax.Precision.HIGHEST

NORM_EPS = 1e-6
LRU_C = 8.0
S5_GROUP = 16
CONV_WIDTH = 4
SUBLANES = 8
LANES = 128
S5_CHUNK = SUBLANES
S5_SET = LANES // S5_GROUP
VMEM_LIMIT = 56 * 1024 * 1024
ROUTE_TILE = 512
MOE_BLOCK = 256
COMBINE_TILE = 256


def _params(n_axes, vmem=VMEM_LIMIT):
    return pltpu.CompilerParams(dimension_semantics=("arbitrary",) * n_axes,
                                vmem_limit_bytes=vmem)


def _rms(x, w):
    ms = jnp.mean(x * x, axis=-1, keepdims=True)
    return x * lax.rsqrt(ms + NORM_EPS) * w


def _in_proj_kernel(x_ref, nw_ref, w_ref, h_ref, u_ref, xl_ref, yl_ref, *, n_u, n_x):
    j = pl.program_id(1)

    @pl.when(j == 0)
    def _():
        h_ref[...] = _rms(x_ref[...], nw_ref[...]).astype(BF16)

    p = jnp.dot(h_ref[...], w_ref[...], preferred_element_type=F32)

    @pl.when(j < n_u)
    def _():
        u_ref[...] = p

    @pl.when(jnp.logical_and(j >= n_u, j < n_u + n_x))
    def _():
        xl_ref[...] = p

    @pl.when(j >= n_u + n_x)
    def _():
        yl_ref[...] = p


def _in_proj(x, norm_w, w, s5_w, lru_w, tm=1024, tn=512):
    seq, d = x.shape
    n_u, n_x = s5_w // tn, lru_w // tn
    kern = functools.partial(_in_proj_kernel, n_u=n_u, n_x=n_x)
    return pl.pallas_call(
        kern,
        grid=(seq // tm, n_u + 2 * n_x),
        in_specs=[
            pl.BlockSpec((tm, d), lambda i, j: (i, 0)),
            pl.BlockSpec((1, d), lambda i, j: (0, 0)),
            pl.BlockSpec((d, tn), lambda i, j: (0, j)),
        ],
        out_specs=[
            pl.BlockSpec((tm, d), lambda i, j: (i, 0)),
            pl.BlockSpec((tm, tn), lambda i, j: (i, jnp.minimum(j, n_u - 1))),
            pl.BlockSpec((tm, tn), lambda i, j: (i, jnp.clip(j - n_u, 0, n_x - 1))),
            pl.BlockSpec((tm, tn), lambda i, j: (i, jnp.clip(j - n_u - n_x, 0, n_x - 1))),
        ],
        out_shape=[
            jax.ShapeDtypeStruct((seq, d), BF16),
            jax.ShapeDtypeStruct((seq, s5_w), F32),
            jax.ShapeDtypeStruct((seq, lru_w), F32),
            jax.ShapeDtypeStruct((seq, lru_w), F32),
        ],
        compiler_params=_params(2),
        name="in_proj",
    )(x, norm_w, w)


def _s5_kernel(u_ref, t_ref, e_ref, m_ref, a_ref, d_ref, y_ref, x_scr, e_scr, s_scr, c_scr):
    tt = pl.program_id(1)
    n_chunks = x_scr.shape[0]
    half = e_scr.shape[1] // 2
    n_col = half // LANES

    @pl.when(tt == 0)
    def _():
        c_scr[...] = jnp.zeros_like(c_scr)

    for j in range(S5_CHUNK):
        x_scr[:, j * LANES:(j + 1) * LANES] = (
            u_ref[pl.ds(j, n_chunks, stride=S5_CHUNK), :].astype(BF16))

    e_scr[...] = jnp.dot(x_scr[...], e_ref[0], preferred_element_type=F32)

    row = lax.broadcasted_iota(jnp.int32, (SUBLANES, LANES), 0)

    def body(t, carry):
        r0 = pl.multiple_of(t * SUBLANES, SUBLANES)
        new = []
        for q in range(n_col):
            c_re, c_im = carry[2 * q], carry[2 * q + 1]
            re_cols = slice(q * LANES, (q + 1) * LANES)
            im_cols = slice(half + q * LANES, half + (q + 1) * LANES)
            x_re = e_scr[pl.ds(r0, SUBLANES), re_cols]
            x_im = e_scr[pl.ds(r0, SUBLANES), im_cols]
            for s in (1, 2, 4):
                p_re = a_ref[0, s - 1:s, re_cols]
                p_im = a_ref[0, SUBLANES + s - 1:SUBLANES + s, re_cols]
                keep = row >= s
                s_re = jnp.where(keep, pltpu.roll(x_re, s, 0), 0.0)
                s_im = jnp.where(keep, pltpu.roll(x_im, s, 0), 0.0)
                x_re, x_im = (x_re + p_re * s_re - p_im * s_im,
                              x_im + p_re * s_im + p_im * s_re)
            t_re = a_ref[0, 0:SUBLANES, re_cols]
            t_im = a_ref[0, SUBLANES:2 * SUBLANES, re_cols]
            x_re, x_im = (x_re + t_re * c_re - t_im * c_im,
                          x_im + t_re * c_im + t_im * c_re)
            s_scr[pl.ds(r0, SUBLANES), re_cols] = jnp.where(row >= 1, pltpu.roll(x_re, 1, 0), c_re)
            s_scr[pl.ds(r0, SUBLANES), im_cols] = jnp.where(row >= 1, pltpu.roll(x_im, 1, 0), c_im)
            new += [x_re[SUBLANES - 1:SUBLANES, :], x_im[SUBLANES - 1:SUBLANES, :]]
        return tuple(new)

    init = []
    for q in range(n_col):
        init += [c_scr[0:1, q * LANES:(q + 1) * LANES],
                 c_scr[0:1, half + q * LANES:half + (q + 1) * LANES]]
    last = lax.fori_loop(0, n_chunks // SUBLANES, body, tuple(init))
    for q in range(n_col):
        c_scr[0:1, q * LANES:(q + 1) * LANES] = last[2 * q]
        c_scr[0:1, half + q * LANES:half + (q + 1) * LANES] = last[2 * q + 1]

    y = jnp.dot(x_scr[...], t_ref[0], preferred_element_type=F32)
    y = y + jnp.dot(s_scr[...].astype(BF16), m_ref[0], preferred_element_type=F32)
    for j in range(S5_CHUNK):
        rows = pl.ds(j, n_chunks, stride=S5_CHUNK)
        yj = y[:, j * LANES:(j + 1) * LANES] + d_ref[...] * u_ref[rows, :]
        y_ref[rows, :] = jax.nn.gelu(yj)


def _s5(u, t_set, e_set, m_set, a_set, d_skip, tile=4096):
    seq, width = u.shape
    n_sets = width // LANES
    n_chunks = tile // S5_CHUNK
    kdim = S5_CHUNK * LANES
    sdim = e_set.shape[2]
    return pl.pallas_call(
        _s5_kernel,
        grid=(n_sets, seq // tile),
        in_specs=[
            pl.BlockSpec((tile, LANES), lambda s, t: (t, s)),
            pl.BlockSpec((1, kdim, kdim), lambda s, t: (s, 0, 0)),
            pl.BlockSpec((1, kdim, sdim), lambda s, t: (s, 0, 0)),
            pl.BlockSpec((1, sdim, kdim), lambda s, t: (s, 0, 0)),
            pl.BlockSpec((1, 2 * SUBLANES, sdim // 2), lambda s, t: (s, 0, 0)),
            pl.BlockSpec((1, LANES), lambda s, t: (0, s)),
        ],
        out_specs=pl.BlockSpec((tile, LANES), lambda s, t: (t, s)),
        out_shape=jax.ShapeDtypeStruct((seq, width), F32),
        scratch_shapes=[
            pltpu.VMEM((n_chunks, kdim), BF16),
            pltpu.VMEM((n_chunks, sdim), F32),
            pltpu.VMEM((n_chunks, sdim), F32),
            pltpu.VMEM((SUBLANES, sdim), F32),
        ],
        compiler_params=_params(2),
        name="s5",
    )(u, t_set, e_set, m_set, a_set, d_skip)


def _cmul(a, b):
    return a[0] * b[0] - a[1] * b[1], a[0] * b[1] + a[1] * b[0]


def _s5_matrices(lam_re, lam_im, log_step, b_re, b_im, c_re, c_im):
    g, p = lam_re.shape
    gs, tc, ns = S5_GROUP, S5_CHUNK, S5_SET
    n_sets = g // ns
    lam = (lam_re.astype(F32), lam_im.astype(F32))
    step = jnp.exp(log_step.astype(F32))[:, None]
    mag = jnp.exp(lam[0] * step)
    lam_bar = (mag * jnp.cos(lam[1] * step), mag * jnp.sin(lam[1] * step))
    den = lam[0] * lam[0] + lam[1] * lam[1]
    coef = _cmul((lam_bar[0] - 1.0, lam_bar[1]), (lam[0] / den, -lam[1] / den))
    b_c = (b_re.astype(F32), b_im.astype(F32))
    b_bar = _cmul((coef[0][..., None], coef[1][..., None]), b_c)
    c_c = (c_re.astype(F32), c_im.astype(F32))

    pows = [(jnp.ones_like(lam_bar[0]), jnp.zeros_like(lam_bar[0]))]
    for _ in range(tc):
        pows.append(_cmul(pows[-1], lam_bar))
    pw = (jnp.stack([q[0] for q in pows]), jnp.stack([q[1] for q in pows]))

    w = _cmul((pw[0][:tc, :, :, None], pw[1][:tc, :, :, None]), (b_bar[0][None], b_bar[1][None]))
    kmat = (jnp.einsum('gop,tgpc->tgoc', c_c[0], w[0], precision=HIGHEST)
            - jnp.einsum('gop,tgpc->tgoc', c_c[1], w[1], precision=HIGHEST))
    eye = jnp.eye(ns, dtype=F32)
    kmat = kmat.reshape(tc, n_sets, ns, gs, gs)
    t_blocks = []
    for j in range(tc):
        kj = jnp.concatenate([jnp.zeros_like(kmat[:j]), kmat[:tc - j]], axis=0)
        t_blocks.append(jnp.einsum('tsgoc,gh->sgctho', kj, eye))
    t_set = jnp.stack(t_blocks, axis=1).reshape(n_sets, tc * ns * gs, tc * ns * gs)

    def e_half(x):
        x = x[::-1].reshape(tc, n_sets, ns, p, gs)
        return jnp.einsum('jsgpc,gh->sjgchp', x, eye).reshape(n_sets, tc * ns * gs, ns * p)
    e_set = jnp.concatenate([e_half(w[0]), e_half(w[1])], axis=2)

    m = _cmul((c_c[0].transpose(0, 2, 1)[None], c_c[1].transpose(0, 2, 1)[None]),
              (pw[0][1:, :, :, None], pw[1][1:, :, :, None]))

    def m_half(x):
        x = x.reshape(tc, n_sets, ns, p, gs)
        return jnp.einsum('tsgpo,gh->sgptho', x, eye).reshape(n_sets, ns * p, tc * ns * gs)
    m_set = jnp.concatenate([m_half(m[0]), m_half(-m[1])], axis=1)

    a_c = (pw[0][tc], pw[1][tc])
    a_pows = [a_c]
    for _ in range(SUBLANES - 1):
        a_pows.append(_cmul(a_pows[-1], a_c))
    a_set = jnp.concatenate([jnp.stack([q[0] for q in a_pows]), jnp.stack([q[1] for q in a_pows])])
    a_set = a_set.reshape(2 * SUBLANES, n_sets, ns * p).transpose(1, 0, 2)
    return t_set.astype(BF16), e_set.astype(BF16), m_set.astype(BF16), a_set


def _rglru_kernel(xl_ref, yl_ref, cw_ref, cb_ref, w_ref, ba_ref, bx_ref, nls_ref, o_ref,
                  xpad, a_scr, b_scr, h_scr, *, heads, head_dim):
    i = pl.program_id(0)
    tm, width = xl_ref.shape
    hist = SUBLANES

    @pl.when(i == 0)
    def _():
        xpad[0:hist, :] = jnp.zeros((hist, width), F32)
        h_scr[...] = jnp.zeros_like(h_scr)

    @pl.when(i > 0)
    def _():
        xpad[0:hist, :] = xpad[tm:tm + hist, :]

    xpad[hist:hist + tm, :] = xl_ref[...]

    first = jnp.logical_and(lax.broadcasted_iota(jnp.int32, (tm, head_dim), 0) == 0, i == 0)
    for hd in range(heads):
        cs = slice(hd * head_dim, (hd + 1) * head_dim)
        xc = cb_ref[:, cs]
        for k in range(CONV_WIDTH):
            off = hist - (CONV_WIDTH - 1) + k
            xc = xc + xpad[off:off + tm, cs] * cw_ref[k:k + 1, cs]
        g = jnp.dot(xc.astype(BF16), w_ref[hd], preferred_element_type=F32)
        r = jax.nn.sigmoid(g[:, :head_dim] + ba_ref[:, cs])
        gi = jax.nn.sigmoid(g[:, head_dim:] + bx_ref[:, cs])
        log_a = nls_ref[:, cs] * r
        a = jnp.exp(log_a)
        mult = jnp.sqrt(-jnp.tanh(log_a) * (a * a + 1.0))
        mult = jnp.where(first, 1.0, mult)
        a_scr[:, cs] = a
        b_scr[:, cs] = xc * gi * mult

    row = lax.broadcasted_iota(jnp.int32, (SUBLANES, width), 0)

    def body(t, h_prev):
        r0 = pl.multiple_of(t * SUBLANES, SUBLANES)
        a = a_scr[pl.ds(r0, SUBLANES), :]
        b = b_scr[pl.ds(r0, SUBLANES), :]
        for s in (1, 2, 4):
            keep = row >= s
            b = b + a * jnp.where(keep, pltpu.roll(b, s, 0), 0.0)
            a = a * jnp.where(keep, pltpu.roll(a, s, 0), 1.0)
        h = b + a * h_prev
        y = yl_ref[pl.ds(r0, SUBLANES), :]
        o_ref[pl.ds(r0, SUBLANES), :] = (h * jax.nn.gelu(y)).astype(o_ref.dtype)
        return h[SUBLANES - 1:SUBLANES, :]

    h_scr[...] = lax.fori_loop(0, tm // SUBLANES, body, h_scr[...])


def _rglru(xl, yl, conv_w, conv_b, w_cat, b_a, b_x, nls, heads, head_dim, tm=512):
    seq, width = xl.shape
    kern = functools.partial(_rglru_kernel, heads=heads, head_dim=head_dim)
    row = lambda i: (0, 0)
    return pl.pallas_call(
        kern,
        grid=(seq // tm,),
        in_specs=[
            pl.BlockSpec((tm, width), lambda i: (i, 0)),
            pl.BlockSpec((tm, width), lambda i: (i, 0)),
            pl.BlockSpec((CONV_WIDTH, width), row),
            pl.BlockSpec((1, width), row),
            pl.BlockSpec((heads, head_dim, 2 * head_dim), lambda i: (0, 0, 0)),
            pl.BlockSpec((1, width), row),
            pl.BlockSpec((1, width), row),
            pl.BlockSpec((1, width), row),
        ],
        out_specs=pl.BlockSpec((tm, width), lambda i: (i, 0)),
        out_shape=jax.ShapeDtypeStruct((seq, width), BF16),
        scratch_shapes=[
            pltpu.VMEM((tm + 2 * SUBLANES, width), F32),
            pltpu.VMEM((tm, width), F32),
            pltpu.VMEM((tm, width), F32),
            pltpu.VMEM((1, width), F32),
        ],
        compiler_params=_params(1),
        name="rglru",
    )(xl, yl, conv_w, conv_b, w_cat, b_a, b_x, nls)


def _merge_kernel(h_ref, y_ref, ol_ref, wglu_ref, bglu_ref, wgs_ref, wgl_ref, bgs_ref, bgl_ref,
                  wps_ref, wpl_ref, o_ref, os_scr):
    j = pl.program_id(1)

    @pl.when(j == 0)
    def _():
        y = y_ref[...]
        glu = jax.nn.sigmoid(
            jnp.dot(y.astype(BF16), wglu_ref[...], preferred_element_type=F32) + bglu_ref[...])
        os_scr[...] = (y * glu).astype(BF16)

    h = h_ref[...]
    g_s5 = jax.nn.sigmoid(jnp.dot(h, wgs_ref[...], preferred_element_type=F32) + bgs_ref[...])
    g_lru = jax.nn.sigmoid(jnp.dot(h, wgl_ref[...], preferred_element_type=F32) + bgl_ref[...])
    br_s5 = jnp.dot(os_scr[...], wps_ref[...], preferred_element_type=F32)
    br_lru = jnp.dot(ol_ref[...], wpl_ref[...], preferred_element_type=F32)
    o_ref[...] = (g_s5 * br_s5 + g_lru * br_lru).astype(o_ref.dtype)


def _merge(h, y_s5, out_lru, w_glu, b_glu, w_in, gate_col, b_gate, wp_s5, wp_lru,
           tm=1024, tn=512):
    seq, d = h.shape
    s5_w, lru_w = y_s5.shape[1], out_lru.shape[1]
    const = lambda i, j: (0, 0)
    c0, nd = gate_col // tn, d // tn
    return pl.pallas_call(
        _merge_kernel,
        grid=(seq // tm, nd),
        in_specs=[
            pl.BlockSpec((tm, d), lambda i, j: (i, 0)),
            pl.BlockSpec((tm, s5_w), lambda i, j: (i, 0)),
            pl.BlockSpec((tm, lru_w), lambda i, j: (i, 0)),
            pl.BlockSpec((s5_w, s5_w), const),
            pl.BlockSpec((1, s5_w), const),
            pl.BlockSpec((d, tn), lambda i, j: (0, c0 + j)),
            pl.BlockSpec((d, tn), lambda i, j: (0, c0 + nd + j)),
            pl.BlockSpec((1, tn), lambda i, j: (0, j)),
            pl.BlockSpec((1, tn), lambda i, j: (0, nd + j)),
            pl.BlockSpec((s5_w, tn), lambda i, j: (0, j)),
            pl.BlockSpec((lru_w, tn), lambda i, j: (0, j)),
        ],
        out_specs=pl.BlockSpec((tm, tn), lambda i, j: (i, j)),
        out_shape=jax.ShapeDtypeStruct((seq, d), BF16),
        scratch_shapes=[pltpu.VMEM((tm, s5_w), BF16)],
        compiler_params=_params(2),
        name="merge",
    )(h, y_s5, out_lru, w_glu, b_glu, w_in, w_in, b_gate, b_gate, wp_s5, wp_lru)


def _n_slabs(d):
    assert d % LANES == 0
    return d // LANES


def _store_slabs(ref, x):
    for s in range(ref.shape[0]):
        ref[s] = x[:, s * LANES:(s + 1) * LANES]


def _route(logits, n_experts, n_groups):
    per_group = n_experts // n_groups
    lane = lax.broadcasted_iota(jnp.int32, logits.shape, 1)
    big = jnp.int32(LANES)
    neg = jnp.float32(-jnp.inf)
    is_g = jnp.logical_and(lane >= n_experts, lane < n_experts + n_groups)
    lg = jnp.where(is_g, logits, neg)
    g_max = jnp.max(lg, axis=-1, keepdims=True)
    g_lane = jnp.min(jnp.where(lg == g_max, lane, big), axis=-1, keepdims=True)
    g_top = 1.0 / jnp.sum(jnp.where(is_g, jnp.exp(lg - g_max), 0.0), axis=-1, keepdims=True)
    g_idx = g_lane - n_experts
    in_group = jnp.logical_and(lane >= g_idx * per_group, lane < (g_idx + 1) * per_group)
    le = jnp.where(in_group, logits, neg)
    m1 = jnp.max(le, axis=-1, keepdims=True)
    i1 = jnp.min(jnp.where(le == m1, lane, big), axis=-1, keepdims=True)
    le2 = jnp.where(lane == i1, neg, le)
    m2 = jnp.max(le2, axis=-1, keepdims=True)
    i2 = jnp.min(jnp.where(le2 == m2, lane, big), axis=-1, keepdims=True)
    r = jnp.exp(m2 - m1)
    w1 = g_top / (1.0 + r)
    w2 = g_top * r / (1.0 + r)
    return i1, i2, w1, w2


def _out_route_kernel(m_ref, w_ref, x_ref, nw_ref, wr_ref, br_ref, x2_ref, ht_ref, route_ref,
                      route_t_ref, cnt_ref, cnt_scr, *, n_experts, n_groups):
    i = pl.program_id(0)
    tm = x_ref.shape[0]

    @pl.when(i == 0)
    def _():
        cnt_scr[...] = jnp.zeros_like(cnt_scr)

    x2 = x_ref[...] + jnp.dot(m_ref[...], w_ref[...], preferred_element_type=F32)
    x2_ref[...] = x2
    ht = _rms(x2, nw_ref[...])
    _store_slabs(ht_ref.at[0], ht)
    ht_hi = ht.astype(BF16)
    ht_lo = (ht - ht_hi.astype(F32)).astype(BF16)
    both = jnp.dot(ht_hi, wr_ref[...], preferred_element_type=F32)
    logits = (both[:, :LANES] + both[:, LANES:]
              + jnp.dot(ht_lo, wr_ref[:, :LANES], preferred_element_type=F32) + br_ref[...])
    i1, i2, w1, w2 = _route(logits, n_experts, n_groups)

    lane = lax.broadcasted_iota(jnp.int32, (tm, LANES), 1)
    oh1 = (lane == i1).astype(F32)
    oh2 = (lane == i2).astype(F32)
    oh = oh1 + oh2
    earlier = (lax.broadcasted_iota(jnp.int32, (tm, tm), 0)
               > lax.broadcasted_iota(jnp.int32, (tm, tm), 1)).astype(BF16)
    before = jnp.dot(earlier, oh.astype(BF16), preferred_element_type=F32) + cnt_scr[0:1, :]
    r1 = jnp.sum(before * oh1, axis=-1, keepdims=True)
    r2 = jnp.sum(before * oh2, axis=-1, keepdims=True)
    cnt_scr[0:1, :] = cnt_scr[0:1, :] + jnp.sum(oh, axis=0, keepdims=True)
    cnt_ref[...] = cnt_scr[...]

    cols = [i1.astype(F32), i2.astype(F32), r1, r2, w1, w2]
    route = jnp.zeros((tm, LANES), F32)
    for k, v in enumerate(cols):
        route = jnp.where(lane == k, v, route)
    route_ref[...] = route
    route_t_ref[...] = route.T[0:SUBLANES, :]


def _out_route(merged, w_out, x, norm_w, w_router, b_router, n_experts, n_groups, tm):
    seq, d = x.shape
    ns = _n_slabs(d)
    const = lambda i: (0, 0)
    kern = functools.partial(_out_route_kernel, n_experts=n_experts, n_groups=n_groups)
    return pl.pallas_call(
        kern,
        grid=(seq // tm,),
        in_specs=[
            pl.BlockSpec((tm, d), lambda i: (i, 0)),
            pl.BlockSpec((d, d), const),
            pl.BlockSpec((tm, d), lambda i: (i, 0)),
            pl.BlockSpec((1, d), const),
            pl.BlockSpec((d, 2 * LANES), const),
            pl.BlockSpec((1, LANES), const),
        ],
        out_specs=[
            pl.BlockSpec((tm, d), lambda i: (i, 0)),
            pl.BlockSpec((1, ns, tm, LANES), lambda i: (i, 0, 0, 0)),
            pl.BlockSpec((tm, LANES), lambda i: (i, 0)),
            pl.BlockSpec((SUBLANES, tm), lambda i: (0, i)),
            pl.BlockSpec((SUBLANES, LANES), const),
        ],
        out_shape=[
            jax.ShapeDtypeStruct((seq, d), F32),
            jax.ShapeDtypeStruct((seq // tm, ns, tm, LANES), F32),
            jax.ShapeDtypeStruct((seq, LANES), F32),
            jax.ShapeDtypeStruct((SUBLANES, seq), F32),
            jax.ShapeDtypeStruct((SUBLANES, LANES), F32),
        ],
        scratch_shapes=[pltpu.VMEM((SUBLANES, LANES), F32)],
        compiler_params=_params(1),
        name="out_route",
    )(merged, w_out, x, norm_w, w_router, b_router)


GATHER_UNROLL = 8


def _start_row_gather(src_hbm, idx_ref, idx_base, dst, base, sem, n_rows):
    tile_rows = src_hbm.shape[2]
    shift = tile_rows.bit_length() - 1
    assert tile_rows == 1 << shift

    def body(g, carry):
        for k in range(GATHER_UNROLL):
            r = g * GATHER_UNROLL + k
            idx = idx_ref[0, 0, idx_base + r]
            tile = lax.shift_right_logical(idx, shift)
            row = jnp.bitwise_and(idx, tile_rows - 1)
            pltpu.make_async_copy(src_hbm.at[tile, :, pl.ds(row, 1), :],
                                  dst.at[:, pl.ds(base + r, 1), :], sem).start(priority=k % 2)
        return carry
    lax.fori_loop(0, n_rows // GATHER_UNROLL, body, 0)


def _wait_row_gather(src_hbm, dst, base, sem, n_rows):
    pltpu.make_async_copy(src_hbm.at[0, :, pl.ds(0, n_rows), :],
                          dst.at[:, pl.ds(base, n_rows), :], sem).wait()


def _experts_kernel(be_ref, valid_ref, idx_ref, idx_next_ref, ht_hbm, wg_ref, wu_ref, wd_ref,
                    y_ref, xbuf, xs_scr, wgu_scr, wd_scr, sem):
    b = pl.program_id(0)
    nb = pl.num_programs(0)
    bm = xs_scr.shape[0]
    ff = wd_scr.shape[0]
    slot = b % 2

    @pl.when(b == 0)
    def _():
        _start_row_gather(ht_hbm, idx_ref, 0, xbuf.at[0], 0, sem.at[0], bm)

    @pl.when(jnp.logical_and(b + 1 < nb, valid_ref[jnp.minimum(b + 1, nb - 1)] == 1))
    def _():
        _start_row_gather(ht_hbm, idx_next_ref, 0, xbuf.at[1 - slot], 0, sem.at[1 - slot], bm)

    @pl.when(jnp.logical_or(b == 0, be_ref[b] != be_ref[jnp.maximum(b - 1, 0)]))
    def _():
        wgu_scr[:, :ff] = wg_ref[0].astype(BF16)
        wgu_scr[:, ff:] = wu_ref[0].astype(BF16)
        wd_scr[...] = wd_ref[0].astype(BF16)

    @pl.when(valid_ref[b] == 1)
    def _():
        buf = xbuf.at[slot]
        _wait_row_gather(ht_hbm, buf, 0, sem.at[slot], bm)
        for s in range(buf.shape[0]):
            xs_scr[:, s * LANES:(s + 1) * LANES] = buf[s].astype(BF16)
        hid = jnp.dot(xs_scr[...], wgu_scr[...], preferred_element_type=F32)
        act = (jax.nn.silu(hid[:, :ff]) * hid[:, ff:]).astype(BF16)
        y = jnp.dot(act, wd_scr[...], preferred_element_type=F32)
        _store_slabs(y_ref.at[0], y)

    @pl.when(valid_ref[b] == 0)
    def _():
        y_ref[...] = jnp.zeros_like(y_ref)


def _experts(block_expert, valid, inv, ht_slabs, w_gate, w_up, w_down, bm):
    nb = block_expert.shape[0]
    n_experts, d, ff = w_gate.shape
    ns = _n_slabs(d)
    grid_spec = pltpu.PrefetchScalarGridSpec(
        num_scalar_prefetch=2,
        grid=(nb,),
        in_specs=[
            pl.BlockSpec((1, 1, bm), lambda b, be, va: (b, 0, 0), memory_space=pltpu.SMEM),
            pl.BlockSpec((1, 1, bm), lambda b, be, va: (jnp.minimum(b + 1, nb - 1), 0, 0),
                         memory_space=pltpu.SMEM),
            pl.BlockSpec(memory_space=pl.ANY),
            pl.BlockSpec((1, d, ff), lambda b, be, va: (be[b], 0, 0)),
            pl.BlockSpec((1, d, ff), lambda b, be, va: (be[b], 0, 0)),
            pl.BlockSpec((1, ff, d), lambda b, be, va: (be[b], 0, 0)),
        ],
        out_specs=pl.BlockSpec((1, ns, bm, LANES), lambda b, be, va: (b, 0, 0, 0)),
        scratch_shapes=[
            pltpu.VMEM((2, ns, bm, LANES), F32),
            pltpu.VMEM((bm, d), BF16),
            pltpu.VMEM((d, 2 * ff), BF16),
            pltpu.VMEM((ff, d), BF16),
            pltpu.SemaphoreType.DMA((2,)),
        ],
    )
    return pl.pallas_call(
        _experts_kernel,
        grid_spec=grid_spec,
        out_shape=jax.ShapeDtypeStruct((nb, ns, bm, LANES), F32),
        compiler_params=_params(1),
        name="experts",
    )(block_expert, valid, inv, inv, ht_slabs, w_gate, w_up, w_down)


def _combine_kernel(idx_ref, idx_next_ref, y_hbm, x_ref, route_ref, fw_ref, o_ref, ybuf, sem,
                    *, final_norm):
    i = pl.program_id(0)
    nt = pl.num_programs(0)
    tm = x_ref.shape[0]
    slot = i % 2

    @pl.when(i == 0)
    def _():
        _start_row_gather(y_hbm, idx_ref, 0, ybuf.at[0], 0, sem.at[0], 2 * tm)

    @pl.when(i + 1 < nt)
    def _():
        _start_row_gather(y_hbm, idx_next_ref, 0, ybuf.at[1 - slot], 0, sem.at[1 - slot], 2 * tm)

    buf = ybuf.at[slot]
    _wait_row_gather(y_hbm, buf, 0, sem.at[slot], tm)
    _wait_row_gather(y_hbm, buf, tm, sem.at[slot], tm)
    w1 = route_ref[:, 4:5]
    w2 = route_ref[:, 5:6]
    ssq = jnp.zeros((tm, 1), F32)
    for s in range(buf.shape[0]):
        cols = slice(s * LANES, (s + 1) * LANES)
        v = x_ref[:, cols] + w1 * buf[s, 0:tm, :] + w2 * buf[s, tm:2 * tm, :]
        o_ref[:, cols] = v
        ssq = ssq + jnp.sum(v * v, axis=-1, keepdims=True)
    if final_norm:
        scale = lax.rsqrt(ssq / o_ref.shape[1] + NORM_EPS)
        o_ref[...] = o_ref[...] * scale * fw_ref[...]


def _combine(pos, y_slabs, x2, route, final_w, final_norm, tm):
    seq, d = x2.shape
    nt = seq // tm
    assert y_slabs.shape[2] >= tm
    kern = functools.partial(_combine_kernel, final_norm=final_norm)
    return pl.pallas_call(
        kern,
        grid=(nt,),
        in_specs=[
            pl.BlockSpec((1, 1, 2 * tm), lambda i: (i, 0, 0), memory_space=pltpu.SMEM),
            pl.BlockSpec((1, 1, 2 * tm), lambda i: (jnp.minimum(i + 1, nt - 1), 0, 0),
                         memory_space=pltpu.SMEM),
            pl.BlockSpec(memory_space=pl.ANY),
            pl.BlockSpec((tm, d), lambda i: (i, 0)),
            pl.BlockSpec((tm, LANES), lambda i: (i, 0)),
            pl.BlockSpec((1, d), lambda i: (0, 0)),
        ],
        out_specs=pl.BlockSpec((tm, d), lambda i: (i, 0)),
        out_shape=jax.ShapeDtypeStruct((seq, d), F32),
        scratch_shapes=[
            pltpu.VMEM((2, _n_slabs(d), 2 * tm, LANES), F32),
            pltpu.SemaphoreType.DMA((2,)),
        ],
        compiler_params=_params(1),
        name="combine",
    )(pos, pos, y_slabs, x2, route, final_w)


def _dispatch_plan(route_t, counts, n_experts, bm):
    seq = route_t.shape[1]
    nb = (2 * seq) // bm + n_experts
    e1, e2, r1, r2 = (route_t[k].astype(jnp.int32) for k in range(4))
    cnt = counts[0, :n_experts].astype(jnp.int32)
    cnt_pad = ((cnt + bm - 1) // bm) * bm
    ends = jnp.cumsum(cnt_pad)
    starts = ends - cnt_pad
    experts = jnp.arange(n_experts, dtype=jnp.int32)

    def start_of(e):
        return jnp.sum(jnp.where(e[:, None] == experts[None, :], starts[None, :], 0), axis=1)
    pos1 = start_of(e1) + r1
    pos2 = start_of(e2) + r2
    tok = jnp.arange(seq, dtype=jnp.int32)
    inv = jnp.zeros((nb * bm,), jnp.int32).at[jnp.concatenate([pos1, pos2])].set(
        jnp.concatenate([tok, tok]))
    blk_start = jnp.arange(nb, dtype=jnp.int32) * bm
    n_before = jnp.sum((ends[None, :] <= blk_start[:, None]).astype(jnp.int32), axis=1)
    block_expert = jnp.minimum(n_before, n_experts - 1)
    valid = (blk_start < ends[-1]).astype(jnp.int32)
    return pos1, pos2, inv.reshape(nb, 1, bm), block_expert, valid


def kernel(x, norm_mix_w, w_in, b_gate, s5_lam_re, s5_lam_im, s5_log_step, s5_b_re, s5_b_im, s5_c_re, s5_c_im, s5_d, s5_w_glu, s5_b_glu, lru_conv_w, lru_conv_b, lru_w_a, lru_b_a, lru_w_x, lru_b_x, lru_lambda, w_proj_s5, w_proj_lru, w_out, norm_ffn_w, w_router_group, b_router_group, w_router_expert, b_router_expert, w_e_gate, w_e_up, w_e_down, norm_final_w):
    depth = w_in.shape[0]
    bsz, seq, d = x.shape
    s5_w = s5_w_glu.shape[-1]
    lru_w = lru_conv_b.shape[-1]
    heads, head_dim = lru_w_a.shape[1], lru_w_a.shape[2]
    n_groups = w_router_group.shape[-1]
    n_experts = w_router_expert.shape[-1]
    o3 = s5_w + 2 * lru_w

    outs = []
    for b in range(bsz):
        xb = x[b]
        for l in range(depth):
            row = lambda v: v.astype(F32).reshape(1, -1)
            w_in_l = w_in[l].astype(BF16)
            h, u, xl, yl = _in_proj(xb, row(norm_mix_w[l]), w_in_l, s5_w, lru_w)

            t_set, e_set, m_set, a_set = _s5_matrices(
                s5_lam_re[l], s5_lam_im[l], s5_log_step[l], s5_b_re[l], s5_b_im[l],
                s5_c_re[l], s5_c_im[l])
            y_s5 = _s5(u, t_set, e_set, m_set, a_set, row(s5_d[l]))

            w_cat = jnp.concatenate([lru_w_a[l], lru_w_x[l]], axis=-1).astype(BF16)
            nls = -LRU_C * jax.nn.softplus(-lru_lambda[l].astype(F32))
            out_lru = _rglru(xl, yl, lru_conv_w[l].astype(F32), row(lru_conv_b[l]), w_cat,
                             row(lru_b_a[l]), row(lru_b_x[l]), row(nls), heads, head_dim)

            merged = _merge(h, y_s5, out_lru, s5_w_glu[l].astype(BF16), row(s5_b_glu[l]),
                            w_in_l, o3, row(b_gate[l]),
                            w_proj_s5[l].astype(BF16), w_proj_lru[l].astype(BF16))
            pad = LANES - n_experts - n_groups
            w_router = jnp.concatenate([w_router_expert[l].astype(F32), w_router_group[l].astype(F32),
                                        jnp.zeros((d, pad), F32)], axis=1)
            w_router_hi = w_router.astype(BF16)
            w_router_lo = (w_router - w_router_hi.astype(F32)).astype(BF16)
            b_router = jnp.concatenate([b_router_expert[l].astype(F32), b_router_group[l].astype(F32),
                                        jnp.zeros((pad,), F32)]).reshape(1, LANES)
            x2, ht_slabs, route, route_t, counts = _out_route(
                merged, w_out[l].astype(BF16), xb, row(norm_ffn_w[l]),
                jnp.concatenate([w_router_hi, w_router_lo], axis=1), b_router,
                n_experts, n_groups, tm=ROUTE_TILE)

            pos1, pos2, inv, block_expert, valid = _dispatch_plan(route_t, counts, n_experts,
                                                                  MOE_BLOCK)
            y_slabs = _experts(block_expert, valid, inv, ht_slabs, w_e_gate[l], w_e_up[l],
                               w_e_down[l], MOE_BLOCK)
            tc = COMBINE_TILE
            pos = jnp.concatenate([pos1.reshape(seq // tc, 1, tc), pos2.reshape(seq // tc, 1, tc)],
                                  axis=2)
            xb = _combine(pos, y_slabs, x2, route, row(norm_final_w),
                          final_norm=(l == depth - 1), tm=tc)
        outs.append(xb)
    return jnp.stack(outs)
```

```python
import functools

import jax
import jax.numpy as jnp
from jax import lax
from jax.experimental import pallas as pl
from jax.experimental.pallas import tpu as pltpu

F32 = jnp.float32
BF16 = jnp.bfloat16
HIGHEST = lax.Precision.HIGHEST

NORM_EPS = 1e-6
LRU_C = 8.0
S5_GROUP = 16
CONV_WIDTH = 4
SUBLANES = 8
LANES = 128
S5_CHUNK = SUBLANES
S5_SET = LANES // S5_GROUP
VMEM_LIMIT = 56 * 1024 * 1024
ROUTE_TILE = 512
MOE_BLOCK = 256
COMBINE_TILE = 256


def _params(n_axes, vmem=VMEM_LIMIT):
    return pltpu.CompilerParams(dimension_semantics=("arbitrary",) * n_axes,
                                vmem_limit_bytes=vmem)


def _rms(x, w):
    ms = jnp.mean(x * x, axis=-1, keepdims=True)
    return x * lax.rsqrt(ms + NORM_EPS) * w


def _in_proj_kernel(x_ref, nw_ref, w_ref, h_ref, u_ref, xl_ref, yl_ref, *, n_u, n_x):
    j = pl.program_id(1)

    @pl.when(j == 0)
    def _():
        h_ref[...] = _rms(x_ref[...], nw_ref[...]).astype(BF16)

    p = jnp.dot(h_ref[...], w_ref[...], preferred_element_type=F32)

    @pl.when(j < n_u)
    def _():
        u_ref[...] = p

    @pl.when(jnp.logical_and(j >= n_u, j < n_u + n_x))
    def _():
        xl_ref[...] = p

    @pl.when(j >= n_u + n_x)
    def _():
        yl_ref[...] = p


def _in_proj(x, norm_w, w, s5_w, lru_w, tm=1024, tn=512):
    seq, d = x.shape
    n_u, n_x = s5_w // tn, lru_w // tn
    kern = functools.partial(_in_proj_kernel, n_u=n_u, n_x=n_x)
    return pl.pallas_call(
        kern,
        grid=(seq // tm, n_u + 2 * n_x),
        in_specs=[
            pl.BlockSpec((tm, d), lambda i, j: (i, 0)),
            pl.BlockSpec((1, d), lambda i, j: (0, 0)),
            pl.BlockSpec((d, tn), lambda i, j: (0, j)),
        ],
        out_specs=[
            pl.BlockSpec((tm, d), lambda i, j: (i, 0)),
            pl.BlockSpec((tm, tn), lambda i, j: (i, jnp.minimum(j, n_u - 1))),
            pl.BlockSpec((tm, tn), lambda i, j: (i, jnp.clip(j - n_u, 0, n_x - 1))),
            pl.BlockSpec((tm, tn), lambda i, j: (i, jnp.clip(j - n_u - n_x, 0, n_x - 1))),
        ],
        out_shape=[
            jax.ShapeDtypeStruct((seq, d), BF16),
            jax.ShapeDtypeStruct((seq, s5_w), F32),
            jax.ShapeDtypeStruct((seq, lru_w), F32),
            jax.ShapeDtypeStruct((seq, lru_w), F32),
        ],
        compiler_params=_params(2),
        name="in_proj",
    )(x, norm_w, w)


def _s5_kernel(u_ref, t_ref, e_ref, m_ref, a_ref, d_ref, y_ref, x_scr, e_scr, s_scr, c_scr):
    tt = pl.program_id(1)
    n_chunks = x_scr.shape[0]
    half = e_scr.shape[1] // 2
    n_col = half // LANES

    @pl.when(tt == 0)
    def _():
        c_scr[...] = jnp.zeros_like(c_scr)

    for j in range(S5_CHUNK):
        x_scr[:, j * LANES:(j + 1) * LANES] = (
            u_ref[pl.ds(j, n_chunks, stride=S5_CHUNK), :].astype(BF16))

    e_scr[...] = jnp.dot(x_scr[...], e_ref[0], preferred_element_type=F32)

    row = lax.broadcasted_iota(jnp.int32, (SUBLANES, LANES), 0)

    def body(t, carry):
        r0 = pl.multiple_of(t * SUBLANES, SUBLANES)
        new = []
        for q in range(n_col):
            c_re, c_im = carry[2 * q], carry[2 * q + 1]
            re_cols = slice(q * LANES, (q + 1) * LANES)
            im_cols = slice(half + q * LANES, half + (q + 1) * LANES)
            x_re = e_scr[pl.ds(r0, SUBLANES), re_cols]
            x_im = e_scr[pl.ds(r0, SUBLANES), im_cols]
            for s in (1, 2, 4):
                p_re = a_ref[0, s - 1:s, re_cols]
                p_im = a_ref[0, SUBLANES + s - 1:SUBLANES + s, re_cols]
                keep = row >= s
                s_re = jnp.where(keep, pltpu.roll(x_re, s, 0), 0.0)
                s_im = jnp.where(keep, pltpu.roll(x_im, s, 0), 0.0)
                x_re, x_im = (x_re + p_re * s_re - p_im * s_im,
                              x_im + p_re * s_im + p_im * s_re)
            t_re = a_ref[0, 0:SUBLANES, re_cols]
            t_im = a_ref[0, SUBLANES:2 * SUBLANES, re_cols]
            x_re, x_im = (x_re + t_re * c_re - t_im * c_im,
                          x_im + t_re * c_im + t_im * c_re)
            s_scr[pl.ds(r0, SUBLANES), re_cols] = jnp.where(row >= 1, pltpu.roll(x_re, 1, 0), c_re)
            s_scr[pl.ds(r0, SUBLANES), im_cols] = jnp.where(row >= 1, pltpu.roll(x_im, 1, 0), c_im)
            new += [x_re[SUBLANES - 1:SUBLANES, :], x_im[SUBLANES - 1:SUBLANES, :]]
        return tuple(new)

    init = []
    for q in range(n_col):
        init += [c_scr[0:1, q * LANES:(q + 1) * LANES],
                 c_scr[0:1, half + q * LANES:half + (q + 1) * LANES]]
    last = lax.fori_loop(0, n_chunks // SUBLANES, body, tuple(init))
    for q in range(n_col):
        c_scr[0:1, q * LANES:(q + 1) * LANES] = last[2 * q]
        c_scr[0:1, half + q * LANES:half + (q + 1) * LANES] = last[2 * q + 1]

    y = jnp.dot(x_scr[...], t_ref[0], preferred_element_type=F32)
    y = y + jnp.dot(s_scr[...].astype(BF16), m_ref[0], preferred_element_type=F32)
    for j in range(S5_CHUNK):
        rows = pl.ds(j, n_chunks, stride=S5_CHUNK)
        yj = y[:, j * LANES:(j + 1) * LANES] + d_ref[...] * u_ref[rows, :]
        y_ref[rows, :] = jax.nn.gelu(yj)


def _s5(u, t_set, e_set, m_set, a_set, d_skip, tile=4096):
    seq, width = u.shape
    n_sets = width // LANES
    n_chunks = tile // S5_CHUNK
    kdim = S5_CHUNK * LANES
    sdim = e_set.shape[2]
    return pl.pallas_call(
        _s5_kernel,
        grid=(n_sets, seq // tile),
        in_specs=[
            pl.BlockSpec((tile, LANES), lambda s, t: (t, s)),
            pl.BlockSpec((1, kdim, kdim), lambda s, t: (s, 0, 0)),
            pl.BlockSpec((1, kdim, sdim), lambda s, t: (s, 0, 0)),
            pl.BlockSpec((1, sdim, kdim), lambda s, t: (s, 0, 0)),
            pl.BlockSpec((1, 2 * SUBLANES, sdim // 2), lambda s, t: (s, 0, 0)),
            pl.BlockSpec((1, LANES), lambda s, t: (0, s)),
        ],
        out_specs=pl.BlockSpec((tile, LANES), lambda s, t: (t, s)),
        out_shape=jax.ShapeDtypeStruct((seq, width), F32),
        scratch_shapes=[
            pltpu.VMEM((n_chunks, kdim), BF16),
            pltpu.VMEM((n_chunks, sdim), F32),
            pltpu.VMEM((n_chunks, sdim), F32),
            pltpu.VMEM((SUBLANES, sdim), F32),
        ],
        compiler_params=_params(2),
        name="s5",
    )(u, t_set, e_set, m_set, a_set, d_skip)


def _cmul(a, b):
    return a[0] * b[0] - a[1] * b[1], a[0] * b[1] + a[1] * b[0]


def _s5_matrices(lam_re, lam_im, log_step, b_re, b_im, c_re, c_im):
    g, p = lam_re.shape
    gs, tc, ns = S5_GROUP, S5_CHUNK, S5_SET
    n_sets = g // ns
    lam = (lam_re.astype(F32), lam_im.astype(F32))
    step = jnp.exp(log_step.astype(F32))[:, None]
    mag = jnp.exp(lam[0] * step)
    lam_bar = (mag * jnp.cos(lam[1] * step), mag * jnp.sin(lam[1] * step))
    den = lam[0] * lam[0] + lam[1] * lam[1]
    coef = _cmul((lam_bar[0] - 1.0, lam_bar[1]), (lam[0] / den, -lam[1] / den))
    b_c = (b_re.astype(F32), b_im.astype(F32))
    b_bar = _cmul((coef[0][..., None], coef[1][..., None]), b_c)
    c_c = (c_re.astype(F32), c_im.astype(F32))

    pows = [(jnp.ones_like(lam_bar[0]), jnp.zeros_like(lam_bar[0]))]
    for _ in range(tc):
        pows.append(_cmul(pows[-1], lam_bar))
    pw = (jnp.stack([q[0] for q in pows]), jnp.stack([q[1] for q in pows]))

    w = _cmul((pw[0][:tc, :, :, None], pw[1][:tc, :, :, None]), (b_bar[0][None], b_bar[1][None]))
    kmat = (jnp.einsum('gop,tgpc->tgoc', c_c[0], w[0], precision=HIGHEST)
            - jnp.einsum('gop,tgpc->tgoc', c_c[1], w[1], precision=HIGHEST))

    def diag_expand(a, row_group, col_group):
        tiled = jnp.tile(a, (1,) * (a.ndim - 1) + (ns,))
        r = jnp.arange(tiled.shape[-2])[:, None] // row_group
        c = jnp.arange(tiled.shape[-1])[None, :] // col_group
        return jnp.where(r == c, tiled, 0.0)

    k_blk = kmat.reshape(tc, n_sets, ns, gs, gs).transpose(0, 1, 2, 4, 3)
    k_blk = diag_expand(k_blk.reshape(tc, n_sets, ns * gs, gs), gs, gs)
    t_rows = [jnp.concatenate([jnp.zeros_like(k_blk[:j]), k_blk[:tc - j]], axis=0)
              for j in range(tc)]
    t_set = jnp.stack(t_rows).transpose(2, 0, 3, 1, 4).reshape(n_sets, tc * LANES, tc * LANES)

    def e_half(x):
        x = x[::-1].reshape(tc, n_sets, ns, p, gs).transpose(0, 1, 2, 4, 3)
        x = diag_expand(x.reshape(tc, n_sets, ns * gs, p), gs, p)
        return x.transpose(1, 0, 2, 3).reshape(n_sets, tc * LANES, ns * p)
    e_set = jnp.concatenate([e_half(w[0]), e_half(w[1])], axis=2)

    m = _cmul((c_c[0].transpose(0, 2, 1)[None], c_c[1].transpose(0, 2, 1)[None]),
              (pw[0][1:, :, :, None], pw[1][1:, :, :, None]))

    def m_half(x):
        x = diag_expand(x.reshape(tc, n_sets, ns * p, gs), p, gs)
        return x.transpose(1, 2, 0, 3).reshape(n_sets, ns * p, tc * LANES)
    m_set = jnp.concatenate([m_half(m[0]), m_half(-m[1])], axis=1)

    a_c = (pw[0][tc], pw[1][tc])
    a_pows = [a_c]
    for _ in range(SUBLANES - 1):
        a_pows.append(_cmul(a_pows[-1], a_c))
    a_set = jnp.concatenate([jnp.stack([q[0] for q in a_pows]), jnp.stack([q[1] for q in a_pows])])
    a_set = a_set.reshape(2 * SUBLANES, n_sets, ns * p).transpose(1, 0, 2)
    return t_set.astype(BF16), e_set.astype(BF16), m_set.astype(BF16), a_set


def _rglru_kernel(xl_ref, yl_ref, cw_ref, cb_ref, w_ref, ba_ref, bx_ref, nls_ref, o_ref,
                  tail, a_scr, b_scr, h_scr, *, heads, head_dim):
    i = pl.program_id(0)
    tm, width = xl_ref.shape

    @pl.when(i == 0)
    def _():
        tail[...] = jnp.zeros_like(tail)
        h_scr[...] = jnp.zeros_like(h_scr)

    first = jnp.logical_and(lax.broadcasted_iota(jnp.int32, (tm, head_dim), 0) == 0, i == 0)
    row8 = lax.broadcasted_iota(jnp.int32, (SUBLANES, head_dim), 0)
    for hd in range(heads):
        cs = slice(hd * head_dim, (hd + 1) * head_dim)
        x = xl_ref[:, cs]
        prev = tail[:, cs]
        xc = cb_ref[:, cs] + x * cw_ref[CONV_WIDTH - 1:CONV_WIDTH, cs]
        for s in range(1, CONV_WIDTH):
            sh = pltpu.roll(x, s, 0)
            head = jnp.where(row8 < s, pltpu.roll(prev, s, 0), sh[0:SUBLANES])
            sh = jnp.concatenate([head, sh[SUBLANES:]], axis=0)
            xc = xc + sh * cw_ref[CONV_WIDTH - 1 - s:CONV_WIDTH - s, cs]
        g = jnp.dot(xc.astype(BF16), w_ref[hd], preferred_element_type=F32)
        r = jax.nn.sigmoid(g[:, :head_dim] + ba_ref[:, cs])
        gi = jax.nn.sigmoid(g[:, head_dim:] + bx_ref[:, cs])
        log_a = nls_ref[:, cs] * r
        a = jnp.exp(log_a)
        mult = jnp.sqrt(-jnp.tanh(log_a) * (a * a + 1.0))
        mult = jnp.where(first, 1.0, mult)
        a_scr[:, cs] = a
        b_scr[:, cs] = xc * gi * mult
    tail[...] = xl_ref[tm - SUBLANES:tm, :]

    row = lax.broadcasted_iota(jnp.int32, (SUBLANES, width), 0)

    def body(t, h_prev):
        r0 = pl.multiple_of(t * SUBLANES, SUBLANES)
        a = a_scr[pl.ds(r0, SUBLANES), :]
        b = b_scr[pl.ds(r0, SUBLANES), :]
        for s in (1, 2, 4):
            keep = row >= s
            b = b + a * jnp.where(keep, pltpu.roll(b, s, 0), 0.0)
            a = a * jnp.where(keep, pltpu.roll(a, s, 0), 1.0)
        h = b + a * h_prev
        y = yl_ref[pl.ds(r0, SUBLANES), :]
        o_ref[pl.ds(r0, SUBLANES), :] = (h * jax.nn.gelu(y)).astype(o_ref.dtype)
        return h[SUBLANES - 1:SUBLANES, :]

    h_scr[...] = lax.fori_loop(0, tm // SUBLANES, body, h_scr[...])


def _rglru(xl, yl, conv_w, conv_b, w_cat, b_a, b_x, nls, heads, head_dim, tm=512):
    seq, width = xl.shape
    kern = functools.partial(_rglru_kernel, heads=heads, head_dim=head_dim)
    row = lambda i: (0, 0)
    return pl.pallas_call(
        kern,
        grid=(seq // tm,),
        in_specs=[
            pl.BlockSpec((tm, width), lambda i: (i, 0)),
            pl.BlockSpec((tm, width), lambda i: (i, 0)),
            pl.BlockSpec((CONV_WIDTH, width), row),
            pl.BlockSpec((1, width), row),
            pl.BlockSpec((heads, head_dim, 2 * head_dim), lambda i: (0, 0, 0)),
            pl.BlockSpec((1, width), row),
            pl.BlockSpec((1, width), row),
            pl.BlockSpec((1, width), row),
        ],
        out_specs=pl.BlockSpec((tm, width), lambda i: (i, 0)),
        out_shape=jax.ShapeDtypeStruct((seq, width), BF16),
        scratch_shapes=[
            pltpu.VMEM((SUBLANES, width), F32),
            pltpu.VMEM((tm, width), F32),
            pltpu.VMEM((tm, width), F32),
            pltpu.VMEM((1, width), F32),
        ],
        compiler_params=_params(1),
        name="rglru",
    )(xl, yl, conv_w, conv_b, w_cat, b_a, b_x, nls)


def _merge_kernel(h_ref, y_ref, ol_ref, wglu_ref, bglu_ref, wgs_ref, wgl_ref, bgs_ref, bgl_ref,
                  wps_ref, wpl_ref, o_ref, os_scr):
    j = pl.program_id(1)

    @pl.when(j == 0)
    def _():
        y = y_ref[...]
        glu = jax.nn.sigmoid(
            jnp.dot(y.astype(BF16), wglu_ref[...], preferred_element_type=F32) + bglu_ref[...])
        os_scr[...] = (y * glu).astype(BF16)

    h = h_ref[...]
    g_s5 = jax.nn.sigmoid(jnp.dot(h, wgs_ref[...], preferred_element_type=F32) + bgs_ref[...])
    g_lru = jax.nn.sigmoid(jnp.dot(h, wgl_ref[...], preferred_element_type=F32) + bgl_ref[...])
    br_s5 = jnp.dot(os_scr[...], wps_ref[...], preferred_element_type=F32)
    br_lru = jnp.dot(ol_ref[...], wpl_ref[...], preferred_element_type=F32)
    o_ref[...] = (g_s5 * br_s5 + g_lru * br_lru).astype(o_ref.dtype)


def _merge(h, y_s5, out_lru, w_glu, b_glu, w_in, gate_col, b_gate, wp_s5, wp_lru,
           tm=1024, tn=512):
    seq, d = h.shape
    s5_w, lru_w = y_s5.shape[1], out_lru.shape[1]
    const = lambda i, j: (0, 0)
    c0, nd = gate_col // tn, d // tn
    return pl.pallas_call(
        _merge_kernel,
        grid=(seq // tm, nd),
        in_specs=[
            pl.BlockSpec((tm, d), lambda i, j: (i, 0)),
            pl.BlockSpec((tm, s5_w), lambda i, j: (i, 0)),
            pl.BlockSpec((tm, lru_w), lambda i, j: (i, 0)),
            pl.BlockSpec((s5_w, s5_w), const),
            pl.BlockSpec((1, s5_w), const),
            pl.BlockSpec((d, tn), lambda i, j: (0, c0 + j)),
            pl.BlockSpec((d, tn), lambda i, j: (0, c0 + nd + j)),
            pl.BlockSpec((1, tn), lambda i, j: (0, j)),
            pl.BlockSpec((1, tn), lambda i, j: (0, nd + j)),
            pl.BlockSpec((s5_w, tn), lambda i, j: (0, j)),
            pl.BlockSpec((lru_w, tn), lambda i, j: (0, j)),
        ],
        out_specs=pl.BlockSpec((tm, tn), lambda i, j: (i, j)),
        out_shape=jax.ShapeDtypeStruct((seq, d), BF16),
        scratch_shapes=[pltpu.VMEM((tm, s5_w), BF16)],
        compiler_params=_params(2),
        name="merge",
    )(h, y_s5, out_lru, w_glu, b_glu, w_in, w_in, b_gate, b_gate, wp_s5, wp_lru)


def _n_slabs(d):
    assert d % LANES == 0
    return d // LANES


def _store_slabs(ref, x):
    for s in range(ref.shape[0]):
        ref[s] = x[:, s * LANES:(s + 1) * LANES]


def _slab_row(ref, g):
    rows = ref.shape[2]
    shift = rows.bit_length() - 1
    assert rows == 1 << shift
    return ref.at[lax.shift_right_logical(g, shift), :, pl.ds(jnp.bitwise_and(g, rows - 1), 1), :]


def _slab_rows(ref, n):
    assert n <= ref.shape[2]
    return ref.at[0, :, pl.ds(0, n), :]


def _route(logits, n_experts, n_groups):
    per_group = n_experts // n_groups
    lane = lax.broadcasted_iota(jnp.int32, logits.shape, 1)
    big = jnp.int32(LANES)
    neg = jnp.float32(-jnp.inf)
    is_g = jnp.logical_and(lane >= n_experts, lane < n_experts + n_groups)
    lg = jnp.where(is_g, logits, neg)
    g_max = jnp.max(lg, axis=-1, keepdims=True)
    g_lane = jnp.min(jnp.where(lg == g_max, lane, big), axis=-1, keepdims=True)
    g_top = 1.0 / jnp.sum(jnp.where(is_g, jnp.exp(lg - g_max), 0.0), axis=-1, keepdims=True)
    g_idx = g_lane - n_experts
    in_group = jnp.logical_and(lane >= g_idx * per_group, lane < (g_idx + 1) * per_group)
    le = jnp.where(in_group, logits, neg)
    m1 = jnp.max(le, axis=-1, keepdims=True)
    i1 = jnp.min(jnp.where(le == m1, lane, big), axis=-1, keepdims=True)
    le2 = jnp.where(lane == i1, neg, le)
    m2 = jnp.max(le2, axis=-1, keepdims=True)
    i2 = jnp.min(jnp.where(le2 == m2, lane, big), axis=-1, keepdims=True)
    r = jnp.exp(m2 - m1)
    w1 = g_top / (1.0 + r)
    w2 = g_top * r / (1.0 + r)
    return i1, i2, w1, w2


def _out_route_kernel(m_ref, w_ref, x_ref, nw_ref, wr_ref, br_ref, x2_ref, ht_ref, route_ref,
                      route_t_ref, cnt_ref, cnt_scr, *, n_experts, n_groups):
    i = pl.program_id(0)
    tm = x_ref.shape[0]

    @pl.when(i == 0)
    def _():
        cnt_scr[...] = jnp.zeros_like(cnt_scr)

    x2 = x_ref[...] + jnp.dot(m_ref[...], w_ref[...], preferred_element_type=F32)
    x2_ref[...] = x2
    ht = _rms(x2, nw_ref[...])
    _store_slabs(ht_ref.at[0], ht)
    ht_hi = ht.astype(BF16)
    ht_lo = (ht - ht_hi.astype(F32)).astype(BF16)
    both = jnp.dot(ht_hi, wr_ref[...], preferred_element_type=F32)
    logits = (both[:, :LANES] + both[:, LANES:]
              + jnp.dot(ht_lo, wr_ref[:, :LANES], preferred_element_type=F32) + br_ref[...])
    i1, i2, w1, w2 = _route(logits, n_experts, n_groups)

    lane = lax.broadcasted_iota(jnp.int32, (tm, LANES), 1)
    oh1 = (lane == i1).astype(F32)
    oh2 = (lane == i2).astype(F32)
    oh = oh1 + oh2
    earlier = (lax.broadcasted_iota(jnp.int32, (tm, tm), 0)
               > lax.broadcasted_iota(jnp.int32, (tm, tm), 1)).astype(BF16)
    before = jnp.dot(earlier, oh.astype(BF16), preferred_element_type=F32) + cnt_scr[0:1, :]
    r1 = jnp.sum(before * oh1, axis=-1, keepdims=True)
    r2 = jnp.sum(before * oh2, axis=-1, keepdims=True)
    cnt_scr[0:1, :] = cnt_scr[0:1, :] + jnp.sum(oh, axis=0, keepdims=True)
    cnt_ref[...] = cnt_scr[...]

    cols = [i1.astype(F32), i2.astype(F32), r1, r2, w1, w2]
    route = jnp.zeros((tm, LANES), F32)
    for k, v in enumerate(cols):
        route = jnp.where(lane == k, v, route)
    route_ref[...] = route
    route_t_ref[...] = route.T[0:SUBLANES, :]


def _out_route(merged, w_out, x, norm_w, w_router, b_router, n_experts, n_groups, tm):
    seq, d = x.shape
    const = lambda i: (0, 0)
    kern = functools.partial(_out_route_kernel, n_experts=n_experts, n_groups=n_groups)
    return pl.pallas_call(
        kern,
        grid=(seq // tm,),
        in_specs=[
            pl.BlockSpec((tm, d), lambda i: (i, 0)),
            pl.BlockSpec((d, d), const),
            pl.BlockSpec((tm, d), lambda i: (i, 0)),
            pl.BlockSpec((1, d), const),
            pl.BlockSpec((d, 2 * LANES), const),
            pl.BlockSpec((1, LANES), const),
        ],
        out_specs=[
            pl.BlockSpec((tm, d), lambda i: (i, 0)),
            pl.BlockSpec((1, _n_slabs(d), tm, LANES), lambda i: (i, 0, 0, 0)),
            pl.BlockSpec((tm, LANES), lambda i: (i, 0)),
            pl.BlockSpec((SUBLANES, tm), lambda i: (0, i)),
            pl.BlockSpec((SUBLANES, LANES), const),
        ],
        out_shape=[
            jax.ShapeDtypeStruct((seq, d), F32),
            jax.ShapeDtypeStruct((seq // tm, _n_slabs(d), tm, LANES), F32),
            jax.ShapeDtypeStruct((seq, LANES), F32),
            jax.ShapeDtypeStruct((SUBLANES, seq), F32),
            jax.ShapeDtypeStruct((SUBLANES, LANES), F32),
        ],
        scratch_shapes=[pltpu.VMEM((SUBLANES, LANES), F32)],
        compiler_params=_params(1),
        name="out_route",
    )(merged, w_out, x, norm_w, w_router, b_router)


DMA_UNROLL = 8


def _dispatch_kernel(npad_ref, padstart_ref, nused_ref, idx_ref, ht_ref, xs_hbm, stage, zero_scr,
                     sem, pad_sem):
    i = pl.program_id(0)
    nt = pl.num_programs(0)
    n_experts = npad_ref.shape[0]
    nb, _, bm, _ = xs_hbm.shape
    tm = stage.shape[2]
    slot = i % 2
    assert tm <= bm

    def wait_rows(dsem):
        for _ in range(2):
            pltpu.make_async_copy(stage.at[0], _slab_rows(xs_hbm, tm), dsem).wait()

    stage[slot] = ht_ref[0]

    def body(g, carry):
        for k in range(DMA_UNROLL):
            r = g * DMA_UNROLL + k
            src = stage.at[slot, :, pl.ds(r, 1), :]
            for half in range(2):
                dst = _slab_row(xs_hbm, idx_ref[0, 0, half * tm + r])
                pltpu.make_async_copy(src, dst, sem.at[slot]).start(priority=half)
        return carry
    lax.fori_loop(0, tm // DMA_UNROLL, body, 0)

    @pl.when(i > 0)
    def _():
        wait_rows(sem.at[1 - slot])

    @pl.when(i == nt - 1)
    def _():
        wait_rows(sem.at[slot])
        zero_scr[...] = jnp.zeros_like(zero_scr)
        zero_row = zero_scr.at[:, pl.ds(0, 1), :]
        for e in range(n_experts):
            def pad_start(j, carry):
                pltpu.make_async_copy(zero_row, _slab_row(xs_hbm, padstart_ref[e] + j),
                                      pad_sem).start()
                return carry

            def pad_wait(j, carry):
                pltpu.make_async_copy(zero_row, _slab_row(xs_hbm, 0), pad_sem).wait()
                return carry
            lax.fori_loop(0, npad_ref[e], pad_start, 0)
            lax.fori_loop(0, npad_ref[e], pad_wait, 0)

        def blk_start(b, carry):
            pltpu.make_async_copy(zero_scr, xs_hbm.at[b], pad_sem).start()
            return carry

        def blk_wait(b, carry):
            pltpu.make_async_copy(zero_scr, xs_hbm.at[0], pad_sem).wait()
            return carry
        lax.fori_loop(nused_ref[0], nb, blk_start, 0)
        lax.fori_loop(nused_ref[0], nb, blk_wait, 0)


def _dispatch(npad, padstart, n_used, pos, ht_slabs, nb, bm, tm):
    _, ns, tile_rows, _ = ht_slabs.shape
    seq = pos.shape[0] * tm
    per_tile = tile_rows // tm
    assert per_tile * tm == tile_rows
    grid_spec = pltpu.PrefetchScalarGridSpec(
        num_scalar_prefetch=3,
        grid=(seq // tm,),
        in_specs=[
            pl.BlockSpec((1, 1, 2 * tm), lambda i, a, b, c: (i, 0, 0), memory_space=pltpu.SMEM),
            pl.BlockSpec((1, ns, tm, LANES),
                         lambda i, a, b, c: (i // per_tile, 0, i % per_tile, 0)),
        ],
        out_specs=pl.BlockSpec(memory_space=pl.ANY),
        scratch_shapes=[
            pltpu.VMEM((2, ns, tm, LANES), F32),
            pltpu.VMEM((ns, bm, LANES), F32),
            pltpu.SemaphoreType.DMA((2,)),
            pltpu.SemaphoreType.DMA(()),
        ],
    )
    return pl.pallas_call(
        _dispatch_kernel,
        grid_spec=grid_spec,
        out_shape=jax.ShapeDtypeStruct((nb, ns, bm, LANES), F32),
        compiler_params=pltpu.CompilerParams(dimension_semantics=("arbitrary",),
                                             has_side_effects=True),
        name="dispatch",
    )(npad, padstart, n_used, pos, ht_slabs)


def _experts_kernel(be_ref, nused_ref, xs_ref, wg_ref, wu_ref, wd_ref, y_ref,
                    xs_scr, wgu_scr, wd_scr):
    b = pl.program_id(0)
    ff = wd_scr.shape[0]

    @pl.when(b < nused_ref[0])
    def _():
        @pl.when(jnp.logical_or(b == 0, be_ref[b] != be_ref[jnp.maximum(b - 1, 0)]))
        def _():
            wgu_scr[:, :ff] = wg_ref[0].astype(BF16)
            wgu_scr[:, ff:] = wu_ref[0].astype(BF16)
            wd_scr[...] = wd_ref[0].astype(BF16)

        for s in range(xs_ref.shape[1]):
            xs_scr[:, s * LANES:(s + 1) * LANES] = xs_ref[0, s].astype(BF16)
        hid = jnp.dot(xs_scr[...], wgu_scr[...], preferred_element_type=F32)
        act = (jax.nn.silu(hid[:, :ff]) * hid[:, ff:]).astype(BF16)
        y = jnp.dot(act, wd_scr[...], preferred_element_type=F32)
        _store_slabs(y_ref.at[0], y)

    @pl.when(b >= nused_ref[0])
    def _():
        y_ref[...] = jnp.zeros_like(y_ref)


def _experts(block_expert, n_used, xs_slabs, w_gate, w_up, w_down):
    nb, ns, bm, _ = xs_slabs.shape
    n_experts, d, ff = w_gate.shape
    grid_spec = pltpu.PrefetchScalarGridSpec(
        num_scalar_prefetch=2,
        grid=(nb,),
        in_specs=[
            pl.BlockSpec((1, ns, bm, LANES), lambda b, be, nu: (b, 0, 0, 0)),
            pl.BlockSpec((1, d, ff), lambda b, be, nu: (be[b], 0, 0)),
            pl.BlockSpec((1, d, ff), lambda b, be, nu: (be[b], 0, 0)),
            pl.BlockSpec((1, ff, d), lambda b, be, nu: (be[b], 0, 0)),
        ],
        out_specs=pl.BlockSpec((1, ns, bm, LANES), lambda b, be, nu: (b, 0, 0, 0)),
        scratch_shapes=[
            pltpu.VMEM((bm, d), BF16),
            pltpu.VMEM((d, 2 * ff), BF16),
            pltpu.VMEM((ff, d), BF16),
        ],
    )
    return pl.pallas_call(
        _experts_kernel,
        grid_spec=grid_spec,
        out_shape=jax.ShapeDtypeStruct((nb, ns, bm, LANES), F32),
        compiler_params=_params(1),
        name="experts",
    )(block_expert, n_used, xs_slabs, w_gate, w_up, w_down)


def _combine_kernel(idx_ref, idx_next_ref, y_hbm, x_ref, route_ref, fw_ref, o_ref, ybuf, sem,
                    *, final_norm):
    i = pl.program_id(0)
    nt = pl.num_programs(0)
    tm = x_ref.shape[0]
    slot = i % 2

    def start_gather(ids, buf, dsem):
        def body(g, carry):
            for k in range(DMA_UNROLL):
                r = g * DMA_UNROLL + k
                pltpu.make_async_copy(_slab_row(y_hbm, ids[0, 0, r]),
                                      buf.at[:, pl.ds(r, 1), :], dsem).start(priority=k % 2)
            return carry
        lax.fori_loop(0, 2 * tm // DMA_UNROLL, body, 0)

    @pl.when(i == 0)
    def _():
        start_gather(idx_ref, ybuf.at[0], sem.at[0])

    @pl.when(i + 1 < nt)
    def _():
        start_gather(idx_next_ref, ybuf.at[1 - slot], sem.at[1 - slot])

    buf = ybuf.at[slot]
    for half in range(2):
        pltpu.make_async_copy(_slab_rows(y_hbm, tm), buf.at[:, pl.ds(half * tm, tm), :],
                              sem.at[slot]).wait()
    w1 = route_ref[:, 4:5]
    w2 = route_ref[:, 5:6]
    ssq = jnp.zeros((tm, 1), F32)
    for s in range(buf.shape[0]):
        cols = slice(s * LANES, (s + 1) * LANES)
        v = x_ref[:, cols] + w1 * buf[s, 0:tm, :] + w2 * buf[s, tm:2 * tm, :]
        o_ref[:, cols] = v
        ssq = ssq + jnp.sum(v * v, axis=-1, keepdims=True)
    if final_norm:
        scale = lax.rsqrt(ssq / o_ref.shape[1] + NORM_EPS)
        o_ref[...] = o_ref[...] * scale * fw_ref[...]


def _combine(pos, y_rows, x2, route, final_w, final_norm, tm):
    seq, d = x2.shape
    nt = seq // tm
    kern = functools.partial(_combine_kernel, final_norm=final_norm)
    return pl.pallas_call(
        kern,
        grid=(nt,),
        in_specs=[
            pl.BlockSpec((1, 1, 2 * tm), lambda i: (i, 0, 0), memory_space=pltpu.SMEM),
            pl.BlockSpec((1, 1, 2 * tm), lambda i: (jnp.minimum(i + 1, nt - 1), 0, 0),
                         memory_space=pltpu.SMEM),
            pl.BlockSpec(memory_space=pl.ANY),
            pl.BlockSpec((tm, d), lambda i: (i, 0)),
            pl.BlockSpec((tm, LANES), lambda i: (i, 0)),
            pl.BlockSpec((1, d), lambda i: (0, 0)),
        ],
        out_specs=pl.BlockSpec((tm, d), lambda i: (i, 0)),
        out_shape=jax.ShapeDtypeStruct((seq, d), F32),
        scratch_shapes=[
            pltpu.VMEM((2, _n_slabs(d), 2 * tm, LANES), F32),
            pltpu.SemaphoreType.DMA((2,)),
        ],
        compiler_params=_params(1),
        name="combine",
    )(pos, pos, y_rows, x2, route, final_w)


def _dispatch_plan(route_t, counts, n_experts, bm):
    seq = route_t.shape[1]
    nb = (2 * seq) // bm + n_experts
    e1, e2, r1, r2 = (route_t[k].astype(jnp.int32) for k in range(4))
    cnt = counts[0, :n_experts].astype(jnp.int32)
    cnt_pad = ((cnt + bm - 1) // bm) * bm
    ends = jnp.cumsum(cnt_pad)
    starts = ends - cnt_pad
    experts = jnp.arange(n_experts, dtype=jnp.int32)

    def start_of(e):
        return jnp.sum(jnp.where(e[:, None] == experts[None, :], starts[None, :], 0), axis=1)
    pos1 = start_of(e1) + r1
    pos2 = start_of(e2) + r2
    blk_start = jnp.arange(nb, dtype=jnp.int32) * bm
    n_before = jnp.sum((ends[None, :] <= blk_start[:, None]).astype(jnp.int32), axis=1)
    block_expert = jnp.minimum(n_before, n_experts - 1)
    n_used = (ends[-1:] // bm).astype(jnp.int32)
    return pos1, pos2, cnt_pad - cnt, starts + cnt, block_expert, n_used, nb


def kernel(x, norm_mix_w, w_in, b_gate, s5_lam_re, s5_lam_im, s5_log_step, s5_b_re, s5_b_im, s5_c_re, s5_c_im, s5_d, s5_w_glu, s5_b_glu, lru_conv_w, lru_conv_b, lru_w_a, lru_b_a, lru_w_x, lru_b_x, lru_lambda, w_proj_s5, w_proj_lru, w_out, norm_ffn_w, w_router_group, b_router_group, w_router_expert, b_router_expert, w_e_gate, w_e_up, w_e_down, norm_final_w):
    depth = w_in.shape[0]
    bsz, seq, d = x.shape
    s5_w = s5_w_glu.shape[-1]
    lru_w = lru_conv_b.shape[-1]
    heads, head_dim = lru_w_a.shape[1], lru_w_a.shape[2]
    n_groups = w_router_group.shape[-1]
    n_experts = w_router_expert.shape[-1]
    o3 = s5_w + 2 * lru_w

    outs = []
    for b in range(bsz):
        xb = x[b]
        for l in range(depth):
            row = lambda v: v.astype(F32).reshape(1, -1)
            w_in_l = w_in[l].astype(BF16)
            h, u, xl, yl = _in_proj(xb, row(norm_mix_w[l]), w_in_l, s5_w, lru_w)

            t_set, e_set, m_set, a_set = _s5_matrices(
                s5_lam_re[l], s5_lam_im[l], s5_log_step[l], s5_b_re[l], s5_b_im[l],
                s5_c_re[l], s5_c_im[l])
            y_s5 = _s5(u, t_set, e_set, m_set, a_set, row(s5_d[l]))

            w_cat = jnp.concatenate([lru_w_a[l], lru_w_x[l]], axis=-1).astype(BF16)
            nls = -LRU_C * jax.nn.softplus(-lru_lambda[l].astype(F32))
            out_lru = _rglru(xl, yl, lru_conv_w[l].astype(F32), row(lru_conv_b[l]), w_cat,
                             row(lru_b_a[l]), row(lru_b_x[l]), row(nls), heads, head_dim)

            merged = _merge(h, y_s5, out_lru, s5_w_glu[l].astype(BF16), row(s5_b_glu[l]),
                            w_in_l, o3, row(b_gate[l]),
                            w_proj_s5[l].astype(BF16), w_proj_lru[l].astype(BF16))
            pad = LANES - n_experts - n_groups
            w_router = jnp.concatenate([w_router_expert[l].astype(F32), w_router_group[l].astype(F32),
                                        jnp.zeros((d, pad), F32)], axis=1)
            w_router_hi = w_router.astype(BF16)
            w_router_lo = (w_router - w_router_hi.astype(F32)).astype(BF16)
            b_router = jnp.concatenate([b_router_expert[l].astype(F32), b_router_group[l].astype(F32),
                                        jnp.zeros((pad,), F32)]).reshape(1, LANES)
            x2, ht_slabs, route, route_t, counts = _out_route(
                merged, w_out[l].astype(BF16), xb, row(norm_ffn_w[l]),
                jnp.concatenate([w_router_hi, w_router_lo], axis=1), b_router,
                n_experts, n_groups, tm=ROUTE_TILE)

            pos1, pos2, npad, padstart, block_expert, n_used, nb = _dispatch_plan(
                route_t, counts, n_experts, MOE_BLOCK)
            tc = COMBINE_TILE
            pos = jnp.concatenate([pos1.reshape(seq // tc, 1, tc), pos2.reshape(seq // tc, 1, tc)],
                                  axis=2)
            xs_slabs = _dispatch(npad, padstart, n_used, pos, ht_slabs, nb, MOE_BLOCK, tc)
            y_rows = _experts(block_expert, n_used, xs_slabs, w_e_gate[l], w_e_up[l], w_e_down[l])
            xb = _combine(pos, y_rows, x2, route, row(norm_final_w),
                          final_norm=(l == depth - 1), tm=tc)
        outs.append(xb)
    return jnp.stack(outs)
```

```python
import functools

import jax
import jax.numpy as jnp
from jax import lax
from jax.experimental import pallas as pl
from jax.experimental.pallas import tpu as pltpu

F32 = jnp.float32
BF16 = jnp.bfloat16
HIGHEST = lax.Precision.HIGHEST

NORM_EPS = 1e-6
LRU_C = 8.0
S5_GROUP = 16
CONV_WIDTH = 4
SUBLANES = 8
LANES = 128
S5_CHUNK = SUBLANES
S5_SET = LANES // S5_GROUP
VMEM_LIMIT = 56 * 1024 * 1024
ROUTE_TILE = 512
MOE_BLOCK = 256
COMBINE_TILE = 256


def _params(n_axes, vmem=VMEM_LIMIT):
    return pltpu.CompilerParams(dimension_semantics=("arbitrary",) * n_axes,
                                vmem_limit_bytes=vmem)


def _rms(x, w):
    ms = jnp.mean(x * x, axis=-1, keepdims=True)
    return x * lax.rsqrt(ms + NORM_EPS) * w


def _in_proj_kernel(x_ref, nw_ref, w_ref, h_ref, p_ref):
    @pl.when(pl.program_id(1) == 0)
    def _():
        h_ref[...] = _rms(x_ref[...], nw_ref[...]).astype(BF16)

    p_ref[...] = jnp.dot(h_ref[...], w_ref[...], preferred_element_type=F32)


def _in_proj(x, norm_w, w, n_cols, tm=1024, tn=512):
    seq, d = x.shape
    return pl.pallas_call(
        _in_proj_kernel,
        grid=(seq // tm, n_cols // tn),
        in_specs=[
            pl.BlockSpec((tm, d), lambda i, j: (i, 0)),
            pl.BlockSpec((1, d), lambda i, j: (0, 0)),
            pl.BlockSpec((d, tn), lambda i, j: (0, j)),
        ],
        out_specs=[
            pl.BlockSpec((tm, d), lambda i, j: (i, 0)),
            pl.BlockSpec((tm, tn), lambda i, j: (i, j)),
        ],
        out_shape=[
            jax.ShapeDtypeStruct((seq, d), BF16),
            jax.ShapeDtypeStruct((seq, n_cols), F32),
        ],
        compiler_params=_params(2),
        name="in_proj",
    )(x, norm_w, w)


def _s5_kernel(u_ref, t_ref, e_ref, m_ref, a_ref, d_ref, y_ref, x_scr, e_scr, s_scr, c_scr):
    tt = pl.program_id(1)
    n_chunks = x_scr.shape[0]
    half = e_scr.shape[1] // 2
    n_col = half // LANES

    @pl.when(tt == 0)
    def _():
        c_scr[...] = jnp.zeros_like(c_scr)

    for j in range(S5_CHUNK):
        x_scr[:, j * LANES:(j + 1) * LANES] = (
            u_ref[pl.ds(j, n_chunks, stride=S5_CHUNK), :].astype(BF16))

    e_scr[...] = jnp.dot(x_scr[...], e_ref[0], preferred_element_type=F32)

    row = lax.broadcasted_iota(jnp.int32, (SUBLANES, LANES), 0)

    def body(t, carry):
        r0 = pl.multiple_of(t * SUBLANES, SUBLANES)
        new = []
        for q in range(n_col):
            c_re, c_im = carry[2 * q], carry[2 * q + 1]
            re_cols = slice(q * LANES, (q + 1) * LANES)
            im_cols = slice(half + q * LANES, half + (q + 1) * LANES)
            x_re = e_scr[pl.ds(r0, SUBLANES), re_cols]
            x_im = e_scr[pl.ds(r0, SUBLANES), im_cols]
            for s in (1, 2, 4):
                p_re = a_ref[0, s - 1:s, re_cols]
                p_im = a_ref[0, SUBLANES + s - 1:SUBLANES + s, re_cols]
                keep = row >= s
                s_re = jnp.where(keep, pltpu.roll(x_re, s, 0), 0.0)
                s_im = jnp.where(keep, pltpu.roll(x_im, s, 0), 0.0)
                x_re, x_im = (x_re + p_re * s_re - p_im * s_im,
                              x_im + p_re * s_im + p_im * s_re)
            t_re = a_ref[0, 0:SUBLANES, re_cols]
            t_im = a_ref[0, SUBLANES:2 * SUBLANES, re_cols]
            x_re, x_im = (x_re + t_re * c_re - t_im * c_im,
                          x_im + t_re * c_im + t_im * c_re)
            s_scr[pl.ds(r0, SUBLANES), re_cols] = jnp.where(row >= 1, pltpu.roll(x_re, 1, 0), c_re)
            s_scr[pl.ds(r0, SUBLANES), im_cols] = jnp.where(row >= 1, pltpu.roll(x_im, 1, 0), c_im)
            new += [x_re[SUBLANES - 1:SUBLANES, :], x_im[SUBLANES - 1:SUBLANES, :]]
        return tuple(new)

    init = []
    for q in range(n_col):
        init += [c_scr[0:1, q * LANES:(q + 1) * LANES],
                 c_scr[0:1, half + q * LANES:half + (q + 1) * LANES]]
    last = lax.fori_loop(0, n_chunks // SUBLANES, body, tuple(init))
    for q in range(n_col):
        c_scr[0:1, q * LANES:(q + 1) * LANES] = last[2 * q]
        c_scr[0:1, half + q * LANES:half + (q + 1) * LANES] = last[2 * q + 1]

    y = jnp.dot(x_scr[...], t_ref[0], preferred_element_type=F32)
    y = y + jnp.dot(s_scr[...].astype(BF16), m_ref[0], preferred_element_type=F32)
    for j in range(S5_CHUNK):
        rows = pl.ds(j, n_chunks, stride=S5_CHUNK)
        yj = y[:, j * LANES:(j + 1) * LANES] + d_ref[...] * u_ref[rows, :]
        y_ref[rows, :] = jax.nn.gelu(yj)


def _s5(proj, col, t_set, e_set, m_set, a_set, d_skip, tile=4096):
    seq = proj.shape[0]
    width = d_skip.shape[1]
    n_sets = width // LANES
    n_chunks = tile // S5_CHUNK
    kdim = S5_CHUNK * LANES
    sdim = e_set.shape[2]
    c0 = col // LANES
    assert c0 * LANES == col
    return pl.pallas_call(
        _s5_kernel,
        grid=(n_sets, seq // tile),
        in_specs=[
            pl.BlockSpec((tile, LANES), lambda s, t: (t, c0 + s)),
            pl.BlockSpec((1, kdim, kdim), lambda s, t: (s, 0, 0)),
            pl.BlockSpec((1, kdim, sdim), lambda s, t: (s, 0, 0)),
            pl.BlockSpec((1, sdim, kdim), lambda s, t: (s, 0, 0)),
            pl.BlockSpec((1, 2 * SUBLANES, sdim // 2), lambda s, t: (s, 0, 0)),
            pl.BlockSpec((1, LANES), lambda s, t: (0, s)),
        ],
        out_specs=pl.BlockSpec((tile, LANES), lambda s, t: (t, s)),
        out_shape=jax.ShapeDtypeStruct((seq, width), F32),
        scratch_shapes=[
            pltpu.VMEM((n_chunks, kdim), BF16),
            pltpu.VMEM((n_chunks, sdim), F32),
            pltpu.VMEM((n_chunks, sdim), F32),
            pltpu.VMEM((SUBLANES, sdim), F32),
        ],
        compiler_params=_params(2),
        name="s5",
    )(proj, t_set, e_set, m_set, a_set, d_skip)


def _cmul(a, b):
    return a[0] * b[0] - a[1] * b[1], a[0] * b[1] + a[1] * b[0]


def _s5_matrices(lam_re, lam_im, log_step, b_re, b_im, c_re, c_im):
    g, p = lam_re.shape
    gs, tc, ns = S5_GROUP, S5_CHUNK, S5_SET
    n_sets = g // ns
    lam = (lam_re.astype(F32), lam_im.astype(F32))
    step = jnp.exp(log_step.astype(F32))[:, None]
    mag = jnp.exp(lam[0] * step)
    lam_bar = (mag * jnp.cos(lam[1] * step), mag * jnp.sin(lam[1] * step))
    den = lam[0] * lam[0] + lam[1] * lam[1]
    coef = _cmul((lam_bar[0] - 1.0, lam_bar[1]), (lam[0] / den, -lam[1] / den))
    b_c = (b_re.astype(F32), b_im.astype(F32))
    b_bar = _cmul((coef[0][..., None], coef[1][..., None]), b_c)
    c_c = (c_re.astype(F32), c_im.astype(F32))

    pows = [(jnp.ones_like(lam_bar[0]), jnp.zeros_like(lam_bar[0]))]
    for _ in range(tc):
        pows.append(_cmul(pows[-1], lam_bar))
    pw = (jnp.stack([q[0] for q in pows]), jnp.stack([q[1] for q in pows]))

    w = _cmul((pw[0][:tc, :, :, None], pw[1][:tc, :, :, None]), (b_bar[0][None], b_bar[1][None]))
    kmat = (jnp.einsum('gop,tgpc->tgoc', c_c[0], w[0], precision=HIGHEST)
            - jnp.einsum('gop,tgpc->tgoc', c_c[1], w[1], precision=HIGHEST))

    def diag_expand(a, row_group, col_group):
        tiled = jnp.tile(a, (1,) * (a.ndim - 1) + (ns,))
        r = jnp.arange(tiled.shape[-2])[:, None] // row_group
        c = jnp.arange(tiled.shape[-1])[None, :] // col_group
        return jnp.where(r == c, tiled, 0.0)

    k_blk = kmat.reshape(tc, n_sets, ns, gs, gs).transpose(0, 1, 2, 4, 3)
    k_blk = diag_expand(k_blk.reshape(tc, n_sets, ns * gs, gs), gs, gs)
    t_rows = [jnp.concatenate([jnp.zeros_like(k_blk[:j]), k_blk[:tc - j]], axis=0)
              for j in range(tc)]
    t_set = jnp.stack(t_rows).transpose(2, 0, 3, 1, 4).reshape(n_sets, tc * LANES, tc * LANES)

    def e_half(x):
        x = x[::-1].reshape(tc, n_sets, ns, p, gs).transpose(0, 1, 2, 4, 3)
        x = diag_expand(x.reshape(tc, n_sets, ns * gs, p), gs, p)
        return x.transpose(1, 0, 2, 3).reshape(n_sets, tc * LANES, ns * p)
    e_set = jnp.concatenate([e_half(w[0]), e_half(w[1])], axis=2)

    m = _cmul((c_c[0].transpose(0, 2, 1)[None], c_c[1].transpose(0, 2, 1)[None]),
              (pw[0][1:, :, :, None], pw[1][1:, :, :, None]))

    def m_half(x):
        x = diag_expand(x.reshape(tc, n_sets, ns * p, gs), p, gs)
        return x.transpose(1, 2, 0, 3).reshape(n_sets, ns * p, tc * LANES)
    m_set = jnp.concatenate([m_half(m[0]), m_half(-m[1])], axis=1)

    a_c = (pw[0][tc], pw[1][tc])
    a_pows = [a_c]
    for _ in range(SUBLANES - 1):
        a_pows.append(_cmul(a_pows[-1], a_c))
    a_set = jnp.concatenate([jnp.stack([q[0] for q in a_pows]), jnp.stack([q[1] for q in a_pows])])
    a_set = a_set.reshape(2 * SUBLANES, n_sets, ns * p).transpose(1, 0, 2)
    return t_set.astype(BF16), e_set.astype(BF16), m_set.astype(BF16), a_set


def _rglru_kernel(*refs, heads, head_dim, n_in):
    xl_refs, yl_refs = refs[:n_in], refs[n_in:2 * n_in]
    (cw_ref, cb_ref, w_ref, ba_ref, bx_ref, nls_ref, o_ref,
     tail, a_scr, b_scr, h_scr) = refs[2 * n_in:]
    i = pl.program_id(0)
    tm, blk_w = xl_refs[0].shape
    per_blk = blk_w // head_dim
    seg = tm // SUBLANES

    def head_cols(blocks, hd):
        lo = (hd % per_blk) * head_dim
        return blocks[hd // per_blk], slice(lo, lo + head_dim)

    @pl.when(i == 0)
    def _():
        tail[...] = jnp.zeros_like(tail)
        h_scr[...] = jnp.zeros_like(h_scr)

    first = jnp.logical_and(lax.broadcasted_iota(jnp.int32, (tm, head_dim), 0) == 0, i == 0)
    row8 = lax.broadcasted_iota(jnp.int32, (SUBLANES, head_dim), 0)
    for hd in range(heads):
        cs = slice(hd * head_dim, (hd + 1) * head_dim)
        x_ref, xs_cols = head_cols(xl_refs, hd)
        x = x_ref[:, xs_cols]
        prev = tail[:, cs]
        xc = cb_ref[:, cs] + x * cw_ref[CONV_WIDTH - 1:CONV_WIDTH, cs]
        for s in range(1, CONV_WIDTH):
            sh = pltpu.roll(x, s, 0)
            head = jnp.where(row8 < s, pltpu.roll(prev, s, 0), sh[0:SUBLANES])
            sh = jnp.concatenate([head, sh[SUBLANES:]], axis=0)
            xc = xc + sh * cw_ref[CONV_WIDTH - 1 - s:CONV_WIDTH - s, cs]
        g = jnp.dot(xc.astype(BF16), w_ref[hd], preferred_element_type=F32)
        r = jax.nn.sigmoid(g[:, :head_dim] + ba_ref[:, cs])
        gi = jax.nn.sigmoid(g[:, head_dim:] + bx_ref[:, cs])
        log_a = nls_ref[:, cs] * r
        a = jnp.exp(log_a)
        mult = jnp.sqrt(-jnp.tanh(log_a) * (a * a + 1.0))
        mult = jnp.where(first, 1.0, mult)
        b = xc * gi * mult
        for sg in range(SUBLANES):
            rows = pl.ds(sg, seg, stride=SUBLANES)
            a_scr[hd, rows, :] = a[sg * seg:(sg + 1) * seg]
            b_scr[hd, rows, :] = b[sg * seg:(sg + 1) * seg]
        tail[:, cs] = x[tm - SUBLANES:tm]

    def body(k, carry):
        r0 = pl.multiple_of(k * SUBLANES, SUBLANES)
        new = []
        for hd in range(heads):
            h_loc, a_cum = carry[2 * hd], carry[2 * hd + 1]
            a = a_scr[hd, pl.ds(r0, SUBLANES), :]
            h_loc = a * h_loc + b_scr[hd, pl.ds(r0, SUBLANES), :]
            a_cum = a * a_cum
            b_scr[hd, pl.ds(r0, SUBLANES), :] = h_loc
            a_scr[hd, pl.ds(r0, SUBLANES), :] = a_cum
            new += [h_loc, a_cum]
        return tuple(new)

    init = (jnp.zeros((SUBLANES, head_dim), F32), jnp.ones((SUBLANES, head_dim), F32)) * heads
    ends = lax.fori_loop(0, seg, body, init)

    for hd in range(heads):
        cs = slice(hd * head_dim, (hd + 1) * head_dim)
        h_end, a_end = ends[2 * hd], ends[2 * hd + 1]
        y_ref, y_cols = head_cols(yl_refs, hd)
        state = h_scr[:, cs]
        for sg in range(SUBLANES):
            rows = pl.ds(sg, seg, stride=SUBLANES)
            h = b_scr[hd, rows, :] + a_scr[hd, rows, :] * state
            y = y_ref[sg * seg:(sg + 1) * seg, y_cols]
            o_ref[sg * seg:(sg + 1) * seg, cs] = (h * jax.nn.gelu(y)).astype(o_ref.dtype)
            state = h_end[sg:sg + 1, :] + a_end[sg:sg + 1, :] * state
        h_scr[:, cs] = state


def _rglru(proj, col, conv_w, conv_b, w_cat, b_a, b_x, nls, heads, head_dim, tm=512, blk_w=512):
    seq = proj.shape[0]
    width = heads * head_dim
    assert head_dim == LANES
    n_in, c0 = width // blk_w, col // blk_w
    assert n_in * blk_w == width and c0 * blk_w == col and blk_w % head_dim == 0
    kern = functools.partial(_rglru_kernel, heads=heads, head_dim=head_dim, n_in=n_in)
    row = lambda i: (0, 0)
    col_block = lambda c: pl.BlockSpec((tm, blk_w), lambda i: (i, c))
    return pl.pallas_call(
        kern,
        grid=(seq // tm,),
        in_specs=[col_block(c0 + c) for c in range(2 * n_in)] + [
            pl.BlockSpec((CONV_WIDTH, width), row),
            pl.BlockSpec((1, width), row),
            pl.BlockSpec((heads, head_dim, 2 * head_dim), lambda i: (0, 0, 0)),
            pl.BlockSpec((1, width), row),
            pl.BlockSpec((1, width), row),
            pl.BlockSpec((1, width), row),
        ],
        out_specs=pl.BlockSpec((tm, width), lambda i: (i, 0)),
        out_shape=jax.ShapeDtypeStruct((seq, width), BF16),
        scratch_shapes=[
            pltpu.VMEM((SUBLANES, width), F32),
            pltpu.VMEM((heads, tm, head_dim), F32),
            pltpu.VMEM((heads, tm, head_dim), F32),
            pltpu.VMEM((1, width), F32),
        ],
        compiler_params=_params(1),
        name="rglru",
    )(*([proj] * (2 * n_in)), conv_w, conv_b, w_cat, b_a, b_x, nls)


def _merge_kernel(h_ref, y_ref, ol_ref, wglu_ref, bglu_ref, wgs_ref, wgl_ref, bgs_ref, bgl_ref,
                  wps_ref, wpl_ref, o_ref, os_scr):
    j = pl.program_id(1)

    @pl.when(j == 0)
    def _():
        y = y_ref[...]
        glu = jax.nn.sigmoid(
            jnp.dot(y.astype(BF16), wglu_ref[...], preferred_element_type=F32) + bglu_ref[...])
        os_scr[...] = (y * glu).astype(BF16)

    h = h_ref[...]
    g_s5 = jax.nn.sigmoid(jnp.dot(h, wgs_ref[...], preferred_element_type=F32) + bgs_ref[...])
    g_lru = jax.nn.sigmoid(jnp.dot(h, wgl_ref[...], preferred_element_type=F32) + bgl_ref[...])
    br_s5 = jnp.dot(os_scr[...], wps_ref[...], preferred_element_type=F32)
    br_lru = jnp.dot(ol_ref[...], wpl_ref[...], preferred_element_type=F32)
    o_ref[...] = (g_s5 * br_s5 + g_lru * br_lru).astype(o_ref.dtype)


def _merge(h, y_s5, out_lru, w_glu, b_glu, w_in, gate_col, b_gate, wp_s5, wp_lru,
           tm=1024, tn=512):
    seq, d = h.shape
    s5_w, lru_w = y_s5.shape[1], out_lru.shape[1]
    const = lambda i, j: (0, 0)
    c0, nd = gate_col // tn, d // tn
    return pl.pallas_call(
        _merge_kernel,
        grid=(seq // tm, nd),
        in_specs=[
            pl.BlockSpec((tm, d), lambda i, j: (i, 0)),
            pl.BlockSpec((tm, s5_w), lambda i, j: (i, 0)),
            pl.BlockSpec((tm, lru_w), lambda i, j: (i, 0)),
            pl.BlockSpec((s5_w, s5_w), const),
            pl.BlockSpec((1, s5_w), const),
            pl.BlockSpec((d, tn), lambda i, j: (0, c0 + j)),
            pl.BlockSpec((d, tn), lambda i, j: (0, c0 + nd + j)),
            pl.BlockSpec((1, tn), lambda i, j: (0, j)),
            pl.BlockSpec((1, tn), lambda i, j: (0, nd + j)),
            pl.BlockSpec((s5_w, tn), lambda i, j: (0, j)),
            pl.BlockSpec((lru_w, tn), lambda i, j: (0, j)),
        ],
        out_specs=pl.BlockSpec((tm, tn), lambda i, j: (i, j)),
        out_shape=jax.ShapeDtypeStruct((seq, d), BF16),
        scratch_shapes=[pltpu.VMEM((tm, s5_w), BF16)],
        compiler_params=_params(2),
        name="merge",
    )(h, y_s5, out_lru, w_glu, b_glu, w_in, w_in, b_gate, b_gate, wp_s5, wp_lru)


def _n_slabs(d):
    assert d % LANES == 0
    return d // LANES


def _store_slabs(ref, x):
    for s in range(ref.shape[0]):
        ref[s] = x[:, s * LANES:(s + 1) * LANES]


def _slab_row(ref, g):
    rows = ref.shape[2]
    shift = rows.bit_length() - 1
    assert rows == 1 << shift
    return ref.at[lax.shift_right_logical(g, shift), :, pl.ds(jnp.bitwise_and(g, rows - 1), 1), :]


def _slab_rows(ref, n):
    assert n <= ref.shape[2]
    return ref.at[0, :, pl.ds(0, n), :]


def _route(logits, n_experts, n_groups):
    per_group = n_experts // n_groups
    lane = lax.broadcasted_iota(jnp.int32, logits.shape, 1)
    big = jnp.int32(LANES)
    neg = jnp.float32(-jnp.inf)
    is_g = jnp.logical_and(lane >= n_experts, lane < n_experts + n_groups)
    lg = jnp.where(is_g, logits, neg)
    g_max = jnp.max(lg, axis=-1, keepdims=True)
    g_lane = jnp.min(jnp.where(lg == g_max, lane, big), axis=-1, keepdims=True)
    g_top = 1.0 / jnp.sum(jnp.where(is_g, jnp.exp(lg - g_max), 0.0), axis=-1, keepdims=True)
    g_idx = g_lane - n_experts
    in_group = jnp.logical_and(lane >= g_idx * per_group, lane < (g_idx + 1) * per_group)
    le = jnp.where(in_group, logits, neg)
    m1 = jnp.max(le, axis=-1, keepdims=True)
    i1 = jnp.min(jnp.where(le == m1, lane, big), axis=-1, keepdims=True)
    le2 = jnp.where(lane == i1, neg, le)
    m2 = jnp.max(le2, axis=-1, keepdims=True)
    i2 = jnp.min(jnp.where(le2 == m2, lane, big), axis=-1, keepdims=True)
    r = jnp.exp(m2 - m1)
    w1 = g_top / (1.0 + r)
    w2 = g_top * r / (1.0 + r)
    return i1, i2, w1, w2


def _out_route_kernel(m_ref, w_ref, x_ref, nw_ref, wr_ref, br_ref, x2_ref, ht_ref, route_ref,
                      route_t_ref, cnt_ref, cnt_scr, *, n_experts, n_groups):
    i = pl.program_id(0)
    tm = x_ref.shape[0]

    @pl.when(i == 0)
    def _():
        cnt_scr[...] = jnp.zeros_like(cnt_scr)

    x2 = x_ref[...] + jnp.dot(m_ref[...], w_ref[...], preferred_element_type=F32)
    x2_ref[...] = x2
    ht = _rms(x2, nw_ref[...])
    _store_slabs(ht_ref.at[0], ht)
    ht_hi = ht.astype(BF16)
    ht_lo = (ht - ht_hi.astype(F32)).astype(BF16)
    both = jnp.dot(ht_hi, wr_ref[...], preferred_element_type=F32)
    logits = (both[:, :LANES] + both[:, LANES:]
              + jnp.dot(ht_lo, wr_ref[:, :LANES], preferred_element_type=F32) + br_ref[...])
    i1, i2, w1, w2 = _route(logits, n_experts, n_groups)

    lane = lax.broadcasted_iota(jnp.int32, (tm, LANES), 1)
    oh1 = (lane == i1).astype(F32)
    oh2 = (lane == i2).astype(F32)
    oh = oh1 + oh2
    earlier = (lax.broadcasted_iota(jnp.int32, (tm, tm), 0)
               > lax.broadcasted_iota(jnp.int32, (tm, tm), 1)).astype(BF16)
    before = jnp.dot(earlier, oh.astype(BF16), preferred_element_type=F32) + cnt_scr[0:1, :]
    r1 = jnp.sum(before * oh1, axis=-1, keepdims=True)
    r2 = jnp.sum(before * oh2, axis=-1, keepdims=True)
    cnt_scr[0:1, :] = cnt_scr[0:1, :] + jnp.sum(oh, axis=0, keepdims=True)
    cnt_ref[...] = cnt_scr[...]

    cols = [i1.astype(F32), i2.astype(F32), r1, r2, w1, w2]
    route = jnp.zeros((tm, LANES), F32)
    for k, v in enumerate(cols):
        route = jnp.where(lane == k, v, route)
    route_ref[...] = route
    route_t_ref[...] = route.T[0:SUBLANES, :]


def _out_route(merged, w_out, x, norm_w, w_router, b_router, n_experts, n_groups, tm):
    seq, d = x.shape
    const = lambda i: (0, 0)
    kern = functools.partial(_out_route_kernel, n_experts=n_experts, n_groups=n_groups)
    return pl.pallas_call(
        kern,
        grid=(seq // tm,),
        in_specs=[
            pl.BlockSpec((tm, d), lambda i: (i, 0)),
            pl.BlockSpec((d, d), const),
            pl.BlockSpec((tm, d), lambda i: (i, 0)),
            pl.BlockSpec((1, d), const),
            pl.BlockSpec((d, 2 * LANES), const),
            pl.BlockSpec((1, LANES), const),
        ],
        out_specs=[
            pl.BlockSpec((tm, d), lambda i: (i, 0)),
            pl.BlockSpec((1, _n_slabs(d), tm, LANES), lambda i: (i, 0, 0, 0)),
            pl.BlockSpec((tm, LANES), lambda i: (i, 0)),
            pl.BlockSpec((SUBLANES, tm), lambda i: (0, i)),
            pl.BlockSpec((SUBLANES, LANES), const),
        ],
        out_shape=[
            jax.ShapeDtypeStruct((seq, d), F32),
            jax.ShapeDtypeStruct((seq // tm, _n_slabs(d), tm, LANES), F32),
            jax.ShapeDtypeStruct((seq, LANES), F32),
            jax.ShapeDtypeStruct((SUBLANES, seq), F32),
            jax.ShapeDtypeStruct((SUBLANES, LANES), F32),
        ],
        scratch_shapes=[pltpu.VMEM((SUBLANES, LANES), F32)],
        compiler_params=_params(1),
        name="out_route",
    )(merged, w_out, x, norm_w, w_router, b_router)


DMA_UNROLL = 8


def _dispatch_kernel(npad_ref, padstart_ref, nused_ref, idx_ref, ht_ref, xs_hbm, stage, zero_scr,
                     sem, pad_sem):
    i = pl.program_id(0)
    nt = pl.num_programs(0)
    n_experts = npad_ref.shape[0]
    nb, _, bm, _ = xs_hbm.shape
    tm = stage.shape[2]
    slot = i % 2
    assert tm <= bm

    def wait_rows(dsem):
        for _ in range(2):
            pltpu.make_async_copy(stage.at[0], _slab_rows(xs_hbm, tm), dsem).wait()

    stage[slot] = ht_ref[0]

    def body(g, carry):
        for k in range(DMA_UNROLL):
            r = g * DMA_UNROLL + k
            src = stage.at[slot, :, pl.ds(r, 1), :]
            for half in range(2):
                dst = _slab_row(xs_hbm, idx_ref[0, 0, half * tm + r])
                pltpu.make_async_copy(src, dst, sem.at[slot]).start(priority=half)
        return carry
    lax.fori_loop(0, tm // DMA_UNROLL, body, 0)

    @pl.when(i > 0)
    def _():
        wait_rows(sem.at[1 - slot])

    @pl.when(i == nt - 1)
    def _():
        wait_rows(sem.at[slot])
        zero_scr[...] = jnp.zeros_like(zero_scr)
        zero_row = zero_scr.at[:, pl.ds(0, 1), :]
        shift = bm.bit_length() - 1
        for e in range(n_experts):
            p0 = padstart_ref[e]
            n_pad = npad_ref[e]
            blk = lax.shift_right_logical(p0, shift)
            row0 = jnp.bitwise_and(p0, bm - 1)
            n_single = jnp.minimum(jnp.bitwise_and(-row0, SUBLANES - 1), n_pad)
            q0 = row0 + n_single

            def single_start(j, carry):
                pltpu.make_async_copy(zero_row, _slab_row(xs_hbm, p0 + j), pad_sem).start()
                return carry

            def single_wait(j, carry):
                pltpu.make_async_copy(zero_row, _slab_row(xs_hbm, 0), pad_sem).wait()
                return carry

            def chunks(start):
                q = q0
                size = SUBLANES
                while size < bm:
                    take = jnp.logical_and(jnp.bitwise_and(q, size) != 0, n_pad > n_single)
                    src = zero_scr.at[:, pl.ds(0, size), :]

                    @pl.when(take)
                    def _():
                        if start:
                            dst = xs_hbm.at[blk, :, pl.ds(pl.multiple_of(q, SUBLANES), size), :]
                            pltpu.make_async_copy(src, dst, pad_sem).start()
                        else:
                            pltpu.make_async_copy(src, xs_hbm.at[0, :, pl.ds(0, size), :],
                                                  pad_sem).wait()
                    q = q + jnp.where(take, size, 0)
                    size *= 2
            lax.fori_loop(0, n_single, single_start, 0)
            chunks(True)
            lax.fori_loop(0, n_single, single_wait, 0)
            chunks(False)

        def blk_start(b, carry):
            pltpu.make_async_copy(zero_scr, xs_hbm.at[b], pad_sem).start()
            return carry

        def blk_wait(b, carry):
            pltpu.make_async_copy(zero_scr, xs_hbm.at[0], pad_sem).wait()
            return carry
        lax.fori_loop(nused_ref[0], nb, blk_start, 0)
        lax.fori_loop(nused_ref[0], nb, blk_wait, 0)


def _dispatch(npad, padstart, n_used, pos, ht_slabs, nb, bm, tm):
    _, ns, tile_rows, _ = ht_slabs.shape
    seq = pos.shape[0] * tm
    per_tile = tile_rows // tm
    assert per_tile * tm == tile_rows
    grid_spec = pltpu.PrefetchScalarGridSpec(
        num_scalar_prefetch=3,
        grid=(seq // tm,),
        in_specs=[
            pl.BlockSpec((1, 1, 2 * tm), lambda i, a, b, c: (i, 0, 0), memory_space=pltpu.SMEM),
            pl.BlockSpec((1, ns, tm, LANES),
                         lambda i, a, b, c: (i // per_tile, 0, i % per_tile, 0)),
        ],
        out_specs=pl.BlockSpec(memory_space=pl.ANY),
        scratch_shapes=[
            pltpu.VMEM((2, ns, tm, LANES), F32),
            pltpu.VMEM((ns, bm, LANES), F32),
            pltpu.SemaphoreType.DMA((2,)),
            pltpu.SemaphoreType.DMA(()),
        ],
    )
    return pl.pallas_call(
        _dispatch_kernel,
        grid_spec=grid_spec,
        out_shape=jax.ShapeDtypeStruct((nb, ns, bm, LANES), F32),
        compiler_params=pltpu.CompilerParams(dimension_semantics=("arbitrary",),
                                             has_side_effects=True),
        name="dispatch",
    )(npad, padstart, n_used, pos, ht_slabs)


def _experts_kernel(be_ref, nused_ref, xs_ref, wg_ref, wu_ref, wd_ref, y_ref,
                    xs_scr, wgu_scr, wd_scr):
    b = pl.program_id(0)
    ff = wd_scr.shape[0]

    @pl.when(b < nused_ref[0])
    def _():
        @pl.when(jnp.logical_or(b == 0, be_ref[b] != be_ref[jnp.maximum(b - 1, 0)]))
        def _():
            wgu_scr[:, :ff] = wg_ref[0].astype(BF16)
            wgu_scr[:, ff:] = wu_ref[0].astype(BF16)
            wd_scr[...] = wd_ref[0].astype(BF16)

        for s in range(xs_ref.shape[1]):
            xs_scr[:, s * LANES:(s + 1) * LANES] = xs_ref[0, s].astype(BF16)
        hid = jnp.dot(xs_scr[...], wgu_scr[...], preferred_element_type=F32)
        act = (jax.nn.silu(hid[:, :ff]) * hid[:, ff:]).astype(BF16)
        y = jnp.dot(act, wd_scr[...], preferred_element_type=F32)
        _store_slabs(y_ref.at[0], y)

    @pl.when(b >= nused_ref[0])
    def _():
        y_ref[...] = jnp.zeros_like(y_ref)


def _experts(block_expert, n_used, xs_slabs, w_gate, w_up, w_down):
    nb, ns, bm, _ = xs_slabs.shape
    n_experts, d, ff = w_gate.shape
    grid_spec = pltpu.PrefetchScalarGridSpec(
        num_scalar_prefetch=2,
        grid=(nb,),
        in_specs=[
            pl.BlockSpec((1, ns, bm, LANES), lambda b, be, nu: (b, 0, 0, 0)),
            pl.BlockSpec((1, d, ff), lambda b, be, nu: (be[b], 0, 0)),
            pl.BlockSpec((1, d, ff), lambda b, be, nu: (be[b], 0, 0)),
            pl.BlockSpec((1, ff, d), lambda b, be, nu: (be[b], 0, 0)),
        ],
        out_specs=pl.BlockSpec((1, ns, bm, LANES), lambda b, be, nu: (b, 0, 0, 0)),
        scratch_shapes=[
            pltpu.VMEM((bm, d), BF16),
            pltpu.VMEM((d, 2 * ff), BF16),
            pltpu.VMEM((ff, d), BF16),
        ],
    )
    return pl.pallas_call(
        _experts_kernel,
        grid_spec=grid_spec,
        out_shape=jax.ShapeDtypeStruct((nb, ns, bm, LANES), F32),
        compiler_params=_params(1),
        name="experts",
    )(block_expert, n_used, xs_slabs, w_gate, w_up, w_down)


def _combine_kernel(idx_ref, idx_next_ref, y_hbm, x_ref, route_ref, fw_ref, o_ref, ybuf, sem,
                    *, final_norm):
    i = pl.program_id(0)
    nt = pl.num_programs(0)
    tm = x_ref.shape[0]
    slot = i % 2

    def start_gather(ids, buf, dsem):
        def body(g, carry):
            for k in range(DMA_UNROLL):
                r = g * DMA_UNROLL + k
                pltpu.make_async_copy(_slab_row(y_hbm, ids[0, 0, r]),
                                      buf.at[:, pl.ds(r, 1), :], dsem).start(priority=k % 2)
            return carry
        lax.fori_loop(0, 2 * tm // DMA_UNROLL, body, 0)

    @pl.when(i == 0)
    def _():
        start_gather(idx_ref, ybuf.at[0], sem.at[0])

    @pl.when(i + 1 < nt)
    def _():
        start_gather(idx_next_ref, ybuf.at[1 - slot], sem.at[1 - slot])

    buf = ybuf.at[slot]
    for half in range(2):
        pltpu.make_async_copy(_slab_rows(y_hbm, tm), buf.at[:, pl.ds(half * tm, tm), :],
                              sem.at[slot]).wait()
    w1 = route_ref[:, 4:5]
    w2 = route_ref[:, 5:6]
    ssq = jnp.zeros((tm, 1), F32)
    for s in range(buf.shape[0]):
        cols = slice(s * LANES, (s + 1) * LANES)
        v = x_ref[:, cols] + w1 * buf[s, 0:tm, :] + w2 * buf[s, tm:2 * tm, :]
        o_ref[:, cols] = v
        ssq = ssq + jnp.sum(v * v, axis=-1, keepdims=True)
    if final_norm:
        scale = lax.rsqrt(ssq / o_ref.shape[1] + NORM_EPS)
        o_ref[...] = o_ref[...] * scale * fw_ref[...]


def _combine(pos, y_rows, x2, route, final_w, final_norm, tm):
    seq, d = x2.shape
    nt = seq // tm
    kern = functools.partial(_combine_kernel, final_norm=final_norm)
    return pl.pallas_call(
        kern,
        grid=(nt,),
        in_specs=[
            pl.BlockSpec((1, 1, 2 * tm), lambda i: (i, 0, 0), memory_space=pltpu.SMEM),
            pl.BlockSpec((1, 1, 2 * tm), lambda i: (jnp.minimum(i + 1, nt - 1), 0, 0),
                         memory_space=pltpu.SMEM),
            pl.BlockSpec(memory_space=pl.ANY),
            pl.BlockSpec((tm, d), lambda i: (i, 0)),
            pl.BlockSpec((tm, LANES), lambda i: (i, 0)),
            pl.BlockSpec((1, d), lambda i: (0, 0)),
        ],
        out_specs=pl.BlockSpec((tm, d), lambda i: (i, 0)),
        out_shape=jax.ShapeDtypeStruct((seq, d), F32),
        scratch_shapes=[
            pltpu.VMEM((2, _n_slabs(d), 2 * tm, LANES), F32),
            pltpu.SemaphoreType.DMA((2,)),
        ],
        compiler_params=_params(1),
        name="combine",
    )(pos, pos, y_rows, x2, route, final_w)


def _dispatch_plan(route_t, counts, n_experts, bm):
    seq = route_t.shape[1]
    nb = (2 * seq) // bm + n_experts
    e1, e2, r1, r2 = (route_t[k].astype(jnp.int32) for k in range(4))
    cnt = counts[0, :n_experts].astype(jnp.int32)
    cnt_pad = ((cnt + bm - 1) // bm) * bm
    ends = jnp.cumsum(cnt_pad)
    starts = ends - cnt_pad
    experts = jnp.arange(n_experts, dtype=jnp.int32)

    def start_of(e):
        return jnp.sum(jnp.where(e[:, None] == experts[None, :], starts[None, :], 0), axis=1)
    pos1 = start_of(e1) + r1
    pos2 = start_of(e2) + r2
    blk_start = jnp.arange(nb, dtype=jnp.int32) * bm
    n_before = jnp.sum((ends[None, :] <= blk_start[:, None]).astype(jnp.int32), axis=1)
    block_expert = jnp.minimum(n_before, n_experts - 1)
    n_used = (ends[-1:] // bm).astype(jnp.int32)
    return pos1, pos2, cnt_pad - cnt, starts + cnt, block_expert, n_used, nb


def kernel(x, norm_mix_w, w_in, b_gate, s5_lam_re, s5_lam_im, s5_log_step, s5_b_re, s5_b_im, s5_c_re, s5_c_im, s5_d, s5_w_glu, s5_b_glu, lru_conv_w, lru_conv_b, lru_w_a, lru_b_a, lru_w_x, lru_b_x, lru_lambda, w_proj_s5, w_proj_lru, w_out, norm_ffn_w, w_router_group, b_router_group, w_router_expert, b_router_expert, w_e_gate, w_e_up, w_e_down, norm_final_w):
    depth = w_in.shape[0]
    bsz, seq, d = x.shape
    s5_w = s5_w_glu.shape[-1]
    lru_w = lru_conv_b.shape[-1]
    heads, head_dim = lru_w_a.shape[1], lru_w_a.shape[2]
    n_groups = w_router_group.shape[-1]
    n_experts = w_router_expert.shape[-1]
    o3 = s5_w + 2 * lru_w

    outs = []
    for b in range(bsz):
        xb = x[b]
        for l in range(depth):
            row = lambda v: v.astype(F32).reshape(1, -1)
            w_in_l = w_in[l].astype(BF16)
            h, proj = _in_proj(xb, row(norm_mix_w[l]), w_in_l, o3)

            t_set, e_set, m_set, a_set = _s5_matrices(
                s5_lam_re[l], s5_lam_im[l], s5_log_step[l], s5_b_re[l], s5_b_im[l],
                s5_c_re[l], s5_c_im[l])
            y_s5 = _s5(proj, 0, t_set, e_set, m_set, a_set, row(s5_d[l]))

            w_cat = jnp.concatenate([lru_w_a[l], lru_w_x[l]], axis=-1).astype(BF16)
            nls = -LRU_C * jax.nn.softplus(-lru_lambda[l].astype(F32))
            out_lru = _rglru(proj, s5_w, lru_conv_w[l].astype(F32), row(lru_conv_b[l]), w_cat,
                             row(lru_b_a[l]), row(lru_b_x[l]), row(nls), heads, head_dim)

            merged = _merge(h, y_s5, out_lru, s5_w_glu[l].astype(BF16), row(s5_b_glu[l]),
                            w_in_l, o3, row(b_gate[l]),
                            w_proj_s5[l].astype(BF16), w_proj_lru[l].astype(BF16))
            pad = LANES - n_experts - n_groups
            w_router = jnp.concatenate([w_router_expert[l].astype(F32), w_router_group[l].astype(F32),
                                        jnp.zeros((d, pad), F32)], axis=1)
            w_router_hi = w_router.astype(BF16)
            w_router_lo = (w_router - w_router_hi.astype(F32)).astype(BF16)
            b_router = jnp.concatenate([b_router_expert[l].astype(F32), b_router_group[l].astype(F32),
                                        jnp.zeros((pad,), F32)]).reshape(1, LANES)
            x2, ht_slabs, route, route_t, counts = _out_route(
                merged, w_out[l].astype(BF16), xb, row(norm_ffn_w[l]),
                jnp.concatenate([w_router_hi, w_router_lo], axis=1), b_router,
                n_experts, n_groups, tm=ROUTE_TILE)

            pos1, pos2, npad, padstart, block_expert, n_used, nb = _dispatch_plan(
                route_t, counts, n_experts, MOE_BLOCK)
            tc = COMBINE_TILE
            pos = jnp.concatenate([pos1.reshape(seq // tc, 1, tc), pos2.reshape(seq // tc, 1, tc)],
                                  axis=2)
            xs_slabs = _dispatch(npad, padstart, n_used, pos, ht_slabs, nb, MOE_BLOCK, tc)
            y_rows = _experts(block_expert, n_used, xs_slabs, w_e_gate[l], w_e_up[l], w_e_down[l])
            xb = _combine(pos, y_rows, x2, route, row(norm_final_w),
                          final_norm=(l == depth - 1), tm=tc)
        outs.append(xb)
    return jnp.stack(outs)
```

```python
import functools

import jax
import jax.numpy as jnp
from jax import lax
from jax.experimental import pallas as pl
from jax.experimental.pallas import tpu as pltpu

F32 = jnp.float32
BF16 = jnp.bfloat16
HIGHEST = lax.Precision.HIGHEST

NORM_EPS = 1e-6
LRU_C = 8.0
S5_GROUP = 16
CONV_WIDTH = 4
SUBLANES = 8
LANES = 128
S5_CHUNK = SUBLANES
S5_SET = LANES // S5_GROUP
VMEM_LIMIT = 56 * 1024 * 1024
ROUTE_TILE = 512
MOE_BLOCK = 256
COMBINE_TILE = 256


def _params(n_axes, vmem=VMEM_LIMIT):
    return pltpu.CompilerParams(dimension_semantics=("arbitrary",) * n_axes,
                                vmem_limit_bytes=vmem)


def _rms(x, w):
    ms = jnp.mean(x * x, axis=-1, keepdims=True)
    return x * lax.rsqrt(ms + NORM_EPS) * w


def _in_proj_kernel(x_hbm, nw_ref, w_ref, h_ref, p_ref, xbuf, sem):
    i = pl.program_id(0)
    ni = pl.num_programs(0)
    tm = xbuf.shape[1]
    slot = i % 2

    def x_copy(tile, s):
        rows = pl.ds(pl.multiple_of(tile * tm, tm), tm)
        return pltpu.make_async_copy(x_hbm.at[rows, :], xbuf.at[s], sem.at[s])

    @pl.when(pl.program_id(1) == 0)
    def _():
        @pl.when(i == 0)
        def _():
            x_copy(0, 0).start()

        @pl.when(i + 1 < ni)
        def _():
            x_copy(i + 1, 1 - slot).start()

        x_copy(i, slot).wait()
        h_ref[...] = _rms(xbuf[slot], nw_ref[...]).astype(BF16)

    p_ref[...] = jnp.dot(h_ref[...], w_ref[...], preferred_element_type=F32)


def _in_proj(x, norm_w, w, n_cols, tm=1024, tn=512):
    seq, d = x.shape
    return pl.pallas_call(
        _in_proj_kernel,
        grid=(seq // tm, n_cols // tn),
        in_specs=[
            pl.BlockSpec(memory_space=pl.ANY),
            pl.BlockSpec((1, d), lambda i, j: (0, 0)),
            pl.BlockSpec((d, tn), lambda i, j: (0, j)),
        ],
        out_specs=[
            pl.BlockSpec((tm, d), lambda i, j: (i, 0)),
            pl.BlockSpec((tm, tn), lambda i, j: (i, j)),
        ],
        out_shape=[
            jax.ShapeDtypeStruct((seq, d), BF16),
            jax.ShapeDtypeStruct((seq, n_cols), F32),
        ],
        scratch_shapes=[pltpu.VMEM((2, tm, d), F32), pltpu.SemaphoreType.DMA((2,))],
        compiler_params=_params(2),
        name="in_proj",
    )(x, norm_w, w)


def _s5_kernel(u_ref, t_ref, e_ref, m_ref, a_ref, d_ref, y_ref, x_scr, e_scr, s_scr, c_scr):
    tt = pl.program_id(1)
    n_chunks = x_scr.shape[0]
    half = e_scr.shape[1] // 2
    n_col = half // LANES

    @pl.when(tt == 0)
    def _():
        c_scr[...] = jnp.zeros_like(c_scr)

    for j in range(S5_CHUNK):
        x_scr[:, j * LANES:(j + 1) * LANES] = (
            u_ref[pl.ds(j, n_chunks, stride=S5_CHUNK), :].astype(BF16))

    e_scr[...] = jnp.dot(x_scr[...], e_ref[0], preferred_element_type=F32)

    row = lax.broadcasted_iota(jnp.int32, (SUBLANES, LANES), 0)

    def body(t, carry):
        r0 = pl.multiple_of(t * SUBLANES, SUBLANES)
        new = []
        for q in range(n_col):
            c_re, c_im = carry[2 * q], carry[2 * q + 1]
            re_cols = slice(q * LANES, (q + 1) * LANES)
            im_cols = slice(half + q * LANES, half + (q + 1) * LANES)
            x_re = e_scr[pl.ds(r0, SUBLANES), re_cols]
            x_im = e_scr[pl.ds(r0, SUBLANES), im_cols]
            for s in (1, 2, 4):
                p_re = a_ref[0, s - 1:s, re_cols]
                p_im = a_ref[0, SUBLANES + s - 1:SUBLANES + s, re_cols]
                keep = row >= s
                s_re = jnp.where(keep, pltpu.roll(x_re, s, 0), 0.0)
                s_im = jnp.where(keep, pltpu.roll(x_im, s, 0), 0.0)
                x_re, x_im = (x_re + p_re * s_re - p_im * s_im,
                              x_im + p_re * s_im + p_im * s_re)
            t_re = a_ref[0, 0:SUBLANES, re_cols]
            t_im = a_ref[0, SUBLANES:2 * SUBLANES, re_cols]
            x_re, x_im = (x_re + t_re * c_re - t_im * c_im,
                          x_im + t_re * c_im + t_im * c_re)
            s_scr[pl.ds(r0, SUBLANES), re_cols] = jnp.where(row >= 1, pltpu.roll(x_re, 1, 0), c_re)
            s_scr[pl.ds(r0, SUBLANES), im_cols] = jnp.where(row >= 1, pltpu.roll(x_im, 1, 0), c_im)
            new += [x_re[SUBLANES - 1:SUBLANES, :], x_im[SUBLANES - 1:SUBLANES, :]]
        return tuple(new)

    init = []
    for q in range(n_col):
        init += [c_scr[0:1, q * LANES:(q + 1) * LANES],
                 c_scr[0:1, half + q * LANES:half + (q + 1) * LANES]]
    last = lax.fori_loop(0, n_chunks // SUBLANES, body, tuple(init))
    for q in range(n_col):
        c_scr[0:1, q * LANES:(q + 1) * LANES] = last[2 * q]
        c_scr[0:1, half + q * LANES:half + (q + 1) * LANES] = last[2 * q + 1]

    y = jnp.dot(x_scr[...], t_ref[0], preferred_element_type=F32)
    y = y + jnp.dot(s_scr[...].astype(BF16), m_ref[0], preferred_element_type=F32)
    for j in range(S5_CHUNK):
        rows = pl.ds(j, n_chunks, stride=S5_CHUNK)
        yj = y[:, j * LANES:(j + 1) * LANES] + d_ref[...] * u_ref[rows, :]
        y_ref[rows, :] = jax.nn.gelu(yj)


def _s5(proj, col, t_set, e_set, m_set, a_set, d_skip, tile=4096):
    seq = proj.shape[0]
    width = d_skip.shape[1]
    n_sets = width // LANES
    n_chunks = tile // S5_CHUNK
    kdim = S5_CHUNK * LANES
    sdim = e_set.shape[2]
    c0 = col // LANES
    assert c0 * LANES == col
    return pl.pallas_call(
        _s5_kernel,
        grid=(n_sets, seq // tile),
        in_specs=[
            pl.BlockSpec((tile, LANES), lambda s, t: (t, c0 + s)),
            pl.BlockSpec((1, kdim, kdim), lambda s, t: (s, 0, 0)),
            pl.BlockSpec((1, kdim, sdim), lambda s, t: (s, 0, 0)),
            pl.BlockSpec((1, sdim, kdim), lambda s, t: (s, 0, 0)),
            pl.BlockSpec((1, 2 * SUBLANES, sdim // 2), lambda s, t: (s, 0, 0)),
            pl.BlockSpec((1, LANES), lambda s, t: (0, s)),
        ],
        out_specs=pl.BlockSpec((tile, LANES), lambda s, t: (t, s)),
        out_shape=jax.ShapeDtypeStruct((seq, width), F32),
        scratch_shapes=[
            pltpu.VMEM((n_chunks, kdim), BF16),
            pltpu.VMEM((n_chunks, sdim), F32),
            pltpu.VMEM((n_chunks, sdim), F32),
            pltpu.VMEM((SUBLANES, sdim), F32),
        ],
        compiler_params=_params(2),
        name="s5",
    )(proj, t_set, e_set, m_set, a_set, d_skip)


def _cmul(a, b):
    return a[0] * b[0] - a[1] * b[1], a[0] * b[1] + a[1] * b[0]


def _s5_matrices(lam_re, lam_im, log_step, b_re, b_im, c_re, c_im):
    g, p = lam_re.shape
    gs, tc, ns = S5_GROUP, S5_CHUNK, S5_SET
    n_sets = g // ns
    lam = (lam_re.astype(F32), lam_im.astype(F32))
    step = jnp.exp(log_step.astype(F32))[:, None]
    mag = jnp.exp(lam[0] * step)
    lam_bar = (mag * jnp.cos(lam[1] * step), mag * jnp.sin(lam[1] * step))
    den = lam[0] * lam[0] + lam[1] * lam[1]
    coef = _cmul((lam_bar[0] - 1.0, lam_bar[1]), (lam[0] / den, -lam[1] / den))
    b_c = (b_re.astype(F32), b_im.astype(F32))
    b_bar = _cmul((coef[0][..., None], coef[1][..., None]), b_c)
    c_c = (c_re.astype(F32), c_im.astype(F32))

    pows = [(jnp.ones_like(lam_bar[0]), jnp.zeros_like(lam_bar[0]))]
    for _ in range(tc):
        pows.append(_cmul(pows[-1], lam_bar))
    pw = (jnp.stack([q[0] for q in pows]), jnp.stack([q[1] for q in pows]))

    w = _cmul((pw[0][:tc, :, :, None], pw[1][:tc, :, :, None]), (b_bar[0][None], b_bar[1][None]))
    kmat = (jnp.einsum('gop,tgpc->tgoc', c_c[0], w[0], precision=HIGHEST)
            - jnp.einsum('gop,tgpc->tgoc', c_c[1], w[1], precision=HIGHEST))

    def diag_expand(a, row_group, col_group):
        tiled = jnp.tile(a, (1,) * (a.ndim - 1) + (ns,))
        r = jnp.arange(tiled.shape[-2])[:, None] // row_group
        c = jnp.arange(tiled.shape[-1])[None, :] // col_group
        return jnp.where(r == c, tiled, 0.0)

    k_blk = kmat.reshape(tc, n_sets, ns, gs, gs).transpose(0, 1, 2, 4, 3)
    k_blk = diag_expand(k_blk.reshape(tc, n_sets, ns * gs, gs), gs, gs)
    t_rows = [jnp.concatenate([jnp.zeros_like(k_blk[:j]), k_blk[:tc - j]], axis=0)
              for j in range(tc)]
    t_set = jnp.stack(t_rows).transpose(2, 0, 3, 1, 4).reshape(n_sets, tc * LANES, tc * LANES)

    def e_half(x):
        x = x[::-1].reshape(tc, n_sets, ns, p, gs).transpose(0, 1, 2, 4, 3)
        x = diag_expand(x.reshape(tc, n_sets, ns * gs, p), gs, p)
        return x.transpose(1, 0, 2, 3).reshape(n_sets, tc * LANES, ns * p)
    e_set = jnp.concatenate([e_half(w[0]), e_half(w[1])], axis=2)

    m = _cmul((c_c[0].transpose(0, 2, 1)[None], c_c[1].transpose(0, 2, 1)[None]),
              (pw[0][1:, :, :, None], pw[1][1:, :, :, None]))

    def m_half(x):
        x = diag_expand(x.reshape(tc, n_sets, ns * p, gs), p, gs)
        return x.transpose(1, 2, 0, 3).reshape(n_sets, ns * p, tc * LANES)
    m_set = jnp.concatenate([m_half(m[0]), m_half(-m[1])], axis=1)

    a_c = (pw[0][tc], pw[1][tc])
    a_pows = [a_c]
    for _ in range(SUBLANES - 1):
        a_pows.append(_cmul(a_pows[-1], a_c))
    a_set = jnp.concatenate([jnp.stack([q[0] for q in a_pows]), jnp.stack([q[1] for q in a_pows])])
    a_set = a_set.reshape(2 * SUBLANES, n_sets, ns * p).transpose(1, 0, 2)
    return t_set.astype(BF16), e_set.astype(BF16), m_set.astype(BF16), a_set


def _rglru_kernel(*refs, heads, head_dim, n_in):
    xl_refs, yl_refs = refs[:n_in], refs[n_in:2 * n_in]
    (cw_ref, cb_ref, w_ref, ba_ref, bx_ref, nls_ref, o_ref,
     tail, a_scr, b_scr, h_scr) = refs[2 * n_in:]
    i = pl.program_id(0)
    tm, blk_w = xl_refs[0].shape
    per_blk = blk_w // head_dim
    seg = tm // SUBLANES

    def head_cols(blocks, hd):
        lo = (hd % per_blk) * head_dim
        return blocks[hd // per_blk], slice(lo, lo + head_dim)

    @pl.when(i == 0)
    def _():
        tail[...] = jnp.zeros_like(tail)
        h_scr[...] = jnp.zeros_like(h_scr)

    first = jnp.logical_and(lax.broadcasted_iota(jnp.int32, (tm, head_dim), 0) == 0, i == 0)
    row8 = lax.broadcasted_iota(jnp.int32, (SUBLANES, head_dim), 0)
    for hd in range(heads):
        cs = slice(hd * head_dim, (hd + 1) * head_dim)
        x_ref, xs_cols = head_cols(xl_refs, hd)
        x = x_ref[:, xs_cols]
        prev = tail[:, cs]
        xc = cb_ref[:, cs] + x * cw_ref[CONV_WIDTH - 1:CONV_WIDTH, cs]
        for s in range(1, CONV_WIDTH):
            sh = pltpu.roll(x, s, 0)
            head = jnp.where(row8 < s, pltpu.roll(prev, s, 0), sh[0:SUBLANES])
            sh = jnp.concatenate([head, sh[SUBLANES:]], axis=0)
            xc = xc + sh * cw_ref[CONV_WIDTH - 1 - s:CONV_WIDTH - s, cs]
        g = jnp.dot(xc.astype(BF16), w_ref[hd], preferred_element_type=F32)
        r = jax.nn.sigmoid(g[:, :head_dim] + ba_ref[:, cs])
        gi = jax.nn.sigmoid(g[:, head_dim:] + bx_ref[:, cs])
        log_a = nls_ref[:, cs] * r
        a = jnp.exp(log_a)
        mult = jnp.sqrt(-jnp.tanh(log_a) * (a * a + 1.0))
        mult = jnp.where(first, 1.0, mult)
        b = xc * gi * mult
        for sg in range(SUBLANES):
            rows = pl.ds(sg, seg, stride=SUBLANES)
            a_scr[hd, rows, :] = a[sg * seg:(sg + 1) * seg]
            b_scr[hd, rows, :] = b[sg * seg:(sg + 1) * seg]
        tail[:, cs] = x[tm - SUBLANES:tm]

    def body(k, carry):
        r0 = pl.multiple_of(k * SUBLANES, SUBLANES)
        new = []
        for hd in range(heads):
            h_loc, a_cum = carry[2 * hd], carry[2 * hd + 1]
            a = a_scr[hd, pl.ds(r0, SUBLANES), :]
            h_loc = a * h_loc + b_scr[hd, pl.ds(r0, SUBLANES), :]
            a_cum = a * a_cum
            b_scr[hd, pl.ds(r0, SUBLANES), :] = h_loc
            a_scr[hd, pl.ds(r0, SUBLANES), :] = a_cum
            new += [h_loc, a_cum]
        return tuple(new)

    init = (jnp.zeros((SUBLANES, head_dim), F32), jnp.ones((SUBLANES, head_dim), F32)) * heads
    ends = lax.fori_loop(0, seg, body, init)

    for hd in range(heads):
        cs = slice(hd * head_dim, (hd + 1) * head_dim)
        h_end, a_end = ends[2 * hd], ends[2 * hd + 1]
        y_ref, y_cols = head_cols(yl_refs, hd)
        state = h_scr[:, cs]
        for sg in range(SUBLANES):
            rows = pl.ds(sg, seg, stride=SUBLANES)
            h = b_scr[hd, rows, :] + a_scr[hd, rows, :] * state
            y = y_ref[sg * seg:(sg + 1) * seg, y_cols]
            o_ref[sg * seg:(sg + 1) * seg, cs] = (h * jax.nn.gelu(y)).astype(o_ref.dtype)
            state = h_end[sg:sg + 1, :] + a_end[sg:sg + 1, :] * state
        h_scr[:, cs] = state


def _rglru(proj, col, conv_w, conv_b, w_cat, b_a, b_x, nls, heads, head_dim, tm=512, blk_w=512):
    seq = proj.shape[0]
    width = heads * head_dim
    assert head_dim == LANES
    n_in, c0 = width // blk_w, col // blk_w
    assert n_in * blk_w == width and c0 * blk_w == col and blk_w % head_dim == 0
    kern = functools.partial(_rglru_kernel, heads=heads, head_dim=head_dim, n_in=n_in)
    row = lambda i: (0, 0)
    col_block = lambda c: pl.BlockSpec((tm, blk_w), lambda i: (i, c))
    return pl.pallas_call(
        kern,
        grid=(seq // tm,),
        in_specs=[col_block(c0 + c) for c in range(2 * n_in)] + [
            pl.BlockSpec((CONV_WIDTH, width), row),
            pl.BlockSpec((1, width), row),
            pl.BlockSpec((heads, head_dim, 2 * head_dim), lambda i: (0, 0, 0)),
            pl.BlockSpec((1, width), row),
            pl.BlockSpec((1, width), row),
            pl.BlockSpec((1, width), row),
        ],
        out_specs=pl.BlockSpec((tm, width), lambda i: (i, 0)),
        out_shape=jax.ShapeDtypeStruct((seq, width), BF16),
        scratch_shapes=[
            pltpu.VMEM((SUBLANES, width), F32),
            pltpu.VMEM((heads, tm, head_dim), F32),
            pltpu.VMEM((heads, tm, head_dim), F32),
            pltpu.VMEM((1, width), F32),
        ],
        compiler_params=_params(1),
        name="rglru",
    )(*([proj] * (2 * n_in)), conv_w, conv_b, w_cat, b_a, b_x, nls)


def _merge_kernel(h_ref, y_ref, ol_ref, wglu_ref, bglu_ref, wgs_ref, wgl_ref, bgs_ref, bgl_ref,
                  wps_ref, wpl_ref, o_ref, os_scr):
    j = pl.program_id(1)

    @pl.when(j == 0)
    def _():
        y = y_ref[...]
        glu = jax.nn.sigmoid(
            jnp.dot(y.astype(BF16), wglu_ref[...], preferred_element_type=F32) + bglu_ref[...])
        os_scr[...] = (y * glu).astype(BF16)

    h = h_ref[...]
    g_s5 = jax.nn.sigmoid(jnp.dot(h, wgs_ref[...], preferred_element_type=F32) + bgs_ref[...])
    g_lru = jax.nn.sigmoid(jnp.dot(h, wgl_ref[...], preferred_element_type=F32) + bgl_ref[...])
    br_s5 = jnp.dot(os_scr[...], wps_ref[...], preferred_element_type=F32)
    br_lru = jnp.dot(ol_ref[...], wpl_ref[...], preferred_element_type=F32)
    o_ref[...] = (g_s5 * br_s5 + g_lru * br_lru).astype(o_ref.dtype)


def _merge(h, y_s5, out_lru, w_glu, b_glu, w_in, gate_col, b_gate, wp_s5, wp_lru,
           tm=1024, tn=512):
    seq, d = h.shape
    s5_w, lru_w = y_s5.shape[1], out_lru.shape[1]
    const = lambda i, j: (0, 0)
    c0, nd = gate_col // tn, d // tn
    return pl.pallas_call(
        _merge_kernel,
        grid=(seq // tm, nd),
        in_specs=[
            pl.BlockSpec((tm, d), lambda i, j: (i, 0)),
            pl.BlockSpec((tm, s5_w), lambda i, j: (i, 0)),
            pl.BlockSpec((tm, lru_w), lambda i, j: (i, 0)),
            pl.BlockSpec((s5_w, s5_w), const),
            pl.BlockSpec((1, s5_w), const),
            pl.BlockSpec((d, tn), lambda i, j: (0, c0 + j)),
            pl.BlockSpec((d, tn), lambda i, j: (0, c0 + nd + j)),
            pl.BlockSpec((1, tn), lambda i, j: (0, j)),
            pl.BlockSpec((1, tn), lambda i, j: (0, nd + j)),
            pl.BlockSpec((s5_w, tn), lambda i, j: (0, j)),
            pl.BlockSpec((lru_w, tn), lambda i, j: (0, j)),
        ],
        out_specs=pl.BlockSpec((tm, tn), lambda i, j: (i, j)),
        out_shape=jax.ShapeDtypeStruct((seq, d), BF16),
        scratch_shapes=[pltpu.VMEM((tm, s5_w), BF16)],
        compiler_params=_params(2),
        name="merge",
    )(h, y_s5, out_lru, w_glu, b_glu, w_in, w_in, b_gate, b_gate, wp_s5, wp_lru)


def _n_slabs(d):
    assert d % LANES == 0
    return d // LANES


def _store_slabs(ref, x):
    for s in range(ref.shape[0]):
        ref[s] = x[:, s * LANES:(s + 1) * LANES]


def _slab_row(ref, g):
    rows = ref.shape[2]
    shift = rows.bit_length() - 1
    assert rows == 1 << shift
    return ref.at[lax.shift_right_logical(g, shift), :, pl.ds(jnp.bitwise_and(g, rows - 1), 1), :]


def _slab_rows(ref, n):
    assert n <= ref.shape[2]
    return ref.at[0, :, pl.ds(0, n), :]


def _route(logits, n_experts, n_groups):
    per_group = n_experts // n_groups
    lane = lax.broadcasted_iota(jnp.int32, logits.shape, 1)
    big = jnp.int32(LANES)
    neg = jnp.float32(-jnp.inf)
    is_g = jnp.logical_and(lane >= n_experts, lane < n_experts + n_groups)
    lg = jnp.where(is_g, logits, neg)
    g_max = jnp.max(lg, axis=-1, keepdims=True)
    g_lane = jnp.min(jnp.where(lg == g_max, lane, big), axis=-1, keepdims=True)
    g_top = 1.0 / jnp.sum(jnp.where(is_g, jnp.exp(lg - g_max), 0.0), axis=-1, keepdims=True)
    g_idx = g_lane - n_experts
    in_group = jnp.logical_and(lane >= g_idx * per_group, lane < (g_idx + 1) * per_group)
    le = jnp.where(in_group, logits, neg)
    m1 = jnp.max(le, axis=-1, keepdims=True)
    i1 = jnp.min(jnp.where(le == m1, lane, big), axis=-1, keepdims=True)
    le2 = jnp.where(lane == i1, neg, le)
    m2 = jnp.max(le2, axis=-1, keepdims=True)
    i2 = jnp.min(jnp.where(le2 == m2, lane, big), axis=-1, keepdims=True)
    r = jnp.exp(m2 - m1)
    w1 = g_top / (1.0 + r)
    w2 = g_top * r / (1.0 + r)
    return i1, i2, w1, w2


def _out_route_kernel(m_ref, w_ref, x_ref, nw_ref, wr_ref, br_ref, x2_ref, ht_ref, route_ref,
                      route_t_ref, cnt_ref, cnt_scr, *, n_experts, n_groups):
    i = pl.program_id(0)
    tm = x_ref.shape[0]

    @pl.when(i == 0)
    def _():
        cnt_scr[...] = jnp.zeros_like(cnt_scr)

    x2 = x_ref[...] + jnp.dot(m_ref[...], w_ref[...], preferred_element_type=F32)
    x2_ref[...] = x2
    ht = _rms(x2, nw_ref[...])
    _store_slabs(ht_ref.at[0], ht)
    ht_hi = ht.astype(BF16)
    ht_lo = (ht - ht_hi.astype(F32)).astype(BF16)
    both = jnp.dot(ht_hi, wr_ref[...], preferred_element_type=F32)
    logits = (both[:, :LANES] + both[:, LANES:]
              + jnp.dot(ht_lo, wr_ref[:, :LANES], preferred_element_type=F32) + br_ref[...])
    i1, i2, w1, w2 = _route(logits, n_experts, n_groups)

    lane = lax.broadcasted_iota(jnp.int32, (tm, LANES), 1)
    oh1 = (lane == i1).astype(F32)
    oh2 = (lane == i2).astype(F32)
    oh = oh1 + oh2
    earlier = (lax.broadcasted_iota(jnp.int32, (tm, tm), 0)
               > lax.broadcasted_iota(jnp.int32, (tm, tm), 1)).astype(BF16)
    before = jnp.dot(earlier, oh.astype(BF16), preferred_element_type=F32) + cnt_scr[0:1, :]
    r1 = jnp.sum(before * oh1, axis=-1, keepdims=True)
    r2 = jnp.sum(before * oh2, axis=-1, keepdims=True)
    cnt_scr[0:1, :] = cnt_scr[0:1, :] + jnp.sum(oh, axis=0, keepdims=True)
    cnt_ref[...] = cnt_scr[...]

    cols = [i1.astype(F32), i2.astype(F32), r1, r2, w1, w2]
    route = jnp.zeros((tm, LANES), F32)
    for k, v in enumerate(cols):
        route = jnp.where(lane == k, v, route)
    route_ref[...] = route
    route_t_ref[...] = route.T[0:SUBLANES, :]


def _out_route(merged, w_out, x, norm_w, w_router, b_router, n_experts, n_groups, tm):
    seq, d = x.shape
    const = lambda i: (0, 0)
    kern = functools.partial(_out_route_kernel, n_experts=n_experts, n_groups=n_groups)
    return pl.pallas_call(
        kern,
        grid=(seq // tm,),
        in_specs=[
            pl.BlockSpec((tm, d), lambda i: (i, 0)),
            pl.BlockSpec((d, d), const),
            pl.BlockSpec((tm, d), lambda i: (i, 0)),
            pl.BlockSpec((1, d), const),
            pl.BlockSpec((d, 2 * LANES), const),
            pl.BlockSpec((1, LANES), const),
        ],
        out_specs=[
            pl.BlockSpec((tm, d), lambda i: (i, 0)),
            pl.BlockSpec((1, _n_slabs(d), tm, LANES), lambda i: (i, 0, 0, 0)),
            pl.BlockSpec((tm, LANES), lambda i: (i, 0)),
            pl.BlockSpec((SUBLANES, tm), lambda i: (0, i)),
            pl.BlockSpec((SUBLANES, LANES), const),
        ],
        out_shape=[
            jax.ShapeDtypeStruct((seq, d), F32),
            jax.ShapeDtypeStruct((seq // tm, _n_slabs(d), tm, LANES), F32),
            jax.ShapeDtypeStruct((seq, LANES), F32),
            jax.ShapeDtypeStruct((SUBLANES, seq), F32),
            jax.ShapeDtypeStruct((SUBLANES, LANES), F32),
        ],
        scratch_shapes=[pltpu.VMEM((SUBLANES, LANES), F32)],
        compiler_params=_params(1),
        name="out_route",
    )(merged, w_out, x, norm_w, w_router, b_router)


DMA_UNROLL = 8


def _dispatch_kernel(npad_ref, padstart_ref, nused_ref, idx_ref, ht_ref, xs_hbm, stage, zero_scr,
                     sem, pad_sem):
    i = pl.program_id(0)
    nt = pl.num_programs(0)
    n_experts = npad_ref.shape[0]
    nb, _, bm, _ = xs_hbm.shape
    tm = stage.shape[2]
    slot = i % 2
    assert tm <= bm

    def wait_rows(dsem):
        for _ in range(2):
            pltpu.make_async_copy(stage.at[0], _slab_rows(xs_hbm, tm), dsem).wait()

    stage[slot] = ht_ref[0]

    def body(g, carry):
        for k in range(DMA_UNROLL):
            r = g * DMA_UNROLL + k
            src = stage.at[slot, :, pl.ds(r, 1), :]
            for half in range(2):
                dst = _slab_row(xs_hbm, idx_ref[0, 0, half * tm + r])
                pltpu.make_async_copy(src, dst, sem.at[slot]).start(priority=half)
        return carry
    lax.fori_loop(0, tm // DMA_UNROLL, body, 0)

    @pl.when(i > 0)
    def _():
        wait_rows(sem.at[1 - slot])

    @pl.when(i == nt - 1)
    def _():
        wait_rows(sem.at[slot])
        zero_scr[...] = jnp.zeros_like(zero_scr)
        zero_row = zero_scr.at[:, pl.ds(0, 1), :]
        shift = bm.bit_length() - 1
        for e in range(n_experts):
            p0 = padstart_ref[e]
            n_pad = npad_ref[e]
            blk = lax.shift_right_logical(p0, shift)
            row0 = jnp.bitwise_and(p0, bm - 1)
            n_single = jnp.minimum(jnp.bitwise_and(-row0, SUBLANES - 1), n_pad)
            q0 = row0 + n_single

            def single_start(j, carry):
                pltpu.make_async_copy(zero_row, _slab_row(xs_hbm, p0 + j), pad_sem).start()
                return carry

            def single_wait(j, carry):
                pltpu.make_async_copy(zero_row, _slab_row(xs_hbm, 0), pad_sem).wait()
                return carry

            def chunks(start):
                q = q0
                size = SUBLANES
                while size < bm:
                    take = jnp.logical_and(jnp.bitwise_and(q, size) != 0, n_pad > n_single)
                    src = zero_scr.at[:, pl.ds(0, size), :]

                    @pl.when(take)
                    def _():
                        if start:
                            dst = xs_hbm.at[blk, :, pl.ds(pl.multiple_of(q, SUBLANES), size), :]
                            pltpu.make_async_copy(src, dst, pad_sem).start()
                        else:
                            pltpu.make_async_copy(src, xs_hbm.at[0, :, pl.ds(0, size), :],
                                                  pad_sem).wait()
                    q = q + jnp.where(take, size, 0)
                    size *= 2
            lax.fori_loop(0, n_single, single_start, 0)
            chunks(True)
            lax.fori_loop(0, n_single, single_wait, 0)
            chunks(False)

        def blk_start(b, carry):
            pltpu.make_async_copy(zero_scr, xs_hbm.at[b], pad_sem).start()
            return carry

        def blk_wait(b, carry):
            pltpu.make_async_copy(zero_scr, xs_hbm.at[0], pad_sem).wait()
            return carry
        lax.fori_loop(nused_ref[0], nb, blk_start, 0)
        lax.fori_loop(nused_ref[0], nb, blk_wait, 0)


def _dispatch(npad, padstart, n_used, pos, ht_slabs, nb, bm, tm):
    _, ns, tile_rows, _ = ht_slabs.shape
    seq = pos.shape[0] * tm
    per_tile = tile_rows // tm
    assert per_tile * tm == tile_rows
    grid_spec = pltpu.PrefetchScalarGridSpec(
        num_scalar_prefetch=3,
        grid=(seq // tm,),
        in_specs=[
            pl.BlockSpec((1, 1, 2 * tm), lambda i, a, b, c: (i, 0, 0), memory_space=pltpu.SMEM),
            pl.BlockSpec((1, ns, tm, LANES),
                         lambda i, a, b, c: (i // per_tile, 0, i % per_tile, 0)),
        ],
        out_specs=pl.BlockSpec(memory_space=pl.ANY),
        scratch_shapes=[
            pltpu.VMEM((2, ns, tm, LANES), F32),
            pltpu.VMEM((ns, bm, LANES), F32),
            pltpu.SemaphoreType.DMA((2,)),
            pltpu.SemaphoreType.DMA(()),
        ],
    )
    return pl.pallas_call(
        _dispatch_kernel,
        grid_spec=grid_spec,
        out_shape=jax.ShapeDtypeStruct((nb, ns, bm, LANES), F32),
        compiler_params=pltpu.CompilerParams(dimension_semantics=("arbitrary",),
                                             has_side_effects=True),
        name="dispatch",
    )(npad, padstart, n_used, pos, ht_slabs)


def _experts_kernel(be_ref, nused_ref, par_ref, nxt_ref, xs_ref, wg_hbm, wu_hbm, wd_hbm, y_ref,
                    xs_scr, wgu_scr, wd_scr, wg_buf, wu_buf, wd_buf, sem):
    b = pl.program_id(0)
    ff = wd_scr.shape[0]

    def weight_copies(e, s):
        return [pltpu.make_async_copy(src.at[e], dst.at[s], sem.at[s])
                for src, dst in ((wg_hbm, wg_buf), (wu_hbm, wu_buf), (wd_hbm, wd_buf))]

    @pl.when(b < nused_ref[0])
    def _():
        @pl.when(jnp.logical_or(b == 0, be_ref[b] != be_ref[jnp.maximum(b - 1, 0)]))
        def _():
            s = par_ref[b]

            @pl.when(b == 0)
            def _():
                for c in weight_copies(be_ref[0], 0):
                    c.start()

            @pl.when(nxt_ref[b] >= 0)
            def _():
                for c in weight_copies(nxt_ref[b], 1 - s):
                    c.start()

            for c in weight_copies(be_ref[b], s):
                c.wait()
            wgu_scr[:, :ff] = wg_buf[s].astype(BF16)
            wgu_scr[:, ff:] = wu_buf[s].astype(BF16)
            wd_scr[...] = wd_buf[s].astype(BF16)

        for s in range(xs_ref.shape[1]):
            xs_scr[:, s * LANES:(s + 1) * LANES] = xs_ref[0, s].astype(BF16)
        hid = jnp.dot(xs_scr[...], wgu_scr[...], preferred_element_type=F32)
        act = (jax.nn.silu(hid[:, :ff]) * hid[:, ff:]).astype(BF16)
        y = jnp.dot(act, wd_scr[...], preferred_element_type=F32)
        _store_slabs(y_ref.at[0], y)

    @pl.when(b >= nused_ref[0])
    def _():
        y_ref[...] = jnp.zeros_like(y_ref)


def _experts(block_expert, n_used, xs_slabs, w_gate, w_up, w_down):
    nb, ns, bm, _ = xs_slabs.shape
    n_experts, d, ff = w_gate.shape
    blocks = jnp.arange(nb, dtype=jnp.int32)
    used = blocks < n_used[0]
    change = jnp.concatenate([jnp.ones((1,), jnp.bool_), block_expert[1:] != block_expert[:-1]])
    parity = ((jnp.cumsum(change.astype(jnp.int32)) - 1) % 2).astype(jnp.int32)
    later = jnp.where(jnp.logical_and(used[None, :], block_expert[None, :] > block_expert[:, None]),
                      block_expert[None, :], n_experts)
    nxt = jnp.min(later, axis=1)
    nxt = jnp.where(nxt < n_experts, nxt, -1).astype(jnp.int32)
    anywhere = pl.BlockSpec(memory_space=pl.ANY)
    grid_spec = pltpu.PrefetchScalarGridSpec(
        num_scalar_prefetch=4,
        grid=(nb,),
        in_specs=[
            pl.BlockSpec((1, ns, bm, LANES), lambda b, *_: (b, 0, 0, 0)),
            anywhere, anywhere, anywhere,
        ],
        out_specs=pl.BlockSpec((1, ns, bm, LANES), lambda b, *_: (b, 0, 0, 0)),
        scratch_shapes=[
            pltpu.VMEM((bm, d), BF16),
            pltpu.VMEM((d, 2 * ff), BF16),
            pltpu.VMEM((ff, d), BF16),
            pltpu.VMEM((2, d, ff), F32),
            pltpu.VMEM((2, d, ff), F32),
            pltpu.VMEM((2, ff, d), F32),
            pltpu.SemaphoreType.DMA((2,)),
        ],
    )
    return pl.pallas_call(
        _experts_kernel,
        grid_spec=grid_spec,
        out_shape=jax.ShapeDtypeStruct((nb, ns, bm, LANES), F32),
        compiler_params=_params(1),
        name="experts",
    )(block_expert, n_used, parity, nxt, xs_slabs, w_gate, w_up, w_down)


def _combine_kernel(idx_ref, idx_next_ref, y_hbm, x_ref, route_ref, fw_ref, o_ref, ybuf, sem,
                    *, final_norm):
    i = pl.program_id(0)
    nt = pl.num_programs(0)
    tm = x_ref.shape[0]
    slot = i % 2

    def start_gather(ids, buf, dsem):
        def body(g, carry):
            for k in range(DMA_UNROLL):
                r = g * DMA_UNROLL + k
                pltpu.make_async_copy(_slab_row(y_hbm, ids[0, 0, r]),
                                      buf.at[:, pl.ds(r, 1), :], dsem).start(priority=k % 2)
            return carry
        lax.fori_loop(0, 2 * tm // DMA_UNROLL, body, 0)

    @pl.when(i == 0)
    def _():
        start_gather(idx_ref, ybuf.at[0], sem.at[0])

    @pl.when(i + 1 < nt)
    def _():
        start_gather(idx_next_ref, ybuf.at[1 - slot], sem.at[1 - slot])

    buf = ybuf.at[slot]
    for half in range(2):
        pltpu.make_async_copy(_slab_rows(y_hbm, tm), buf.at[:, pl.ds(half * tm, tm), :],
                              sem.at[slot]).wait()
    w1 = route_ref[:, 4:5]
    w2 = route_ref[:, 5:6]
    ssq = jnp.zeros((tm, 1), F32)
    for s in range(buf.shape[0]):
        cols = slice(s * LANES, (s + 1) * LANES)
        v = x_ref[:, cols] + w1 * buf[s, 0:tm, :] + w2 * buf[s, tm:2 * tm, :]
        o_ref[:, cols] = v
        ssq = ssq + jnp.sum(v * v, axis=-1, keepdims=True)
    if final_norm:
        scale = lax.rsqrt(ssq / o_ref.shape[1] + NORM_EPS)
        o_ref[...] = o_ref[...] * scale * fw_ref[...]


def _combine(pos, y_rows, x2, route, final_w, final_norm, tm):
    seq, d = x2.shape
    nt = seq // tm
    kern = functools.partial(_combine_kernel, final_norm=final_norm)
    return pl.pallas_call(
        kern,
        grid=(nt,),
        in_specs=[
            pl.BlockSpec((1, 1, 2 * tm), lambda i: (i, 0, 0), memory_space=pltpu.SMEM),
            pl.BlockSpec((1, 1, 2 * tm), lambda i: (jnp.minimum(i + 1, nt - 1), 0, 0),
                         memory_space=pltpu.SMEM),
            pl.BlockSpec(memory_space=pl.ANY),
            pl.BlockSpec((tm, d), lambda i: (i, 0)),
            pl.BlockSpec((tm, LANES), lambda i: (i, 0)),
            pl.BlockSpec((1, d), lambda i: (0, 0)),
        ],
        out_specs=pl.BlockSpec((tm, d), lambda i: (i, 0)),
        out_shape=jax.ShapeDtypeStruct((seq, d), F32),
        scratch_shapes=[
            pltpu.VMEM((2, _n_slabs(d), 2 * tm, LANES), F32),
            pltpu.SemaphoreType.DMA((2,)),
        ],
        compiler_params=_params(1),
        name="combine",
    )(pos, pos, y_rows, x2, route, final_w)


def _dispatch_plan(route_t, counts, n_experts, bm):
    seq = route_t.shape[1]
    nb = (2 * seq) // bm + n_experts
    e1, e2, r1, r2 = (route_t[k].astype(jnp.int32) for k in range(4))
    cnt = counts[0, :n_experts].astype(jnp.int32)
    cnt_pad = ((cnt + bm - 1) // bm) * bm
    ends = jnp.cumsum(cnt_pad)
    starts = ends - cnt_pad
    experts = jnp.arange(n_experts, dtype=jnp.int32)

    def start_of(e):
        return jnp.sum(jnp.where(e[:, None] == experts[None, :], starts[None, :], 0), axis=1)
    pos1 = start_of(e1) + r1
    pos2 = start_of(e2) + r2
    blk_start = jnp.arange(nb, dtype=jnp.int32) * bm
    n_before = jnp.sum((ends[None, :] <= blk_start[:, None]).astype(jnp.int32), axis=1)
    block_expert = jnp.minimum(n_before, n_experts - 1)
    n_used = (ends[-1:] // bm).astype(jnp.int32)
    return pos1, pos2, cnt_pad - cnt, starts + cnt, block_expert, n_used, nb


def kernel(x, norm_mix_w, w_in, b_gate, s5_lam_re, s5_lam_im, s5_log_step, s5_b_re, s5_b_im, s5_c_re, s5_c_im, s5_d, s5_w_glu, s5_b_glu, lru_conv_w, lru_conv_b, lru_w_a, lru_b_a, lru_w_x, lru_b_x, lru_lambda, w_proj_s5, w_proj_lru, w_out, norm_ffn_w, w_router_group, b_router_group, w_router_expert, b_router_expert, w_e_gate, w_e_up, w_e_down, norm_final_w):
    depth = w_in.shape[0]
    bsz, seq, d = x.shape
    s5_w = s5_w_glu.shape[-1]
    lru_w = lru_conv_b.shape[-1]
    heads, head_dim = lru_w_a.shape[1], lru_w_a.shape[2]
    n_groups = w_router_group.shape[-1]
    n_experts = w_router_expert.shape[-1]
    o3 = s5_w + 2 * lru_w

    outs = []
    for b in range(bsz):
        xb = x[b]
        for l in range(depth):
            row = lambda v: v.astype(F32).reshape(1, -1)
            w_in_l = w_in[l].astype(BF16)
            h, proj = _in_proj(xb, row(norm_mix_w[l]), w_in_l, o3)

            t_set, e_set, m_set, a_set = _s5_matrices(
                s5_lam_re[l], s5_lam_im[l], s5_log_step[l], s5_b_re[l], s5_b_im[l],
                s5_c_re[l], s5_c_im[l])
            y_s5 = _s5(proj, 0, t_set, e_set, m_set, a_set, row(s5_d[l]))

            w_cat = jnp.concatenate([lru_w_a[l], lru_w_x[l]], axis=-1).astype(BF16)
            nls = -LRU_C * jax.nn.softplus(-lru_lambda[l].astype(F32))
            out_lru = _rglru(proj, s5_w, lru_conv_w[l].astype(F32), row(lru_conv_b[l]), w_cat,
                             row(lru_b_a[l]), row(lru_b_x[l]), row(nls), heads, head_dim)

            merged = _merge(h, y_s5, out_lru, s5_w_glu[l].astype(BF16), row(s5_b_glu[l]),
                            w_in_l, o3, row(b_gate[l]),
                            w_proj_s5[l].astype(BF16), w_proj_lru[l].astype(BF16))
            pad = LANES - n_experts - n_groups
            w_router = jnp.concatenate([w_router_expert[l].astype(F32), w_router_group[l].astype(F32),
                                        jnp.zeros((d, pad), F32)], axis=1)
            w_router_hi = w_router.astype(BF16)
            w_router_lo = (w_router - w_router_hi.astype(F32)).astype(BF16)
            b_router = jnp.concatenate([b_router_expert[l].astype(F32), b_router_group[l].astype(F32),
                                        jnp.zeros((pad,), F32)]).reshape(1, LANES)
            x2, ht_slabs, route, route_t, counts = _out_route(
                merged, w_out[l].astype(BF16), xb, row(norm_ffn_w[l]),
                jnp.concatenate([w_router_hi, w_router_lo], axis=1), b_router,
                n_experts, n_groups, tm=ROUTE_TILE)

            pos1, pos2, npad, padstart, block_expert, n_used, nb = _dispatch_plan(
                route_t, counts, n_experts, MOE_BLOCK)
            tc = COMBINE_TILE
            pos = jnp.concatenate([pos1.reshape(seq // tc, 1, tc), pos2.reshape(seq // tc, 1, tc)],
                                  axis=2)
            xs_slabs = _dispatch(npad, padstart, n_used, pos, ht_slabs, nb, MOE_BLOCK, tc)
            y_rows = _experts(block_expert, n_used, xs_slabs, w_e_gate[l], w_e_up[l], w_e_down[l])
            xb = _combine(pos, y_rows, x2, route, row(norm_final_w),
                          final_norm=(l == depth - 1), tm=tc)
        outs.append(xb)
    return jnp.stack(outs)
```

```python
import functools

import jax
import jax.numpy as jnp
from jax import lax
from jax.experimental import pallas as pl
from jax.experimental.pallas import tpu as pltpu

F32 = jnp.float32
BF16 = jnp.bfloat16
HIGHEST = lax.Precision.HIGHEST

NORM_EPS = 1e-6
LRU_C = 8.0
S5_GROUP = 16
CONV_WIDTH = 4
SUBLANES = 8
LANES = 128
S5_CHUNK = SUBLANES
S5_SET = LANES // S5_GROUP
VMEM_LIMIT = 56 * 1024 * 1024
ROUTE_TILE = 512
MOE_BLOCK = 256
DISPATCH_TILE = 512
COMBINE_TILE = 256


def _params(n_axes, vmem=VMEM_LIMIT):
    return pltpu.CompilerParams(dimension_semantics=("arbitrary",) * n_axes,
                                vmem_limit_bytes=vmem)


def _rms(x, w):
    ms = jnp.mean(x * x, axis=-1, keepdims=True)
    return x * lax.rsqrt(ms + NORM_EPS) * w


def _row_tile_ring(src_hbm, buf, sem, i, ni):
    tm = buf.shape[1]
    slot = i % 2

    def copy(tile, s):
        rows = pl.ds(pl.multiple_of(tile * tm, tm), tm)
        return pltpu.make_async_copy(src_hbm.at[rows, :], buf.at[s], sem.at[s])

    @pl.when(i == 0)
    def _():
        copy(0, 0).start()

    @pl.when(i + 1 < ni)
    def _():
        copy(i + 1, 1 - slot).start()

    copy(i, slot).wait()


def _in_proj_kernel(x_hbm, nw_ref, w_ref, h_ref, p_ref, xbuf, sem):
    i = pl.program_id(0)

    @pl.when(pl.program_id(1) == 0)
    def _():
        _row_tile_ring(x_hbm, xbuf, sem, i, pl.num_programs(0))
        h_ref[...] = _rms(xbuf[i % 2], nw_ref[...]).astype(BF16)

    p_ref[...] = jnp.dot(h_ref[...], w_ref[...], preferred_element_type=F32)


def _in_proj(x, norm_w, w, n_cols, tm=1024, tn=512):
    seq, d = x.shape
    return pl.pallas_call(
        _in_proj_kernel,
        grid=(seq // tm, n_cols // tn),
        in_specs=[
            pl.BlockSpec(memory_space=pl.ANY),
            pl.BlockSpec((1, d), lambda i, j: (0, 0)),
            pl.BlockSpec((d, tn), lambda i, j: (0, j)),
        ],
        out_specs=[
            pl.BlockSpec((tm, d), lambda i, j: (i, 0)),
            pl.BlockSpec((tm, tn), lambda i, j: (i, j)),
        ],
        out_shape=[
            jax.ShapeDtypeStruct((seq, d), BF16),
            jax.ShapeDtypeStruct((seq, n_cols), F32),
        ],
        scratch_shapes=[pltpu.VMEM((2, tm, d), F32), pltpu.SemaphoreType.DMA((2,))],
        compiler_params=_params(2),
        name="in_proj",
    )(x, norm_w, w)


def _s5_kernel(u_ref, t_ref, e_ref, m_ref, a_ref, d_ref, y_ref, x_scr, e_scr, s_scr, c_scr):
    tt = pl.program_id(1)
    n_chunks = x_scr.shape[0]
    half = e_scr.shape[1] // 2
    n_col = half // LANES

    @pl.when(tt == 0)
    def _():
        c_scr[...] = jnp.zeros_like(c_scr)

    for j in range(S5_CHUNK):
        x_scr[:, j * LANES:(j + 1) * LANES] = (
            u_ref[pl.ds(j, n_chunks, stride=S5_CHUNK), :].astype(BF16))

    e_scr[...] = jnp.dot(x_scr[...], e_ref[0], preferred_element_type=F32)

    row = lax.broadcasted_iota(jnp.int32, (SUBLANES, LANES), 0)

    def body(t, carry):
        r0 = pl.multiple_of(t * SUBLANES, SUBLANES)
        new = []
        for q in range(n_col):
            c_re, c_im = carry[2 * q], carry[2 * q + 1]
            re_cols = slice(q * LANES, (q + 1) * LANES)
            im_cols = slice(half + q * LANES, half + (q + 1) * LANES)
            x_re = e_scr[pl.ds(r0, SUBLANES), re_cols]
            x_im = e_scr[pl.ds(r0, SUBLANES), im_cols]
            for s in (1, 2, 4):
                p_re = a_ref[0, s - 1:s, re_cols]
                p_im = a_ref[0, SUBLANES + s - 1:SUBLANES + s, re_cols]
                keep = row >= s
                s_re = jnp.where(keep, pltpu.roll(x_re, s, 0), 0.0)
                s_im = jnp.where(keep, pltpu.roll(x_im, s, 0), 0.0)
                x_re, x_im = (x_re + p_re * s_re - p_im * s_im,
                              x_im + p_re * s_im + p_im * s_re)
            t_re = a_ref[0, 0:SUBLANES, re_cols]
            t_im = a_ref[0, SUBLANES:2 * SUBLANES, re_cols]
            x_re, x_im = (x_re + t_re * c_re - t_im * c_im,
                          x_im + t_re * c_im + t_im * c_re)
            s_scr[pl.ds(r0, SUBLANES), re_cols] = jnp.where(row >= 1, pltpu.roll(x_re, 1, 0), c_re)
            s_scr[pl.ds(r0, SUBLANES), im_cols] = jnp.where(row >= 1, pltpu.roll(x_im, 1, 0), c_im)
            new += [x_re[SUBLANES - 1:SUBLANES, :], x_im[SUBLANES - 1:SUBLANES, :]]
        return tuple(new)

    init = []
    for q in range(n_col):
        init += [c_scr[0:1, q * LANES:(q + 1) * LANES],
                 c_scr[0:1, half + q * LANES:half + (q + 1) * LANES]]
    last = lax.fori_loop(0, n_chunks // SUBLANES, body, tuple(init))
    for q in range(n_col):
        c_scr[0:1, q * LANES:(q + 1) * LANES] = last[2 * q]
        c_scr[0:1, half + q * LANES:half + (q + 1) * LANES] = last[2 * q + 1]

    y = jnp.dot(x_scr[...], t_ref[0], preferred_element_type=F32)
    y = y + jnp.dot(s_scr[...].astype(BF16), m_ref[0], preferred_element_type=F32)
    for j in range(S5_CHUNK):
        rows = pl.ds(j, n_chunks, stride=S5_CHUNK)
        yj = y[:, j * LANES:(j + 1) * LANES] + d_ref[...] * u_ref[rows, :]
        y_ref[rows, :] = jax.nn.gelu(yj)


def _s5(proj, col, t_set, e_set, m_set, a_set, d_skip, tile=4096):
    seq = proj.shape[0]
    width = d_skip.shape[1]
    n_sets = width // LANES
    n_chunks = tile // S5_CHUNK
    kdim = S5_CHUNK * LANES
    sdim = e_set.shape[2]
    c0 = col // LANES
    assert c0 * LANES == col
    return pl.pallas_call(
        _s5_kernel,
        grid=(n_sets, seq // tile),
        in_specs=[
            pl.BlockSpec((tile, LANES), lambda s, t: (t, c0 + s)),
            pl.BlockSpec((1, kdim, kdim), lambda s, t: (s, 0, 0)),
            pl.BlockSpec((1, kdim, sdim), lambda s, t: (s, 0, 0)),
            pl.BlockSpec((1, sdim, kdim), lambda s, t: (s, 0, 0)),
            pl.BlockSpec((1, 2 * SUBLANES, sdim // 2), lambda s, t: (s, 0, 0)),
            pl.BlockSpec((1, LANES), lambda s, t: (0, s)),
        ],
        out_specs=pl.BlockSpec((tile, LANES), lambda s, t: (t, s)),
        out_shape=jax.ShapeDtypeStruct((seq, width), F32),
        scratch_shapes=[
            pltpu.VMEM((n_chunks, kdim), BF16),
            pltpu.VMEM((n_chunks, sdim), F32),
            pltpu.VMEM((n_chunks, sdim), F32),
            pltpu.VMEM((SUBLANES, sdim), F32),
        ],
        compiler_params=_params(2),
        name="s5",
    )(proj, t_set, e_set, m_set, a_set, d_skip)


def _cmul(a, b):
    return a[0] * b[0] - a[1] * b[1], a[0] * b[1] + a[1] * b[0]


def _s5_matrices(lam_re, lam_im, log_step, b_re, b_im, c_re, c_im):
    g, p = lam_re.shape
    gs, tc, ns = S5_GROUP, S5_CHUNK, S5_SET
    n_sets = g // ns
    lam = (lam_re.astype(F32), lam_im.astype(F32))
    step = jnp.exp(log_step.astype(F32))[:, None]
    mag = jnp.exp(lam[0] * step)
    lam_bar = (mag * jnp.cos(lam[1] * step), mag * jnp.sin(lam[1] * step))
    den = lam[0] * lam[0] + lam[1] * lam[1]
    coef = _cmul((lam_bar[0] - 1.0, lam_bar[1]), (lam[0] / den, -lam[1] / den))
    b_c = (b_re.astype(F32), b_im.astype(F32))
    b_bar = _cmul((coef[0][..., None], coef[1][..., None]), b_c)
    c_c = (c_re.astype(F32), c_im.astype(F32))

    pows = [(jnp.ones_like(lam_bar[0]), jnp.zeros_like(lam_bar[0]))]
    for _ in range(tc):
        pows.append(_cmul(pows[-1], lam_bar))
    pw = (jnp.stack([q[0] for q in pows]), jnp.stack([q[1] for q in pows]))

    w = _cmul((pw[0][:tc, :, :, None], pw[1][:tc, :, :, None]), (b_bar[0][None], b_bar[1][None]))
    kmat = (jnp.einsum('gop,tgpc->tgoc', c_c[0], w[0], precision=HIGHEST)
            - jnp.einsum('gop,tgpc->tgoc', c_c[1], w[1], precision=HIGHEST))

    def diag_expand(a, row_group, col_group):
        tiled = jnp.tile(a, (1,) * (a.ndim - 1) + (ns,))
        r = jnp.arange(tiled.shape[-2])[:, None] // row_group
        c = jnp.arange(tiled.shape[-1])[None, :] // col_group
        return jnp.where(r == c, tiled, 0.0)

    k_blk = kmat.reshape(tc, n_sets, ns, gs, gs).transpose(0, 1, 2, 4, 3)
    k_blk = diag_expand(k_blk.reshape(tc, n_sets, ns * gs, gs), gs, gs)
    t_rows = [jnp.concatenate([jnp.zeros_like(k_blk[:j]), k_blk[:tc - j]], axis=0)
              for j in range(tc)]
    t_set = jnp.stack(t_rows).transpose(2, 0, 3, 1, 4).reshape(n_sets, tc * LANES, tc * LANES)

    def e_half(x):
        x = x[::-1].reshape(tc, n_sets, ns, p, gs).transpose(0, 1, 2, 4, 3)
        x = diag_expand(x.reshape(tc, n_sets, ns * gs, p), gs, p)
        return x.transpose(1, 0, 2, 3).reshape(n_sets, tc * LANES, ns * p)
    e_set = jnp.concatenate([e_half(w[0]), e_half(w[1])], axis=2)

    m = _cmul((c_c[0].transpose(0, 2, 1)[None], c_c[1].transpose(0, 2, 1)[None]),
              (pw[0][1:, :, :, None], pw[1][1:, :, :, None]))

    def m_half(x):
        x = diag_expand(x.reshape(tc, n_sets, ns * p, gs), p, gs)
        return x.transpose(1, 2, 0, 3).reshape(n_sets, ns * p, tc * LANES)
    m_set = jnp.concatenate([m_half(m[0]), m_half(-m[1])], axis=1)

    a_c = (pw[0][tc], pw[1][tc])
    a_pows = [a_c]
    for _ in range(SUBLANES - 1):
        a_pows.append(_cmul(a_pows[-1], a_c))
    a_set = jnp.concatenate([jnp.stack([q[0] for q in a_pows]), jnp.stack([q[1] for q in a_pows])])
    a_set = a_set.reshape(2 * SUBLANES, n_sets, ns * p).transpose(1, 0, 2)
    return t_set.astype(BF16), e_set.astype(BF16), m_set.astype(BF16), a_set


def _rglru_kernel(*refs, heads, head_dim, n_in):
    xl_refs, yl_refs = refs[:n_in], refs[n_in:2 * n_in]
    (cw_ref, cb_ref, w_ref, ba_ref, bx_ref, nls_ref, o_ref,
     tail, a_scr, b_scr, h_scr) = refs[2 * n_in:]
    i = pl.program_id(0)
    tm, blk_w = xl_refs[0].shape
    per_blk = blk_w // head_dim
    seg = tm // SUBLANES

    def head_cols(blocks, hd):
        lo = (hd % per_blk) * head_dim
        return blocks[hd // per_blk], slice(lo, lo + head_dim)

    @pl.when(i == 0)
    def _():
        tail[...] = jnp.zeros_like(tail)
        h_scr[...] = jnp.zeros_like(h_scr)

    first = jnp.logical_and(lax.broadcasted_iota(jnp.int32, (tm, head_dim), 0) == 0, i == 0)
    row8 = lax.broadcasted_iota(jnp.int32, (SUBLANES, head_dim), 0)
    for hd in range(heads):
        cs = slice(hd * head_dim, (hd + 1) * head_dim)
        x_ref, xs_cols = head_cols(xl_refs, hd)
        x = x_ref[:, xs_cols]
        prev = tail[:, cs]
        xc = cb_ref[:, cs] + x * cw_ref[CONV_WIDTH - 1:CONV_WIDTH, cs]
        for s in range(1, CONV_WIDTH):
            sh = pltpu.roll(x, s, 0)
            head = jnp.where(row8 < s, pltpu.roll(prev, s, 0), sh[0:SUBLANES])
            sh = jnp.concatenate([head, sh[SUBLANES:]], axis=0)
            xc = xc + sh * cw_ref[CONV_WIDTH - 1 - s:CONV_WIDTH - s, cs]
        g = jnp.dot(xc.astype(BF16), w_ref[hd], preferred_element_type=F32)
        r = jax.nn.sigmoid(g[:, :head_dim] + ba_ref[:, cs])
        gi = jax.nn.sigmoid(g[:, head_dim:] + bx_ref[:, cs])
        log_a = nls_ref[:, cs] * r
        a = jnp.exp(log_a)
        mult = jnp.sqrt(-jnp.tanh(log_a) * (a * a + 1.0))
        mult = jnp.where(first, 1.0, mult)
        b = xc * gi * mult
        for sg in range(SUBLANES):
            rows = pl.ds(sg, seg, stride=SUBLANES)
            a_scr[hd, rows, :] = a[sg * seg:(sg + 1) * seg]
            b_scr[hd, rows, :] = b[sg * seg:(sg + 1) * seg]
        tail[:, cs] = x[tm - SUBLANES:tm]

    def body(k, carry):
        r0 = pl.multiple_of(k * SUBLANES, SUBLANES)
        new = []
        for hd in range(heads):
            h_loc, a_cum = carry[2 * hd], carry[2 * hd + 1]
            a = a_scr[hd, pl.ds(r0, SUBLANES), :]
            h_loc = a * h_loc + b_scr[hd, pl.ds(r0, SUBLANES), :]
            a_cum = a * a_cum
            b_scr[hd, pl.ds(r0, SUBLANES), :] = h_loc
            a_scr[hd, pl.ds(r0, SUBLANES), :] = a_cum
            new += [h_loc, a_cum]
        return tuple(new)

    init = (jnp.zeros((SUBLANES, head_dim), F32), jnp.ones((SUBLANES, head_dim), F32)) * heads
    ends = lax.fori_loop(0, seg, body, init)

    for hd in range(heads):
        cs = slice(hd * head_dim, (hd + 1) * head_dim)
        h_end, a_end = ends[2 * hd], ends[2 * hd + 1]
        y_ref, y_cols = head_cols(yl_refs, hd)
        state = h_scr[:, cs]
        for sg in range(SUBLANES):
            rows = pl.ds(sg, seg, stride=SUBLANES)
            h = b_scr[hd, rows, :] + a_scr[hd, rows, :] * state
            y = y_ref[sg * seg:(sg + 1) * seg, y_cols]
            o_ref[sg * seg:(sg + 1) * seg, cs] = (h * jax.nn.gelu(y)).astype(o_ref.dtype)
            state = h_end[sg:sg + 1, :] + a_end[sg:sg + 1, :] * state
        h_scr[:, cs] = state


def _rglru(proj, col, conv_w, conv_b, w_cat, b_a, b_x, nls, heads, head_dim, tm=512, blk_w=512):
    seq = proj.shape[0]
    width = heads * head_dim
    assert head_dim == LANES
    n_in, c0 = width // blk_w, col // blk_w
    assert n_in * blk_w == width and c0 * blk_w == col and blk_w % head_dim == 0
    kern = functools.partial(_rglru_kernel, heads=heads, head_dim=head_dim, n_in=n_in)
    row = lambda i: (0, 0)
    col_block = lambda c: pl.BlockSpec((tm, blk_w), lambda i: (i, c))
    return pl.pallas_call(
        kern,
        grid=(seq // tm,),
        in_specs=[col_block(c0 + c) for c in range(2 * n_in)] + [
            pl.BlockSpec((CONV_WIDTH, width), row),
            pl.BlockSpec((1, width), row),
            pl.BlockSpec((heads, head_dim, 2 * head_dim), lambda i: (0, 0, 0)),
            pl.BlockSpec((1, width), row),
            pl.BlockSpec((1, width), row),
            pl.BlockSpec((1, width), row),
        ],
        out_specs=pl.BlockSpec((tm, width), lambda i: (i, 0)),
        out_shape=jax.ShapeDtypeStruct((seq, width), BF16),
        scratch_shapes=[
            pltpu.VMEM((SUBLANES, width), F32),
            pltpu.VMEM((heads, tm, head_dim), F32),
            pltpu.VMEM((heads, tm, head_dim), F32),
            pltpu.VMEM((1, width), F32),
        ],
        compiler_params=_params(1),
        name="rglru",
    )(*([proj] * (2 * n_in)), conv_w, conv_b, w_cat, b_a, b_x, nls)


def _merge_kernel(h_ref, y_ref, ol_ref, wglu_ref, bglu_ref, wgs_ref, wgl_ref, bgs_ref, bgl_ref,
                  wps_ref, wpl_ref, o_ref, os_scr):
    j = pl.program_id(1)

    @pl.when(j == 0)
    def _():
        y = y_ref[...]
        glu = jax.nn.sigmoid(
            jnp.dot(y.astype(BF16), wglu_ref[...], preferred_element_type=F32) + bglu_ref[...])
        os_scr[...] = (y * glu).astype(BF16)

    h = h_ref[...]
    g_s5 = jax.nn.sigmoid(jnp.dot(h, wgs_ref[...], preferred_element_type=F32) + bgs_ref[...])
    g_lru = jax.nn.sigmoid(jnp.dot(h, wgl_ref[...], preferred_element_type=F32) + bgl_ref[...])
    br_s5 = jnp.dot(os_scr[...], wps_ref[...], preferred_element_type=F32)
    br_lru = jnp.dot(ol_ref[...], wpl_ref[...], preferred_element_type=F32)
    o_ref[...] = (g_s5 * br_s5 + g_lru * br_lru).astype(o_ref.dtype)


def _merge(h, y_s5, out_lru, w_glu, b_glu, w_in, gate_col, b_gate, wp_s5, wp_lru,
           tm=1024, tn=512):
    seq, d = h.shape
    s5_w, lru_w = y_s5.shape[1], out_lru.shape[1]
    const = lambda i, j: (0, 0)
    c0, nd = gate_col // tn, d // tn
    return pl.pallas_call(
        _merge_kernel,
        grid=(seq // tm, nd),
        in_specs=[
            pl.BlockSpec((tm, d), lambda i, j: (i, 0)),
            pl.BlockSpec((tm, s5_w), lambda i, j: (i, 0)),
            pl.BlockSpec((tm, lru_w), lambda i, j: (i, 0)),
            pl.BlockSpec((s5_w, s5_w), const),
            pl.BlockSpec((1, s5_w), const),
            pl.BlockSpec((d, tn), lambda i, j: (0, c0 + j)),
            pl.BlockSpec((d, tn), lambda i, j: (0, c0 + nd + j)),
            pl.BlockSpec((1, tn), lambda i, j: (0, j)),
            pl.BlockSpec((1, tn), lambda i, j: (0, nd + j)),
            pl.BlockSpec((s5_w, tn), lambda i, j: (0, j)),
            pl.BlockSpec((lru_w, tn), lambda i, j: (0, j)),
        ],
        out_specs=pl.BlockSpec((tm, tn), lambda i, j: (i, j)),
        out_shape=jax.ShapeDtypeStruct((seq, d), BF16),
        scratch_shapes=[pltpu.VMEM((tm, s5_w), BF16)],
        compiler_params=_params(2),
        name="merge",
    )(h, y_s5, out_lru, w_glu, b_glu, w_in, w_in, b_gate, b_gate, wp_s5, wp_lru)


def _n_slabs(d):
    assert d % LANES == 0
    return d // LANES


def _store_slabs(ref, x):
    for s in range(ref.shape[0]):
        ref[s] = x[:, s * LANES:(s + 1) * LANES]


def _slab_row(ref, g):
    rows = ref.shape[2]
    shift = rows.bit_length() - 1
    assert rows == 1 << shift
    return ref.at[lax.shift_right_logical(g, shift), :, pl.ds(jnp.bitwise_and(g, rows - 1), 1), :]


def _slab_rows(ref, n):
    assert n <= ref.shape[2]
    return ref.at[0, :, pl.ds(0, n), :]


def _route(logits, n_experts, n_groups):
    per_group = n_experts // n_groups
    lane = lax.broadcasted_iota(jnp.int32, logits.shape, 1)
    big = jnp.int32(LANES)
    neg = jnp.float32(-jnp.inf)
    is_g = jnp.logical_and(lane >= n_experts, lane < n_experts + n_groups)
    lg = jnp.where(is_g, logits, neg)
    g_max = jnp.max(lg, axis=-1, keepdims=True)
    g_lane = jnp.min(jnp.where(lg == g_max, lane, big), axis=-1, keepdims=True)
    g_top = 1.0 / jnp.sum(jnp.where(is_g, jnp.exp(lg - g_max), 0.0), axis=-1, keepdims=True)
    g_idx = g_lane - n_experts
    in_group = jnp.logical_and(lane >= g_idx * per_group, lane < (g_idx + 1) * per_group)
    le = jnp.where(in_group, logits, neg)
    m1 = jnp.max(le, axis=-1, keepdims=True)
    i1 = jnp.min(jnp.where(le == m1, lane, big), axis=-1, keepdims=True)
    le2 = jnp.where(lane == i1, neg, le)
    m2 = jnp.max(le2, axis=-1, keepdims=True)
    i2 = jnp.min(jnp.where(le2 == m2, lane, big), axis=-1, keepdims=True)
    r = jnp.exp(m2 - m1)
    w1 = g_top / (1.0 + r)
    w2 = g_top * r / (1.0 + r)
    return i1, i2, w1, w2


def _out_route_kernel(m_ref, w_ref, x_ref, nw_ref, wr_ref, br_ref, x2_ref, ht_ref, route_ref,
                      route_t_ref, cnt_ref, cnt_scr, *, n_experts, n_groups):
    i = pl.program_id(0)
    tm = x_ref.shape[0]

    @pl.when(i == 0)
    def _():
        cnt_scr[...] = jnp.zeros_like(cnt_scr)

    x2 = x_ref[...] + jnp.dot(m_ref[...], w_ref[...], preferred_element_type=F32)
    x2_ref[...] = x2
    ht = _rms(x2, nw_ref[...])
    _store_slabs(ht_ref.at[0], ht)
    ht_hi = ht.astype(BF16)
    ht_lo = (ht - ht_hi.astype(F32)).astype(BF16)
    both = jnp.dot(ht_hi, wr_ref[...], preferred_element_type=F32)
    logits = (both[:, :LANES] + both[:, LANES:]
              + jnp.dot(ht_lo, wr_ref[:, :LANES], preferred_element_type=F32) + br_ref[...])
    i1, i2, w1, w2 = _route(logits, n_experts, n_groups)

    lane = lax.broadcasted_iota(jnp.int32, (tm, LANES), 1)
    oh1 = (lane == i1).astype(F32)
    oh2 = (lane == i2).astype(F32)
    oh = oh1 + oh2
    earlier = (lax.broadcasted_iota(jnp.int32, (tm, tm), 0)
               > lax.broadcasted_iota(jnp.int32, (tm, tm), 1)).astype(BF16)
    before = jnp.dot(earlier, oh.astype(BF16), preferred_element_type=F32) + cnt_scr[0:1, :]
    r1 = jnp.sum(before * oh1, axis=-1, keepdims=True)
    r2 = jnp.sum(before * oh2, axis=-1, keepdims=True)
    cnt_scr[0:1, :] = cnt_scr[0:1, :] + jnp.sum(oh, axis=0, keepdims=True)
    cnt_ref[...] = cnt_scr[...]

    cols = [i1.astype(F32), i2.astype(F32), r1, r2, w1, w2]
    route = jnp.zeros((tm, LANES), F32)
    for k, v in enumerate(cols):
        route = jnp.where(lane == k, v, route)
    route_ref[...] = route
    route_t_ref[...] = route.T[0:SUBLANES, :]


def _out_route(merged, w_out, x, norm_w, w_router, b_router, n_experts, n_groups, tm):
    seq, d = x.shape
    const = lambda i: (0, 0)
    kern = functools.partial(_out_route_kernel, n_experts=n_experts, n_groups=n_groups)
    return pl.pallas_call(
        kern,
        grid=(seq // tm,),
        in_specs=[
            pl.BlockSpec((tm, d), lambda i: (i, 0)),
            pl.BlockSpec((d, d), const),
            pl.BlockSpec((tm, d), lambda i: (i, 0)),
            pl.BlockSpec((1, d), const),
            pl.BlockSpec((d, 2 * LANES), const),
            pl.BlockSpec((1, LANES), const),
        ],
        out_specs=[
            pl.BlockSpec((tm, d), lambda i: (i, 0)),
            pl.BlockSpec((1, _n_slabs(d), tm, LANES), lambda i: (i, 0, 0, 0)),
            pl.BlockSpec((tm, LANES), lambda i: (i, 0)),
            pl.BlockSpec((SUBLANES, tm), lambda i: (0, i)),
            pl.BlockSpec((SUBLANES, LANES), const),
        ],
        out_shape=[
            jax.ShapeDtypeStruct((seq, d), F32),
            jax.ShapeDtypeStruct((seq // tm, _n_slabs(d), tm, LANES), F32),
            jax.ShapeDtypeStruct((seq, LANES), F32),
            jax.ShapeDtypeStruct((SUBLANES, seq), F32),
            jax.ShapeDtypeStruct((SUBLANES, LANES), F32),
        ],
        scratch_shapes=[pltpu.VMEM((SUBLANES, LANES), F32)],
        compiler_params=_params(1),
        name="out_route",
    )(merged, w_out, x, norm_w, w_router, b_router)


DMA_UNROLL = 8


def _dispatch_kernel(npad_ref, padstart_ref, nused_ref, idx_ref, ht_ref, xs_hbm, stage, zero_scr,
                     sem, pad_sem):
    i = pl.program_id(0)
    nt = pl.num_programs(0)
    n_experts = npad_ref.shape[0]
    nb, _, bm, _ = xs_hbm.shape
    tm = stage.shape[2]
    slot = i % 2
    chunk = min(tm, bm)

    def wait_rows(dsem):
        for _ in range(2 * tm // chunk):
            pltpu.make_async_copy(stage.at[0, :, pl.ds(0, chunk), :], _slab_rows(xs_hbm, chunk),
                                  dsem).wait()

    stage[slot] = ht_ref[0]

    def body(g, carry):
        for k in range(DMA_UNROLL):
            r = g * DMA_UNROLL + k
            src = stage.at[slot, :, pl.ds(r, 1), :]
            for half in range(2):
                dst = _slab_row(xs_hbm, idx_ref[0, 0, half * tm + r])
                pltpu.make_async_copy(src, dst, sem.at[slot]).start(priority=half)
        return carry
    lax.fori_loop(0, tm // DMA_UNROLL, body, 0)

    @pl.when(i > 0)
    def _():
        wait_rows(sem.at[1 - slot])

    @pl.when(i == nt - 1)
    def _():
        wait_rows(sem.at[slot])
        zero_scr[...] = jnp.zeros_like(zero_scr)
        zero_row = zero_scr.at[:, pl.ds(0, 1), :]
        shift = bm.bit_length() - 1
        for e in range(n_experts):
            p0 = padstart_ref[e]
            n_pad = npad_ref[e]
            blk = lax.shift_right_logical(p0, shift)
            row0 = jnp.bitwise_and(p0, bm - 1)
            n_single = jnp.minimum(jnp.bitwise_and(-row0, SUBLANES - 1), n_pad)
            q0 = row0 + n_single

            def single_start(j, carry):
                pltpu.make_async_copy(zero_row, _slab_row(xs_hbm, p0 + j), pad_sem).start()
                return carry

            def single_wait(j, carry):
                pltpu.make_async_copy(zero_row, _slab_row(xs_hbm, 0), pad_sem).wait()
                return carry

            def chunks(start):
                q = q0
                size = SUBLANES
                while size < bm:
                    take = jnp.logical_and(jnp.bitwise_and(q, size) != 0, n_pad > n_single)
                    src = zero_scr.at[:, pl.ds(0, size), :]

                    @pl.when(take)
                    def _():
                        if start:
                            dst = xs_hbm.at[blk, :, pl.ds(pl.multiple_of(q, SUBLANES), size), :]
                            pltpu.make_async_copy(src, dst, pad_sem).start()
                        else:
                            pltpu.make_async_copy(src, xs_hbm.at[0, :, pl.ds(0, size), :],
                                                  pad_sem).wait()
                    q = q + jnp.where(take, size, 0)
                    size *= 2
            lax.fori_loop(0, n_single, single_start, 0)
            chunks(True)
            lax.fori_loop(0, n_single, single_wait, 0)
            chunks(False)

        def blk_start(b, carry):
            pltpu.make_async_copy(zero_scr, xs_hbm.at[b], pad_sem).start()
            return carry

        def blk_wait(b, carry):
            pltpu.make_async_copy(zero_scr, xs_hbm.at[0], pad_sem).wait()
            return carry
        lax.fori_loop(nused_ref[0], nb, blk_start, 0)
        lax.fori_loop(nused_ref[0], nb, blk_wait, 0)


def _dispatch(npad, padstart, n_used, pos, ht_slabs, nb, bm, tm):
    _, ns, tile_rows, _ = ht_slabs.shape
    seq = pos.shape[0] * tm
    per_tile = tile_rows // tm
    assert per_tile * tm == tile_rows
    grid_spec = pltpu.PrefetchScalarGridSpec(
        num_scalar_prefetch=3,
        grid=(seq // tm,),
        in_specs=[
            pl.BlockSpec((1, 1, 2 * tm), lambda i, a, b, c: (i, 0, 0), memory_space=pltpu.SMEM),
            pl.BlockSpec((1, ns, tm, LANES),
                         lambda i, a, b, c: (i // per_tile, 0, i % per_tile, 0)),
        ],
        out_specs=pl.BlockSpec(memory_space=pl.ANY),
        scratch_shapes=[
            pltpu.VMEM((2, ns, tm, LANES), F32),
            pltpu.VMEM((ns, bm, LANES), F32),
            pltpu.SemaphoreType.DMA((2,)),
            pltpu.SemaphoreType.DMA(()),
        ],
    )
    return pl.pallas_call(
        _dispatch_kernel,
        grid_spec=grid_spec,
        out_shape=jax.ShapeDtypeStruct((nb, ns, bm, LANES), F32),
        compiler_params=pltpu.CompilerParams(dimension_semantics=("arbitrary",),
                                             has_side_effects=True),
        name="dispatch",
    )(npad, padstart, n_used, pos, ht_slabs)


def _experts_kernel(be_ref, nused_ref, par_ref, nxt_ref, xs_ref, wg_hbm, wu_hbm, wd_hbm, y_ref,
                    xs_scr, wgu_scr, wd_scr, wg_buf, wu_buf, wd_buf, sem):
    b = pl.program_id(0)
    ff = wd_scr.shape[0]

    def weight_copies(e, s):
        return [pltpu.make_async_copy(src.at[e], dst.at[s], sem.at[s])
                for src, dst in ((wg_hbm, wg_buf), (wu_hbm, wu_buf), (wd_hbm, wd_buf))]

    @pl.when(b < nused_ref[0])
    def _():
        @pl.when(jnp.logical_or(b == 0, be_ref[b] != be_ref[jnp.maximum(b - 1, 0)]))
        def _():
            s = par_ref[b]

            @pl.when(b == 0)
            def _():
                for c in weight_copies(be_ref[0], 0):
                    c.start()

            @pl.when(nxt_ref[b] >= 0)
            def _():
                for c in weight_copies(nxt_ref[b], 1 - s):
                    c.start()

            for c in weight_copies(be_ref[b], s):
                c.wait()
            wgu_scr[:, :ff] = wg_buf[s].astype(BF16)
            wgu_scr[:, ff:] = wu_buf[s].astype(BF16)
            wd_scr[...] = wd_buf[s].astype(BF16)

        for s in range(xs_ref.shape[1]):
            xs_scr[:, s * LANES:(s + 1) * LANES] = xs_ref[0, s].astype(BF16)
        hid = jnp.dot(xs_scr[...], wgu_scr[...], preferred_element_type=F32)
        act = (jax.nn.silu(hid[:, :ff]) * hid[:, ff:]).astype(BF16)
        y = jnp.dot(act, wd_scr[...], preferred_element_type=F32)
        _store_slabs(y_ref.at[0], y)

    @pl.when(b >= nused_ref[0])
    def _():
        y_ref[...] = jnp.zeros_like(y_ref)


def _experts(block_expert, n_used, xs_slabs, w_gate, w_up, w_down):
    nb, ns, bm, _ = xs_slabs.shape
    n_experts, d, ff = w_gate.shape
    blocks = jnp.arange(nb, dtype=jnp.int32)
    used = blocks < n_used[0]
    change = jnp.concatenate([jnp.ones((1,), jnp.bool_), block_expert[1:] != block_expert[:-1]])
    parity = ((jnp.cumsum(change.astype(jnp.int32)) - 1) % 2).astype(jnp.int32)
    later = jnp.where(jnp.logical_and(used[None, :], block_expert[None, :] > block_expert[:, None]),
                      block_expert[None, :], n_experts)
    nxt = jnp.min(later, axis=1)
    nxt = jnp.where(nxt < n_experts, nxt, -1).astype(jnp.int32)
    anywhere = pl.BlockSpec(memory_space=pl.ANY)
    grid_spec = pltpu.PrefetchScalarGridSpec(
        num_scalar_prefetch=4,
        grid=(nb,),
        in_specs=[
            pl.BlockSpec((1, ns, bm, LANES), lambda b, *_: (b, 0, 0, 0)),
            anywhere, anywhere, anywhere,
        ],
        out_specs=pl.BlockSpec((1, ns, bm, LANES), lambda b, *_: (b, 0, 0, 0)),
        scratch_shapes=[
            pltpu.VMEM((bm, d), BF16),
            pltpu.VMEM((d, 2 * ff), BF16),
            pltpu.VMEM((ff, d), BF16),
            pltpu.VMEM((2, d, ff), F32),
            pltpu.VMEM((2, d, ff), F32),
            pltpu.VMEM((2, ff, d), F32),
            pltpu.SemaphoreType.DMA((2,)),
        ],
    )
    return pl.pallas_call(
        _experts_kernel,
        grid_spec=grid_spec,
        out_shape=jax.ShapeDtypeStruct((nb, ns, bm, LANES), F32),
        compiler_params=_params(1),
        name="experts",
    )(block_expert, n_used, parity, nxt, xs_slabs, w_gate, w_up, w_down)


def _combine_kernel(idx_ref, idx_next_ref, y_hbm, x_ref, route_ref, fw_ref, o_ref, ybuf, sem,
                    *, final_norm):
    i = pl.program_id(0)
    nt = pl.num_programs(0)
    tm = x_ref.shape[0]
    slot = i % 2

    def start_gather(ids, buf, dsem):
        def body(g, carry):
            for k in range(DMA_UNROLL):
                r = g * DMA_UNROLL + k
                pltpu.make_async_copy(_slab_row(y_hbm, ids[0, 0, r]),
                                      buf.at[:, pl.ds(r, 1), :], dsem).start(priority=k % 2)
            return carry
        lax.fori_loop(0, 2 * tm // DMA_UNROLL, body, 0)

    @pl.when(i == 0)
    def _():
        start_gather(idx_ref, ybuf.at[0], sem.at[0])

    @pl.when(i + 1 < nt)
    def _():
        start_gather(idx_next_ref, ybuf.at[1 - slot], sem.at[1 - slot])

    buf = ybuf.at[slot]
    chunk = min(tm, y_hbm.shape[2])
    for c in range(2 * tm // chunk):
        pltpu.make_async_copy(_slab_rows(y_hbm, chunk), buf.at[:, pl.ds(c * chunk, chunk), :],
                              sem.at[slot]).wait()
    w1 = route_ref[:, 4:5]
    w2 = route_ref[:, 5:6]
    ssq = jnp.zeros((tm, 1), F32)
    for s in range(buf.shape[0]):
        cols = slice(s * LANES, (s + 1) * LANES)
        v = x_ref[:, cols] + w1 * buf[s, 0:tm, :] + w2 * buf[s, tm:2 * tm, :]
        o_ref[:, cols] = v
        ssq = ssq + jnp.sum(v * v, axis=-1, keepdims=True)
    if final_norm:
        scale = lax.rsqrt(ssq / o_ref.shape[1] + NORM_EPS)
        o_ref[...] = o_ref[...] * scale * fw_ref[...]


def _combine(pos, y_rows, x2, route, final_w, final_norm, tm):
    seq, d = x2.shape
    nt = seq // tm
    kern = functools.partial(_combine_kernel, final_norm=final_norm)
    return pl.pallas_call(
        kern,
        grid=(nt,),
        in_specs=[
            pl.BlockSpec((1, 1, 2 * tm), lambda i: (i, 0, 0), memory_space=pltpu.SMEM),
            pl.BlockSpec((1, 1, 2 * tm), lambda i: (jnp.minimum(i + 1, nt - 1), 0, 0),
                         memory_space=pltpu.SMEM),
            pl.BlockSpec(memory_space=pl.ANY),
            pl.BlockSpec((tm, d), lambda i: (i, 0)),
            pl.BlockSpec((tm, LANES), lambda i: (i, 0)),
            pl.BlockSpec((1, d), lambda i: (0, 0)),
        ],
        out_specs=pl.BlockSpec((tm, d), lambda i: (i, 0)),
        out_shape=jax.ShapeDtypeStruct((seq, d), F32),
        scratch_shapes=[
            pltpu.VMEM((2, _n_slabs(d), 2 * tm, LANES), F32),
            pltpu.SemaphoreType.DMA((2,)),
        ],
        compiler_params=_params(1),
        name="combine",
    )(pos, pos, y_rows, x2, route, final_w)


def _dispatch_plan(route_t, counts, n_experts, bm):
    seq = route_t.shape[1]
    nb = (2 * seq) // bm + n_experts
    e1, e2, r1, r2 = (route_t[k].astype(jnp.int32) for k in range(4))
    cnt = counts[0, :n_experts].astype(jnp.int32)
    cnt_pad = ((cnt + bm - 1) // bm) * bm
    ends = jnp.cumsum(cnt_pad)
    starts = ends - cnt_pad
    experts = jnp.arange(n_experts, dtype=jnp.int32)

    def start_of(e):
        return jnp.sum(jnp.where(e[:, None] == experts[None, :], starts[None, :], 0), axis=1)
    pos1 = start_of(e1) + r1
    pos2 = start_of(e2) + r2
    blk_start = jnp.arange(nb, dtype=jnp.int32) * bm
    n_before = jnp.sum((ends[None, :] <= blk_start[:, None]).astype(jnp.int32), axis=1)
    block_expert = jnp.minimum(n_before, n_experts - 1)
    n_used = (ends[-1:] // bm).astype(jnp.int32)
    return pos1, pos2, cnt_pad - cnt, starts + cnt, block_expert, n_used, nb


def kernel(x, norm_mix_w, w_in, b_gate, s5_lam_re, s5_lam_im, s5_log_step, s5_b_re, s5_b_im, s5_c_re, s5_c_im, s5_d, s5_w_glu, s5_b_glu, lru_conv_w, lru_conv_b, lru_w_a, lru_b_a, lru_w_x, lru_b_x, lru_lambda, w_proj_s5, w_proj_lru, w_out, norm_ffn_w, w_router_group, b_router_group, w_router_expert, b_router_expert, w_e_gate, w_e_up, w_e_down, norm_final_w):
    depth = w_in.shape[0]
    bsz, seq, d = x.shape
    s5_w = s5_w_glu.shape[-1]
    lru_w = lru_conv_b.shape[-1]
    heads, head_dim = lru_w_a.shape[1], lru_w_a.shape[2]
    n_groups = w_router_group.shape[-1]
    n_experts = w_router_expert.shape[-1]
    o3 = s5_w + 2 * lru_w

    outs = []
    for b in range(bsz):
        xb = x[b]
        for l in range(depth):
            row = lambda v: v.astype(F32).reshape(1, -1)
            w_in_l = w_in[l].astype(BF16)
            h, proj = _in_proj(xb, row(norm_mix_w[l]), w_in_l, o3)

            t_set, e_set, m_set, a_set = _s5_matrices(
                s5_lam_re[l], s5_lam_im[l], s5_log_step[l], s5_b_re[l], s5_b_im[l],
                s5_c_re[l], s5_c_im[l])
            y_s5 = _s5(proj, 0, t_set, e_set, m_set, a_set, row(s5_d[l]))

            w_cat = jnp.concatenate([lru_w_a[l], lru_w_x[l]], axis=-1).astype(BF16)
            nls = -LRU_C * jax.nn.softplus(-lru_lambda[l].astype(F32))
            out_lru = _rglru(proj, s5_w, lru_conv_w[l].astype(F32), row(lru_conv_b[l]), w_cat,
                             row(lru_b_a[l]), row(lru_b_x[l]), row(nls), heads, head_dim)

            merged = _merge(h, y_s5, out_lru, s5_w_glu[l].astype(BF16), row(s5_b_glu[l]),
                            w_in_l, o3, row(b_gate[l]),
                            w_proj_s5[l].astype(BF16), w_proj_lru[l].astype(BF16))
            pad = LANES - n_experts - n_groups
            w_router = jnp.concatenate([w_router_expert[l].astype(F32), w_router_group[l].astype(F32),
                                        jnp.zeros((d, pad), F32)], axis=1)
            w_router_hi = w_router.astype(BF16)
            w_router_lo = (w_router - w_router_hi.astype(F32)).astype(BF16)
            b_router = jnp.concatenate([b_router_expert[l].astype(F32), b_router_group[l].astype(F32),
                                        jnp.zeros((pad,), F32)]).reshape(1, LANES)
            x2, ht_slabs, route, route_t, counts = _out_route(
                merged, w_out[l].astype(BF16), xb, row(norm_ffn_w[l]),
                jnp.concatenate([w_router_hi, w_router_lo], axis=1), b_router,
                n_experts, n_groups, tm=ROUTE_TILE)

            pos1, pos2, npad, padstart, block_expert, n_used, nb = _dispatch_plan(
                route_t, counts, n_experts, MOE_BLOCK)
            def slots_by_tile(t):
                return jnp.concatenate([pos1.reshape(seq // t, 1, t), pos2.reshape(seq // t, 1, t)],
                                       axis=2)
            xs_slabs = _dispatch(npad, padstart, n_used, slots_by_tile(DISPATCH_TILE), ht_slabs,
                                 nb, MOE_BLOCK, DISPATCH_TILE)
            y_rows = _experts(block_expert, n_used, xs_slabs, w_e_gate[l], w_e_up[l], w_e_down[l])
            xb = _combine(slots_by_tile(COMBINE_TILE), y_rows, x2, route, row(norm_final_w),
                          final_norm=(l == depth - 1), tm=COMBINE_TILE)
        outs.append(xb)
    return jnp.stack(outs)
```

```python
import functools

import jax
import jax.numpy as jnp
from jax import lax
from jax.experimental import pallas as pl
from jax.experimental.pallas import tpu as pltpu

F32 = jnp.float32
BF16 = jnp.bfloat16
HIGHEST = lax.Precision.HIGHEST

NORM_EPS = 1e-6
LRU_C = 8.0
S5_GROUP = 16
CONV_WIDTH = 4
SUBLANES = 8
LANES = 128
S5_CHUNK = SUBLANES
S5_SET = LANES // S5_GROUP
VMEM_LIMIT = 56 * 1024 * 1024
ROUTE_TILE = 512
MOE_BLOCK = 256
DISPATCH_TILE = 512
COMBINE_TILE = 256


def _params(n_axes, vmem=VMEM_LIMIT):
    return pltpu.CompilerParams(dimension_semantics=("arbitrary",) * n_axes,
                                vmem_limit_bytes=vmem)


def _rms(x, w):
    ms = jnp.mean(x * x, axis=-1, keepdims=True)
    return x * lax.rsqrt(ms + NORM_EPS) * w


def _row_tile_ring(src_hbm, buf, sem, i, ni):
    tm = buf.shape[1]
    slot = i % 2

    def copy(tile, s):
        rows = pl.ds(pl.multiple_of(tile * tm, tm), tm)
        return pltpu.make_async_copy(src_hbm.at[rows, :], buf.at[s], sem.at[s])

    @pl.when(i == 0)
    def _():
        copy(0, 0).start()

    @pl.when(i + 1 < ni)
    def _():
        copy(i + 1, 1 - slot).start()

    copy(i, slot).wait()


def _in_proj_kernel(x_hbm, nw_ref, w_ref, h_ref, p_ref, xbuf, sem):
    i = pl.program_id(0)

    @pl.when(pl.program_id(1) == 0)
    def _():
        _row_tile_ring(x_hbm, xbuf, sem, i, pl.num_programs(0))
        h_ref[...] = _rms(xbuf[i % 2], nw_ref[...]).astype(BF16)

    p_ref[...] = jnp.dot(h_ref[...], w_ref[...], preferred_element_type=F32)


def _in_proj(x, norm_w, w, n_cols, tm=1024, tn=512):
    seq, d = x.shape
    return pl.pallas_call(
        _in_proj_kernel,
        grid=(seq // tm, n_cols // tn),
        in_specs=[
            pl.BlockSpec(memory_space=pl.ANY),
            pl.BlockSpec((1, d), lambda i, j: (0, 0)),
            pl.BlockSpec((d, tn), lambda i, j: (0, j)),
        ],
        out_specs=[
            pl.BlockSpec((tm, d), lambda i, j: (i, 0)),
            pl.BlockSpec((tm, tn), lambda i, j: (i, j)),
        ],
        out_shape=[
            jax.ShapeDtypeStruct((seq, d), BF16),
            jax.ShapeDtypeStruct((seq, n_cols), F32),
        ],
        scratch_shapes=[pltpu.VMEM((2, tm, d), F32), pltpu.SemaphoreType.DMA((2,))],
        compiler_params=_params(2),
        name="in_proj",
    )(x, norm_w, w)


def _s5_kernel(u_ref, t_ref, e_ref, m_ref, a_ref, d_ref, y_ref, x_scr, e_scr, s_scr, c_scr):
    tt = pl.program_id(1)
    n_chunks = x_scr.shape[0]
    half = e_scr.shape[1] // 2
    n_col = half // LANES

    @pl.when(tt == 0)
    def _():
        c_scr[...] = jnp.zeros_like(c_scr)

    for j in range(S5_CHUNK):
        x_scr[:, j * LANES:(j + 1) * LANES] = (
            u_ref[pl.ds(j, n_chunks, stride=S5_CHUNK), :].astype(BF16))

    e_scr[...] = jnp.dot(x_scr[...], e_ref[0], preferred_element_type=F32)

    row = lax.broadcasted_iota(jnp.int32, (SUBLANES, LANES), 0)

    def body(t, carry):
        r0 = pl.multiple_of(t * SUBLANES, SUBLANES)
        new = []
        for q in range(n_col):
            c_re, c_im = carry[2 * q], carry[2 * q + 1]
            re_cols = slice(q * LANES, (q + 1) * LANES)
            im_cols = slice(half + q * LANES, half + (q + 1) * LANES)
            x_re = e_scr[pl.ds(r0, SUBLANES), re_cols]
            x_im = e_scr[pl.ds(r0, SUBLANES), im_cols]
            for s in (1, 2, 4):
                p_re = a_ref[0, s - 1:s, re_cols]
                p_im = a_ref[0, SUBLANES + s - 1:SUBLANES + s, re_cols]
                keep = row >= s
                s_re = jnp.where(keep, pltpu.roll(x_re, s, 0), 0.0)
                s_im = jnp.where(keep, pltpu.roll(x_im, s, 0), 0.0)
                x_re, x_im = (x_re + p_re * s_re - p_im * s_im,
                              x_im + p_re * s_im + p_im * s_re)
            t_re = a_ref[0, 0:SUBLANES, re_cols]
            t_im = a_ref[0, SUBLANES:2 * SUBLANES, re_cols]
            x_re, x_im = (x_re + t_re * c_re - t_im * c_im,
                          x_im + t_re * c_im + t_im * c_re)
            s_scr[pl.ds(r0, SUBLANES), re_cols] = jnp.where(row >= 1, pltpu.roll(x_re, 1, 0), c_re)
            s_scr[pl.ds(r0, SUBLANES), im_cols] = jnp.where(row >= 1, pltpu.roll(x_im, 1, 0), c_im)
            new += [x_re[SUBLANES - 1:SUBLANES, :], x_im[SUBLANES - 1:SUBLANES, :]]
        return tuple(new)

    init = []
    for q in range(n_col):
        init += [c_scr[0:1, q * LANES:(q + 1) * LANES],
                 c_scr[0:1, half + q * LANES:half + (q + 1) * LANES]]
    last = lax.fori_loop(0, n_chunks // SUBLANES, body, tuple(init))
    for q in range(n_col):
        c_scr[0:1, q * LANES:(q + 1) * LANES] = last[2 * q]
        c_scr[0:1, half + q * LANES:half + (q + 1) * LANES] = last[2 * q + 1]

    y = jnp.dot(x_scr[...], t_ref[0], preferred_element_type=F32)
    y = y + jnp.dot(s_scr[...].astype(BF16), m_ref[0], preferred_element_type=F32)
    for j in range(S5_CHUNK):
        rows = pl.ds(j, n_chunks, stride=S5_CHUNK)
        yj = y[:, j * LANES:(j + 1) * LANES] + d_ref[...] * u_ref[rows, :]
        y_ref[rows, :] = jax.nn.gelu(yj)


def _s5(proj, col, t_set, e_set, m_set, a_set, d_skip, tile=4096):
    seq = proj.shape[0]
    width = d_skip.shape[1]
    n_sets = width // LANES
    n_chunks = tile // S5_CHUNK
    kdim = S5_CHUNK * LANES
    sdim = e_set.shape[2]
    c0 = col // LANES
    assert c0 * LANES == col
    return pl.pallas_call(
        _s5_kernel,
        grid=(n_sets, seq // tile),
        in_specs=[
            pl.BlockSpec((tile, LANES), lambda s, t: (t, c0 + s)),
            pl.BlockSpec((1, kdim, kdim), lambda s, t: (s, 0, 0)),
            pl.BlockSpec((1, kdim, sdim), lambda s, t: (s, 0, 0)),
            pl.BlockSpec((1, sdim, kdim), lambda s, t: (s, 0, 0)),
            pl.BlockSpec((1, 2 * SUBLANES, sdim // 2), lambda s, t: (s, 0, 0)),
            pl.BlockSpec((1, LANES), lambda s, t: (0, s)),
        ],
        out_specs=pl.BlockSpec((tile, LANES), lambda s, t: (t, s)),
        out_shape=jax.ShapeDtypeStruct((seq, width), F32),
        scratch_shapes=[
            pltpu.VMEM((n_chunks, kdim), BF16),
            pltpu.VMEM((n_chunks, sdim), F32),
            pltpu.VMEM((n_chunks, sdim), F32),
            pltpu.VMEM((SUBLANES, sdim), F32),
        ],
        compiler_params=_params(2),
        name="s5",
    )(proj, t_set, e_set, m_set, a_set, d_skip)


def _cmul(a, b):
    return a[0] * b[0] - a[1] * b[1], a[0] * b[1] + a[1] * b[0]


def _s5_matrices(lam_re, lam_im, log_step, b_re, b_im, c_re, c_im):
    g, p = lam_re.shape
    gs, tc, ns = S5_GROUP, S5_CHUNK, S5_SET
    n_sets = g // ns
    lam = (lam_re.astype(F32), lam_im.astype(F32))
    step = jnp.exp(log_step.astype(F32))[:, None]
    mag = jnp.exp(lam[0] * step)
    lam_bar = (mag * jnp.cos(lam[1] * step), mag * jnp.sin(lam[1] * step))
    den = lam[0] * lam[0] + lam[1] * lam[1]
    coef = _cmul((lam_bar[0] - 1.0, lam_bar[1]), (lam[0] / den, -lam[1] / den))
    b_c = (b_re.astype(F32), b_im.astype(F32))
    b_bar = _cmul((coef[0][..., None], coef[1][..., None]), b_c)
    c_c = (c_re.astype(F32), c_im.astype(F32))

    pows = [(jnp.ones_like(lam_bar[0]), jnp.zeros_like(lam_bar[0]))]
    for _ in range(tc):
        pows.append(_cmul(pows[-1], lam_bar))
    pw = (jnp.stack([q[0] for q in pows]), jnp.stack([q[1] for q in pows]))

    w = _cmul((pw[0][:tc, :, :, None], pw[1][:tc, :, :, None]), (b_bar[0][None], b_bar[1][None]))
    kmat = (jnp.einsum('gop,tgpc->tgoc', c_c[0], w[0], precision=HIGHEST)
            - jnp.einsum('gop,tgpc->tgoc', c_c[1], w[1], precision=HIGHEST))

    def diag_expand(a, row_group, col_group):
        tiled = jnp.tile(a.astype(BF16), (1,) * (a.ndim - 1) + (ns,))
        r = jnp.arange(tiled.shape[-2])[:, None] // row_group
        c = jnp.arange(tiled.shape[-1])[None, :] // col_group
        return jnp.where(r == c, tiled, jnp.zeros((), BF16))

    k_blk = kmat.reshape(tc, n_sets, ns, gs, gs).transpose(0, 1, 2, 4, 3)
    k_blk = diag_expand(k_blk.reshape(tc, n_sets, ns * gs, gs), gs, gs)
    t_rows = [jnp.concatenate([jnp.zeros_like(k_blk[:j]), k_blk[:tc - j]], axis=0)
              for j in range(tc)]
    t_set = jnp.stack(t_rows).transpose(2, 0, 3, 1, 4).reshape(n_sets, tc * LANES, tc * LANES)

    def e_half(x):
        x = x[::-1].reshape(tc, n_sets, ns, p, gs).transpose(0, 1, 2, 4, 3)
        x = diag_expand(x.reshape(tc, n_sets, ns * gs, p), gs, p)
        return x.transpose(1, 0, 2, 3).reshape(n_sets, tc * LANES, ns * p)
    e_set = jnp.concatenate([e_half(w[0]), e_half(w[1])], axis=2)

    m = _cmul((c_c[0].transpose(0, 2, 1)[None], c_c[1].transpose(0, 2, 1)[None]),
              (pw[0][1:, :, :, None], pw[1][1:, :, :, None]))

    def m_half(x):
        x = diag_expand(x.reshape(tc, n_sets, ns * p, gs), p, gs)
        return x.transpose(1, 2, 0, 3).reshape(n_sets, ns * p, tc * LANES)
    m_set = jnp.concatenate([m_half(m[0]), m_half(-m[1])], axis=1)

    a_c = (pw[0][tc], pw[1][tc])
    a_pows = [a_c]
    for _ in range(SUBLANES - 1):
        a_pows.append(_cmul(a_pows[-1], a_c))
    a_set = jnp.concatenate([jnp.stack([q[0] for q in a_pows]), jnp.stack([q[1] for q in a_pows])])
    a_set = a_set.reshape(2 * SUBLANES, n_sets, ns * p).transpose(1, 0, 2)
    return t_set, e_set, m_set, a_set


def _rglru_kernel(*refs, heads, head_dim, n_in):
    xl_refs, yl_refs = refs[:n_in], refs[n_in:2 * n_in]
    (cw_ref, cb_ref, w_ref, ba_ref, bx_ref, nls_ref, o_ref,
     tail, a_scr, b_scr, h_scr) = refs[2 * n_in:]
    i = pl.program_id(0)
    tm, blk_w = xl_refs[0].shape
    per_blk = blk_w // head_dim
    seg = tm // SUBLANES

    def head_cols(blocks, hd):
        lo = (hd % per_blk) * head_dim
        return blocks[hd // per_blk], slice(lo, lo + head_dim)

    @pl.when(i == 0)
    def _():
        tail[...] = jnp.zeros_like(tail)
        h_scr[...] = jnp.zeros_like(h_scr)

    first = jnp.logical_and(lax.broadcasted_iota(jnp.int32, (tm, head_dim), 0) == 0, i == 0)
    row8 = lax.broadcasted_iota(jnp.int32, (SUBLANES, head_dim), 0)
    for hd in range(heads):
        cs = slice(hd * head_dim, (hd + 1) * head_dim)
        x_ref, xs_cols = head_cols(xl_refs, hd)
        x = x_ref[:, xs_cols]
        prev = tail[:, cs]
        xc = cb_ref[:, cs] + x * cw_ref[CONV_WIDTH - 1:CONV_WIDTH, cs]
        for s in range(1, CONV_WIDTH):
            sh = pltpu.roll(x, s, 0)
            head = jnp.where(row8 < s, pltpu.roll(prev, s, 0), sh[0:SUBLANES])
            sh = jnp.concatenate([head, sh[SUBLANES:]], axis=0)
            xc = xc + sh * cw_ref[CONV_WIDTH - 1 - s:CONV_WIDTH - s, cs]
        g = jnp.dot(xc.astype(BF16), w_ref[hd], preferred_element_type=F32)
        r = jax.nn.sigmoid(g[:, :head_dim] + ba_ref[:, cs])
        gi = jax.nn.sigmoid(g[:, head_dim:] + bx_ref[:, cs])
        log_a = nls_ref[:, cs] * r
        a = jnp.exp(log_a)
        mult = jnp.sqrt(-jnp.tanh(log_a) * (a * a + 1.0))
        mult = jnp.where(first, 1.0, mult)
        b = xc * gi * mult
        for sg in range(SUBLANES):
            rows = pl.ds(sg, seg, stride=SUBLANES)
            a_scr[hd, rows, :] = a[sg * seg:(sg + 1) * seg]
            b_scr[hd, rows, :] = b[sg * seg:(sg + 1) * seg]
        tail[:, cs] = x[tm - SUBLANES:tm]

    def body(k, carry):
        r0 = pl.multiple_of(k * SUBLANES, SUBLANES)
        new = []
        for hd in range(heads):
            h_loc, a_cum = carry[2 * hd], carry[2 * hd + 1]
            a = a_scr[hd, pl.ds(r0, SUBLANES), :]
            h_loc = a * h_loc + b_scr[hd, pl.ds(r0, SUBLANES), :]
            a_cum = a * a_cum
            b_scr[hd, pl.ds(r0, SUBLANES), :] = h_loc
            a_scr[hd, pl.ds(r0, SUBLANES), :] = a_cum
            new += [h_loc, a_cum]
        return tuple(new)

    init = (jnp.zeros((SUBLANES, head_dim), F32), jnp.ones((SUBLANES, head_dim), F32)) * heads
    ends = lax.fori_loop(0, seg, body, init)

    for hd in range(heads):
        cs = slice(hd * head_dim, (hd + 1) * head_dim)
        h_end, a_end = ends[2 * hd], ends[2 * hd + 1]
        y_ref, y_cols = head_cols(yl_refs, hd)
        state = h_scr[:, cs]
        for sg in range(SUBLANES):
            rows = pl.ds(sg, seg, stride=SUBLANES)
            h = b_scr[hd, rows, :] + a_scr[hd, rows, :] * state
            y = y_ref[sg * seg:(sg + 1) * seg, y_cols]
            o_ref[sg * seg:(sg + 1) * seg, cs] = (h * jax.nn.gelu(y)).astype(o_ref.dtype)
            state = h_end[sg:sg + 1, :] + a_end[sg:sg + 1, :] * state
        h_scr[:, cs] = state


def _rglru(proj, col, conv_w, conv_b, w_cat, b_a, b_x, nls, heads, head_dim, tm=512, blk_w=512):
    seq = proj.shape[0]
    width = heads * head_dim
    assert head_dim == LANES
    n_in, c0 = width // blk_w, col // blk_w
    assert n_in * blk_w == width and c0 * blk_w == col and blk_w % head_dim == 0
    kern = functools.partial(_rglru_kernel, heads=heads, head_dim=head_dim, n_in=n_in)
    row = lambda i: (0, 0)
    col_block = lambda c: pl.BlockSpec((tm, blk_w), lambda i: (i, c))
    return pl.pallas_call(
        kern,
        grid=(seq // tm,),
        in_specs=[col_block(c0 + c) for c in range(2 * n_in)] + [
            pl.BlockSpec((CONV_WIDTH, width), row),
            pl.BlockSpec((1, width), row),
            pl.BlockSpec((heads, head_dim, 2 * head_dim), lambda i: (0, 0, 0)),
            pl.BlockSpec((1, width), row),
            pl.BlockSpec((1, width), row),
            pl.BlockSpec((1, width), row),
        ],
        out_specs=pl.BlockSpec((tm, width), lambda i: (i, 0)),
        out_shape=jax.ShapeDtypeStruct((seq, width), BF16),
        scratch_shapes=[
            pltpu.VMEM((SUBLANES, width), F32),
            pltpu.VMEM((heads, tm, head_dim), F32),
            pltpu.VMEM((heads, tm, head_dim), F32),
            pltpu.VMEM((1, width), F32),
        ],
        compiler_params=_params(1),
        name="rglru",
    )(*([proj] * (2 * n_in)), conv_w, conv_b, w_cat, b_a, b_x, nls)


def _merge_kernel(h_ref, y_ref, ol_ref, wglu_ref, bglu_ref, wgs_ref, wgl_ref, bgs_ref, bgl_ref,
                  wps_ref, wpl_ref, o_ref, os_scr):
    j = pl.program_id(1)

    @pl.when(j == 0)
    def _():
        y = y_ref[...]
        glu = jax.nn.sigmoid(
            jnp.dot(y.astype(BF16), wglu_ref[...], preferred_element_type=F32) + bglu_ref[...])
        os_scr[...] = (y * glu).astype(BF16)

    h = h_ref[...]
    g_s5 = jax.nn.sigmoid(jnp.dot(h, wgs_ref[...], preferred_element_type=F32) + bgs_ref[...])
    g_lru = jax.nn.sigmoid(jnp.dot(h, wgl_ref[...], preferred_element_type=F32) + bgl_ref[...])
    br_s5 = jnp.dot(os_scr[...], wps_ref[...], preferred_element_type=F32)
    br_lru = jnp.dot(ol_ref[...], wpl_ref[...], preferred_element_type=F32)
    o_ref[...] = (g_s5 * br_s5 + g_lru * br_lru).astype(o_ref.dtype)


def _merge(h, y_s5, out_lru, w_glu, b_glu, w_in, gate_col, b_gate, wp_s5, wp_lru,
           tm=1024, tn=512):
    seq, d = h.shape
    s5_w, lru_w = y_s5.shape[1], out_lru.shape[1]
    const = lambda i, j: (0, 0)
    c0, nd = gate_col // tn, d // tn
    return pl.pallas_call(
        _merge_kernel,
        grid=(seq // tm, nd),
        in_specs=[
            pl.BlockSpec((tm, d), lambda i, j: (i, 0)),
            pl.BlockSpec((tm, s5_w), lambda i, j: (i, 0)),
            pl.BlockSpec((tm, lru_w), lambda i, j: (i, 0)),
            pl.BlockSpec((s5_w, s5_w), const),
            pl.BlockSpec((1, s5_w), const),
            pl.BlockSpec((d, tn), lambda i, j: (0, c0 + j)),
            pl.BlockSpec((d, tn), lambda i, j: (0, c0 + nd + j)),
            pl.BlockSpec((1, tn), lambda i, j: (0, j)),
            pl.BlockSpec((1, tn), lambda i, j: (0, nd + j)),
            pl.BlockSpec((s5_w, tn), lambda i, j: (0, j)),
            pl.BlockSpec((lru_w, tn), lambda i, j: (0, j)),
        ],
        out_specs=pl.BlockSpec((tm, tn), lambda i, j: (i, j)),
        out_shape=jax.ShapeDtypeStruct((seq, d), BF16),
        scratch_shapes=[pltpu.VMEM((tm, s5_w), BF16)],
        compiler_params=_params(2),
        name="merge",
    )(h, y_s5, out_lru, w_glu, b_glu, w_in, w_in, b_gate, b_gate, wp_s5, wp_lru)


def _n_slabs(d):
    assert d % LANES == 0
    return d // LANES


def _store_slabs(ref, x):
    for s in range(ref.shape[0]):
        ref[s] = x[:, s * LANES:(s + 1) * LANES]


def _slab_row(ref, g):
    rows = ref.shape[2]
    shift = rows.bit_length() - 1
    assert rows == 1 << shift
    return ref.at[lax.shift_right_logical(g, shift), :, pl.ds(jnp.bitwise_and(g, rows - 1), 1), :]


def _slab_rows(ref, n):
    assert n <= ref.shape[2]
    return ref.at[0, :, pl.ds(0, n), :]


def _route(logits, n_experts, n_groups):
    per_group = n_experts // n_groups
    lane = lax.broadcasted_iota(jnp.int32, logits.shape, 1)
    big = jnp.int32(LANES)
    neg = jnp.float32(-jnp.inf)
    is_g = jnp.logical_and(lane >= n_experts, lane < n_experts + n_groups)
    lg = jnp.where(is_g, logits, neg)
    g_max = jnp.max(lg, axis=-1, keepdims=True)
    g_lane = jnp.min(jnp.where(lg == g_max, lane, big), axis=-1, keepdims=True)
    g_top = 1.0 / jnp.sum(jnp.where(is_g, jnp.exp(lg - g_max), 0.0), axis=-1, keepdims=True)
    g_idx = g_lane - n_experts
    in_group = jnp.logical_and(lane >= g_idx * per_group, lane < (g_idx + 1) * per_group)
    le = jnp.where(in_group, logits, neg)
    m1 = jnp.max(le, axis=-1, keepdims=True)
    i1 = jnp.min(jnp.where(le == m1, lane, big), axis=-1, keepdims=True)
    le2 = jnp.where(lane == i1, neg, le)
    m2 = jnp.max(le2, axis=-1, keepdims=True)
    i2 = jnp.min(jnp.where(le2 == m2, lane, big), axis=-1, keepdims=True)
    r = jnp.exp(m2 - m1)
    w1 = g_top / (1.0 + r)
    w2 = g_top * r / (1.0 + r)
    return i1, i2, w1, w2


def _out_route_kernel(m_ref, w_ref, x_ref, nw_ref, wr_ref, br_ref, x2_ref, ht_ref, route_ref,
                      route_t_ref, cnt_ref, cnt_scr, *, n_experts, n_groups):
    i = pl.program_id(0)
    tm = x_ref.shape[0]

    @pl.when(i == 0)
    def _():
        cnt_scr[...] = jnp.zeros_like(cnt_scr)

    x2 = x_ref[...] + jnp.dot(m_ref[...], w_ref[...], preferred_element_type=F32)
    x2_ref[...] = x2
    ht = _rms(x2, nw_ref[...])
    _store_slabs(ht_ref.at[0], ht)
    ht_hi = ht.astype(BF16)
    ht_lo = (ht - ht_hi.astype(F32)).astype(BF16)
    both = jnp.dot(ht_hi, wr_ref[...], preferred_element_type=F32)
    logits = (both[:, :LANES] + both[:, LANES:]
              + jnp.dot(ht_lo, wr_ref[:, :LANES], preferred_element_type=F32) + br_ref[...])
    i1, i2, w1, w2 = _route(logits, n_experts, n_groups)

    lane = lax.broadcasted_iota(jnp.int32, (tm, LANES), 1)
    oh1 = (lane == i1).astype(F32)
    oh2 = (lane == i2).astype(F32)
    oh = oh1 + oh2
    earlier = (lax.broadcasted_iota(jnp.int32, (tm, tm), 0)
               > lax.broadcasted_iota(jnp.int32, (tm, tm), 1)).astype(BF16)
    before = jnp.dot(earlier, oh.astype(BF16), preferred_element_type=F32) + cnt_scr[0:1, :]
    r1 = jnp.sum(before * oh1, axis=-1, keepdims=True)
    r2 = jnp.sum(before * oh2, axis=-1, keepdims=True)
    cnt_scr[0:1, :] = cnt_scr[0:1, :] + jnp.sum(oh, axis=0, keepdims=True)
    cnt_ref[...] = cnt_scr[...]

    cols = [i1.astype(F32), i2.astype(F32), r1, r2, w1, w2]
    route = jnp.zeros((tm, LANES), F32)
    for k, v in enumerate(cols):
        route = jnp.where(lane == k, v, route)
    route_ref[...] = route
    route_t_ref[...] = route.T[0:SUBLANES, :]


def _out_route(merged, w_out, x, norm_w, w_router, b_router, n_experts, n_groups, tm):
    seq, d = x.shape
    const = lambda i: (0, 0)
    kern = functools.partial(_out_route_kernel, n_experts=n_experts, n_groups=n_groups)
    return pl.pallas_call(
        kern,
        grid=(seq // tm,),
        in_specs=[
            pl.BlockSpec((tm, d), lambda i: (i, 0)),
            pl.BlockSpec((d, d), const),
            pl.BlockSpec((tm, d), lambda i: (i, 0)),
            pl.BlockSpec((1, d), const),
            pl.BlockSpec((d, 2 * LANES), const),
            pl.BlockSpec((1, LANES), const),
        ],
        out_specs=[
            pl.BlockSpec((tm, d), lambda i: (i, 0)),
            pl.BlockSpec((1, _n_slabs(d), tm, LANES), lambda i: (i, 0, 0, 0)),
            pl.BlockSpec((tm, LANES), lambda i: (i, 0)),
            pl.BlockSpec((SUBLANES, tm), lambda i: (0, i)),
            pl.BlockSpec((SUBLANES, LANES), const),
        ],
        out_shape=[
            jax.ShapeDtypeStruct((seq, d), F32),
            jax.ShapeDtypeStruct((seq // tm, _n_slabs(d), tm, LANES), F32),
            jax.ShapeDtypeStruct((seq, LANES), F32),
            jax.ShapeDtypeStruct((SUBLANES, seq), F32),
            jax.ShapeDtypeStruct((SUBLANES, LANES), F32),
        ],
        scratch_shapes=[pltpu.VMEM((SUBLANES, LANES), F32)],
        compiler_params=_params(1),
        name="out_route",
    )(merged, w_out, x, norm_w, w_router, b_router)


DMA_UNROLL = 8


def _dispatch_kernel(npad_ref, padstart_ref, nused_ref, idx_ref, ht_ref, xs_hbm, stage, zero_scr,
                     sem, pad_sem):
    i = pl.program_id(0)
    nt = pl.num_programs(0)
    n_experts = npad_ref.shape[0]
    nb, _, bm, _ = xs_hbm.shape
    tm = stage.shape[2]
    slot = i % 2
    chunk = min(tm, bm)

    def wait_rows(dsem):
        for _ in range(2 * tm // chunk):
            pltpu.make_async_copy(stage.at[0, :, pl.ds(0, chunk), :], _slab_rows(xs_hbm, chunk),
                                  dsem).wait()

    stage[slot] = ht_ref[0]

    def body(g, carry):
        for k in range(DMA_UNROLL):
            r = g * DMA_UNROLL + k
            src = stage.at[slot, :, pl.ds(r, 1), :]
            for half in range(2):
                dst = _slab_row(xs_hbm, idx_ref[0, 0, half * tm + r])
                pltpu.make_async_copy(src, dst, sem.at[slot]).start(priority=half)
        return carry
    lax.fori_loop(0, tm // DMA_UNROLL, body, 0)

    @pl.when(i > 0)
    def _():
        wait_rows(sem.at[1 - slot])

    @pl.when(i == nt - 1)
    def _():
        wait_rows(sem.at[slot])
        zero_scr[...] = jnp.zeros_like(zero_scr)
        zero_row = zero_scr.at[:, pl.ds(0, 1), :]
        shift = bm.bit_length() - 1
        for e in range(n_experts):
            p0 = padstart_ref[e]
            n_pad = npad_ref[e]
            blk = lax.shift_right_logical(p0, shift)
            row0 = jnp.bitwise_and(p0, bm - 1)
            n_single = jnp.minimum(jnp.bitwise_and(-row0, SUBLANES - 1), n_pad)
            q0 = row0 + n_single

            def single_start(j, carry):
                pltpu.make_async_copy(zero_row, _slab_row(xs_hbm, p0 + j), pad_sem).start()
                return carry

            def single_wait(j, carry):
                pltpu.make_async_copy(zero_row, _slab_row(xs_hbm, 0), pad_sem).wait()
                return carry

            def chunks(start):
                q = q0
                size = SUBLANES
                while size < bm:
                    take = jnp.logical_and(jnp.bitwise_and(q, size) != 0, n_pad > n_single)
                    src = zero_scr.at[:, pl.ds(0, size), :]

                    @pl.when(take)
                    def _():
                        if start:
                            dst = xs_hbm.at[blk, :, pl.ds(pl.multiple_of(q, SUBLANES), size), :]
                            pltpu.make_async_copy(src, dst, pad_sem).start()
                        else:
                            pltpu.make_async_copy(src, xs_hbm.at[0, :, pl.ds(0, size), :],
                                                  pad_sem).wait()
                    q = q + jnp.where(take, size, 0)
                    size *= 2
            lax.fori_loop(0, n_single, single_start, 0)
            chunks(True)
            lax.fori_loop(0, n_single, single_wait, 0)
            chunks(False)

        def blk_start(b, carry):
            pltpu.make_async_copy(zero_scr, xs_hbm.at[b], pad_sem).start()
            return carry

        def blk_wait(b, carry):
            pltpu.make_async_copy(zero_scr, xs_hbm.at[0], pad_sem).wait()
            return carry
        lax.fori_loop(nused_ref[0], nb, blk_start, 0)
        lax.fori_loop(nused_ref[0], nb, blk_wait, 0)


def _dispatch(npad, padstart, n_used, pos, ht_slabs, nb, bm, tm):
    _, ns, tile_rows, _ = ht_slabs.shape
    seq = pos.shape[0] * tm
    per_tile = tile_rows // tm
    assert per_tile * tm == tile_rows
    grid_spec = pltpu.PrefetchScalarGridSpec(
        num_scalar_prefetch=3,
        grid=(seq // tm,),
        in_specs=[
            pl.BlockSpec((1, 1, 2 * tm), lambda i, a, b, c: (i, 0, 0), memory_space=pltpu.SMEM),
            pl.BlockSpec((1, ns, tm, LANES),
                         lambda i, a, b, c: (i // per_tile, 0, i % per_tile, 0)),
        ],
        out_specs=pl.BlockSpec(memory_space=pl.ANY),
        scratch_shapes=[
            pltpu.VMEM((2, ns, tm, LANES), F32),
            pltpu.VMEM((ns, bm, LANES), F32),
            pltpu.SemaphoreType.DMA((2,)),
            pltpu.SemaphoreType.DMA(()),
        ],
    )
    return pl.pallas_call(
        _dispatch_kernel,
        grid_spec=grid_spec,
        out_shape=jax.ShapeDtypeStruct((nb, ns, bm, LANES), F32),
        compiler_params=pltpu.CompilerParams(dimension_semantics=("arbitrary",),
                                             has_side_effects=True),
        name="dispatch",
    )(npad, padstart, n_used, pos, ht_slabs)


def _experts_kernel(be_ref, nused_ref, par_ref, nxt_ref, xs_ref, wg_hbm, wu_hbm, wd_hbm, y_ref,
                    xs_scr, wgu_scr, wd_scr, wg_buf, wu_buf, wd_buf, sem):
    b = pl.program_id(0)
    ff = wd_scr.shape[0]

    def weight_copies(e, s):
        return [pltpu.make_async_copy(src.at[e], dst.at[s], sem.at[s])
                for src, dst in ((wg_hbm, wg_buf), (wu_hbm, wu_buf), (wd_hbm, wd_buf))]

    @pl.when(b < nused_ref[0])
    def _():
        @pl.when(jnp.logical_or(b == 0, be_ref[b] != be_ref[jnp.maximum(b - 1, 0)]))
        def _():
            s = par_ref[b]

            @pl.when(b == 0)
            def _():
                for c in weight_copies(be_ref[0], 0):
                    c.start()

            @pl.when(nxt_ref[b] >= 0)
            def _():
                for c in weight_copies(nxt_ref[b], 1 - s):
                    c.start()

            for c in weight_copies(be_ref[b], s):
                c.wait()
            wgu_scr[:, :ff] = wg_buf[s].astype(BF16)
            wgu_scr[:, ff:] = wu_buf[s].astype(BF16)
            wd_scr[...] = wd_buf[s].astype(BF16)

        for s in range(xs_ref.shape[1]):
            xs_scr[:, s * LANES:(s + 1) * LANES] = xs_ref[0, s].astype(BF16)
        hid = jnp.dot(xs_scr[...], wgu_scr[...], preferred_element_type=F32)
        act = (jax.nn.silu(hid[:, :ff]) * hid[:, ff:]).astype(BF16)
        y = jnp.dot(act, wd_scr[...], preferred_element_type=F32)
        _store_slabs(y_ref.at[0], y)

    @pl.when(b >= nused_ref[0])
    def _():
        y_ref[...] = jnp.zeros_like(y_ref)


def _experts(block_expert, n_used, xs_slabs, w_gate, w_up, w_down):
    nb, ns, bm, _ = xs_slabs.shape
    n_experts, d, ff = w_gate.shape
    blocks = jnp.arange(nb, dtype=jnp.int32)
    used = blocks < n_used[0]
    change = jnp.concatenate([jnp.ones((1,), jnp.bool_), block_expert[1:] != block_expert[:-1]])
    parity = ((jnp.cumsum(change.astype(jnp.int32)) - 1) % 2).astype(jnp.int32)
    later = jnp.where(jnp.logical_and(used[None, :], block_expert[None, :] > block_expert[:, None]),
                      block_expert[None, :], n_experts)
    nxt = jnp.min(later, axis=1)
    nxt = jnp.where(nxt < n_experts, nxt, -1).astype(jnp.int32)
    anywhere = pl.BlockSpec(memory_space=pl.ANY)
    grid_spec = pltpu.PrefetchScalarGridSpec(
        num_scalar_prefetch=4,
        grid=(nb,),
        in_specs=[
            pl.BlockSpec((1, ns, bm, LANES), lambda b, *_: (b, 0, 0, 0)),
            anywhere, anywhere, anywhere,
        ],
        out_specs=pl.BlockSpec((1, ns, bm, LANES), lambda b, *_: (b, 0, 0, 0)),
        scratch_shapes=[
            pltpu.VMEM((bm, d), BF16),
            pltpu.VMEM((d, 2 * ff), BF16),
            pltpu.VMEM((ff, d), BF16),
            pltpu.VMEM((2, d, ff), F32),
            pltpu.VMEM((2, d, ff), F32),
            pltpu.VMEM((2, ff, d), F32),
            pltpu.SemaphoreType.DMA((2,)),
        ],
    )
    return pl.pallas_call(
        _experts_kernel,
        grid_spec=grid_spec,
        out_shape=jax.ShapeDtypeStruct((nb, ns, bm, LANES), F32),
        compiler_params=_params(1),
        name="experts",
    )(block_expert, n_used, parity, nxt, xs_slabs, w_gate, w_up, w_down)


def _combine_kernel(idx_ref, idx_next_ref, y_hbm, x_ref, route_ref, fw_ref, o_ref, ybuf, sem,
                    *, final_norm):
    i = pl.program_id(0)
    nt = pl.num_programs(0)
    tm = x_ref.shape[0]
    slot = i % 2

    def start_gather(ids, buf, dsem):
        def body(g, carry):
            for k in range(DMA_UNROLL):
                r = g * DMA_UNROLL + k
                pltpu.make_async_copy(_slab_row(y_hbm, ids[0, 0, r]),
                                      buf.at[:, pl.ds(r, 1), :], dsem).start(priority=k % 2)
            return carry
        lax.fori_loop(0, 2 * tm // DMA_UNROLL, body, 0)

    @pl.when(i == 0)
    def _():
        start_gather(idx_ref, ybuf.at[0], sem.at[0])

    @pl.when(i + 1 < nt)
    def _():
        start_gather(idx_next_ref, ybuf.at[1 - slot], sem.at[1 - slot])

    buf = ybuf.at[slot]
    chunk = min(tm, y_hbm.shape[2])
    for c in range(2 * tm // chunk):
        pltpu.make_async_copy(_slab_rows(y_hbm, chunk), buf.at[:, pl.ds(c * chunk, chunk), :],
                              sem.at[slot]).wait()
    w1 = route_ref[:, 4:5]
    w2 = route_ref[:, 5:6]
    ssq = jnp.zeros((tm, 1), F32)
    for s in range(buf.shape[0]):
        cols = slice(s * LANES, (s + 1) * LANES)
        v = x_ref[:, cols] + w1 * buf[s, 0:tm, :] + w2 * buf[s, tm:2 * tm, :]
        o_ref[:, cols] = v
        ssq = ssq + jnp.sum(v * v, axis=-1, keepdims=True)
    if final_norm:
        scale = lax.rsqrt(ssq / o_ref.shape[1] + NORM_EPS)
        o_ref[...] = o_ref[...] * scale * fw_ref[...]


def _combine(pos, y_rows, x2, route, final_w, final_norm, tm):
    seq, d = x2.shape
    nt = seq // tm
    kern = functools.partial(_combine_kernel, final_norm=final_norm)
    return pl.pallas_call(
        kern,
        grid=(nt,),
        in_specs=[
            pl.BlockSpec((1, 1, 2 * tm), lambda i: (i, 0, 0), memory_space=pltpu.SMEM),
            pl.BlockSpec((1, 1, 2 * tm), lambda i: (jnp.minimum(i + 1, nt - 1), 0, 0),
                         memory_space=pltpu.SMEM),
            pl.BlockSpec(memory_space=pl.ANY),
            pl.BlockSpec((tm, d), lambda i: (i, 0)),
            pl.BlockSpec((tm, LANES), lambda i: (i, 0)),
            pl.BlockSpec((1, d), lambda i: (0, 0)),
        ],
        out_specs=pl.BlockSpec((tm, d), lambda i: (i, 0)),
        out_shape=jax.ShapeDtypeStruct((seq, d), F32),
        scratch_shapes=[
            pltpu.VMEM((2, _n_slabs(d), 2 * tm, LANES), F32),
            pltpu.SemaphoreType.DMA((2,)),
        ],
        compiler_params=_params(1),
        name="combine",
    )(pos, pos, y_rows, x2, route, final_w)


def _dispatch_plan(route_t, counts, n_experts, bm):
    seq = route_t.shape[1]
    nb = (2 * seq) // bm + n_experts
    e1, e2, r1, r2 = (route_t[k].astype(jnp.int32) for k in range(4))
    cnt = counts[0, :n_experts].astype(jnp.int32)
    cnt_pad = ((cnt + bm - 1) // bm) * bm
    ends = jnp.cumsum(cnt_pad)
    starts = ends - cnt_pad
    experts = jnp.arange(n_experts, dtype=jnp.int32)

    def start_of(e):
        return jnp.sum(jnp.where(e[:, None] == experts[None, :], starts[None, :], 0), axis=1)
    pos1 = start_of(e1) + r1
    pos2 = start_of(e2) + r2
    blk_start = jnp.arange(nb, dtype=jnp.int32) * bm
    n_before = jnp.sum((ends[None, :] <= blk_start[:, None]).astype(jnp.int32), axis=1)
    block_expert = jnp.minimum(n_before, n_experts - 1)
    n_used = (ends[-1:] // bm).astype(jnp.int32)
    return pos1, pos2, cnt_pad - cnt, starts + cnt, block_expert, n_used, nb


def kernel(x, norm_mix_w, w_in, b_gate, s5_lam_re, s5_lam_im, s5_log_step, s5_b_re, s5_b_im, s5_c_re, s5_c_im, s5_d, s5_w_glu, s5_b_glu, lru_conv_w, lru_conv_b, lru_w_a, lru_b_a, lru_w_x, lru_b_x, lru_lambda, w_proj_s5, w_proj_lru, w_out, norm_ffn_w, w_router_group, b_router_group, w_router_expert, b_router_expert, w_e_gate, w_e_up, w_e_down, norm_final_w):
    depth = w_in.shape[0]
    bsz, seq, d = x.shape
    s5_w = s5_w_glu.shape[-1]
    lru_w = lru_conv_b.shape[-1]
    heads, head_dim = lru_w_a.shape[1], lru_w_a.shape[2]
    n_groups = w_router_group.shape[-1]
    n_experts = w_router_expert.shape[-1]
    o3 = s5_w + 2 * lru_w

    outs = []
    for b in range(bsz):
        xb = x[b]
        for l in range(depth):
            row = lambda v: v.astype(F32).reshape(1, -1)
            w_in_l = w_in[l].astype(BF16)
            h, proj = _in_proj(xb, row(norm_mix_w[l]), w_in_l, o3)

            t_set, e_set, m_set, a_set = _s5_matrices(
                s5_lam_re[l], s5_lam_im[l], s5_log_step[l], s5_b_re[l], s5_b_im[l],
                s5_c_re[l], s5_c_im[l])
            y_s5 = _s5(proj, 0, t_set, e_set, m_set, a_set, row(s5_d[l]))

            w_cat = jnp.concatenate([lru_w_a[l], lru_w_x[l]], axis=-1).astype(BF16)
            nls = -LRU_C * jax.nn.softplus(-lru_lambda[l].astype(F32))
            out_lru = _rglru(proj, s5_w, lru_conv_w[l].astype(F32), row(lru_conv_b[l]), w_cat,
                             row(lru_b_a[l]), row(lru_b_x[l]), row(nls), heads, head_dim)

            merged = _merge(h, y_s5, out_lru, s5_w_glu[l].astype(BF16), row(s5_b_glu[l]),
                            w_in_l, o3, row(b_gate[l]),
                            w_proj_s5[l].astype(BF16), w_proj_lru[l].astype(BF16))
            pad = LANES - n_experts - n_groups
            w_router = jnp.concatenate([w_router_expert[l].astype(F32), w_router_group[l].astype(F32),
                                        jnp.zeros((d, pad), F32)], axis=1)
            w_router_hi = w_router.astype(BF16)
            w_router_lo = (w_router - w_router_hi.astype(F32)).astype(BF16)
            b_router = jnp.concatenate([b_router_expert[l].astype(F32), b_router_group[l].astype(F32),
                                        jnp.zeros((pad,), F32)]).reshape(1, LANES)
            x2, ht_slabs, route, route_t, counts = _out_route(
                merged, w_out[l].astype(BF16), xb, row(norm_ffn_w[l]),
                jnp.concatenate([w_router_hi, w_router_lo], axis=1), b_router,
                n_experts, n_groups, tm=ROUTE_TILE)

            pos1, pos2, npad, padstart, block_expert, n_used, nb = _dispatch_plan(
                route_t, counts, n_experts, MOE_BLOCK)
            def slots_by_tile(t):
                return jnp.concatenate([pos1.reshape(seq // t, 1, t), pos2.reshape(seq // t, 1, t)],
                                       axis=2)
            xs_slabs = _dispatch(npad, padstart, n_used, slots_by_tile(DISPATCH_TILE), ht_slabs,
                                 nb, MOE_BLOCK, DISPATCH_TILE)
            y_rows = _experts(block_expert, n_used, xs_slabs, w_e_gate[l], w_e_up[l], w_e_down[l])
            xb = _combine(slots_by_tile(COMBINE_TILE), y_rows, x2, route, row(norm_final_w),
                          final_norm=(l == depth - 1), tm=COMBINE_TILE)
        outs.append(xb)
    return jnp.stack(outs)
```

```python
import functools

import jax
import jax.numpy as jnp
from jax import lax
from jax.experimental import pallas as pl
from jax.experimental.pallas import tpu as pltpu

F32 = jnp.float32
BF16 = jnp.bfloat16
HIGHEST = lax.Precision.HIGHEST

NORM_EPS = 1e-6
LRU_C = 8.0
S5_GROUP = 16
CONV_WIDTH = 4
SUBLANES = 8
LANES = 128
S5_CHUNK = SUBLANES
S5_SET = LANES // S5_GROUP
VMEM_LIMIT = 56 * 1024 * 1024
ROUTE_TILE = 512
MOE_BLOCK = 256
DISPATCH_TILE = 512
COMBINE_TILE = 256


def _params(n_axes, vmem=VMEM_LIMIT):
    return pltpu.CompilerParams(dimension_semantics=("arbitrary",) * n_axes,
                                vmem_limit_bytes=vmem)


def _sigmoid(x):
    return 0.5 * jnp.tanh(0.5 * x) + 0.5


def _rms(x, w):
    ms = jnp.mean(x * x, axis=-1, keepdims=True)
    return x * lax.rsqrt(ms + NORM_EPS) * w


def _row_tile_ring(src_hbm, buf, sem, i, ni):
    tm = buf.shape[1]
    slot = i % 2

    def copy(tile, s):
        rows = pl.ds(pl.multiple_of(tile * tm, tm), tm)
        return pltpu.make_async_copy(src_hbm.at[rows, :], buf.at[s], sem.at[s])

    @pl.when(i == 0)
    def _():
        copy(0, 0).start()

    @pl.when(i + 1 < ni)
    def _():
        copy(i + 1, 1 - slot).start()

    copy(i, slot).wait()


def _in_proj_kernel(x_hbm, nw_ref, w_ref, h_ref, p_ref, xbuf, sem):
    i = pl.program_id(0)

    @pl.when(pl.program_id(1) == 0)
    def _():
        _row_tile_ring(x_hbm, xbuf, sem, i, pl.num_programs(0))
        h_ref[...] = _rms(xbuf[i % 2], nw_ref[...]).astype(BF16)

    p_ref[...] = jnp.dot(h_ref[...], w_ref[...], preferred_element_type=F32)


def _in_proj(x, norm_w, w, n_cols, tm=1024, tn=512):
    seq, d = x.shape
    return pl.pallas_call(
        _in_proj_kernel,
        grid=(seq // tm, n_cols // tn),
        in_specs=[
            pl.BlockSpec(memory_space=pl.ANY),
            pl.BlockSpec((1, d), lambda i, j: (0, 0)),
            pl.BlockSpec((d, tn), lambda i, j: (0, j)),
        ],
        out_specs=[
            pl.BlockSpec((tm, d), lambda i, j: (i, 0)),
            pl.BlockSpec((tm, tn), lambda i, j: (i, j)),
        ],
        out_shape=[
            jax.ShapeDtypeStruct((seq, d), BF16),
            jax.ShapeDtypeStruct((seq, n_cols), F32),
        ],
        scratch_shapes=[pltpu.VMEM((2, tm, d), F32), pltpu.SemaphoreType.DMA((2,))],
        compiler_params=_params(2),
        name="in_proj",
    )(x, norm_w, w)


def _s5_kernel(u_ref, t_ref, e_ref, m_ref, a_ref, d_ref, y_ref, x_scr, e_scr, s_scr, c_scr):
    tt = pl.program_id(1)
    n_chunks = x_scr.shape[0]
    half = e_scr.shape[1] // 2
    n_col = half // LANES

    @pl.when(tt == 0)
    def _():
        c_scr[...] = jnp.zeros_like(c_scr)

    for j in range(S5_CHUNK):
        x_scr[:, j * LANES:(j + 1) * LANES] = (
            u_ref[pl.ds(j, n_chunks, stride=S5_CHUNK), :].astype(BF16))

    e_scr[...] = jnp.dot(x_scr[...], e_ref[0], preferred_element_type=F32)

    row = lax.broadcasted_iota(jnp.int32, (SUBLANES, LANES), 0)

    def body(t, carry):
        r0 = pl.multiple_of(t * SUBLANES, SUBLANES)
        new = []
        for q in range(n_col):
            c_re, c_im = carry[2 * q], carry[2 * q + 1]
            re_cols = slice(q * LANES, (q + 1) * LANES)
            im_cols = slice(half + q * LANES, half + (q + 1) * LANES)
            x_re = e_scr[pl.ds(r0, SUBLANES), re_cols]
            x_im = e_scr[pl.ds(r0, SUBLANES), im_cols]
            for s in (1, 2, 4):
                p_re = a_ref[0, s - 1:s, re_cols]
                p_im = a_ref[0, SUBLANES + s - 1:SUBLANES + s, re_cols]
                keep = row >= s
                s_re = jnp.where(keep, pltpu.roll(x_re, s, 0), 0.0)
                s_im = jnp.where(keep, pltpu.roll(x_im, s, 0), 0.0)
                x_re, x_im = (x_re + p_re * s_re - p_im * s_im,
                              x_im + p_re * s_im + p_im * s_re)
            t_re = a_ref[0, 0:SUBLANES, re_cols]
            t_im = a_ref[0, SUBLANES:2 * SUBLANES, re_cols]
            x_re, x_im = (x_re + t_re * c_re - t_im * c_im,
                          x_im + t_re * c_im + t_im * c_re)
            s_scr[pl.ds(r0, SUBLANES), re_cols] = jnp.where(row >= 1, pltpu.roll(x_re, 1, 0), c_re)
            s_scr[pl.ds(r0, SUBLANES), im_cols] = jnp.where(row >= 1, pltpu.roll(x_im, 1, 0), c_im)
            new += [x_re[SUBLANES - 1:SUBLANES, :], x_im[SUBLANES - 1:SUBLANES, :]]
        return tuple(new)

    init = []
    for q in range(n_col):
        init += [c_scr[0:1, q * LANES:(q + 1) * LANES],
                 c_scr[0:1, half + q * LANES:half + (q + 1) * LANES]]
    last = lax.fori_loop(0, n_chunks // SUBLANES, body, tuple(init))
    for q in range(n_col):
        c_scr[0:1, q * LANES:(q + 1) * LANES] = last[2 * q]
        c_scr[0:1, half + q * LANES:half + (q + 1) * LANES] = last[2 * q + 1]

    y = jnp.dot(x_scr[...], t_ref[0], preferred_element_type=F32)
    y = y + jnp.dot(s_scr[...].astype(BF16), m_ref[0], preferred_element_type=F32)
    for j in range(S5_CHUNK):
        rows = pl.ds(j, n_chunks, stride=S5_CHUNK)
        yj = y[:, j * LANES:(j + 1) * LANES] + d_ref[...] * u_ref[rows, :]
        y_ref[rows, :] = jax.nn.gelu(yj)


def _s5(proj, col, t_set, e_set, m_set, a_set, d_skip, tile=4096):
    seq = proj.shape[0]
    width = d_skip.shape[1]
    n_sets = width // LANES
    n_chunks = tile // S5_CHUNK
    kdim = S5_CHUNK * LANES
    sdim = e_set.shape[2]
    c0 = col // LANES
    assert c0 * LANES == col
    return pl.pallas_call(
        _s5_kernel,
        grid=(n_sets, seq // tile),
        in_specs=[
            pl.BlockSpec((tile, LANES), lambda s, t: (t, c0 + s)),
            pl.BlockSpec((1, kdim, kdim), lambda s, t: (s, 0, 0)),
            pl.BlockSpec((1, kdim, sdim), lambda s, t: (s, 0, 0)),
            pl.BlockSpec((1, sdim, kdim), lambda s, t: (s, 0, 0)),
            pl.BlockSpec((1, 2 * SUBLANES, sdim // 2), lambda s, t: (s, 0, 0)),
            pl.BlockSpec((1, LANES), lambda s, t: (0, s)),
        ],
        out_specs=pl.BlockSpec((tile, LANES), lambda s, t: (t, s)),
        out_shape=jax.ShapeDtypeStruct((seq, width), F32),
        scratch_shapes=[
            pltpu.VMEM((n_chunks, kdim), BF16),
            pltpu.VMEM((n_chunks, sdim), F32),
            pltpu.VMEM((n_chunks, sdim), F32),
            pltpu.VMEM((SUBLANES, sdim), F32),
        ],
        compiler_params=_params(2),
        name="s5",
    )(proj, t_set, e_set, m_set, a_set, d_skip)


def _cmul(a, b):
    return a[0] * b[0] - a[1] * b[1], a[0] * b[1] + a[1] * b[0]


def _s5_matrices(lam_re, lam_im, log_step, b_re, b_im, c_re, c_im):
    g, p = lam_re.shape
    gs, tc, ns = S5_GROUP, S5_CHUNK, S5_SET
    n_sets = g // ns
    lam = (lam_re.astype(F32), lam_im.astype(F32))
    step = jnp.exp(log_step.astype(F32))[:, None]
    mag = jnp.exp(lam[0] * step)
    lam_bar = (mag * jnp.cos(lam[1] * step), mag * jnp.sin(lam[1] * step))
    den = lam[0] * lam[0] + lam[1] * lam[1]
    coef = _cmul((lam_bar[0] - 1.0, lam_bar[1]), (lam[0] / den, -lam[1] / den))
    b_c = (b_re.astype(F32), b_im.astype(F32))
    b_bar = _cmul((coef[0][..., None], coef[1][..., None]), b_c)
    c_c = (c_re.astype(F32), c_im.astype(F32))

    pows = [(jnp.ones_like(lam_bar[0]), jnp.zeros_like(lam_bar[0]))]
    for _ in range(tc):
        pows.append(_cmul(pows[-1], lam_bar))
    pw = (jnp.stack([q[0] for q in pows]), jnp.stack([q[1] for q in pows]))

    w = _cmul((pw[0][:tc, :, :, None], pw[1][:tc, :, :, None]), (b_bar[0][None], b_bar[1][None]))
    kmat = (jnp.einsum('gop,tgpc->tgoc', c_c[0], w[0], precision=HIGHEST)
            - jnp.einsum('gop,tgpc->tgoc', c_c[1], w[1], precision=HIGHEST))

    def diag_expand(a, row_group, col_group):
        tiled = jnp.tile(a.astype(BF16), (1,) * (a.ndim - 1) + (ns,))
        r = jnp.arange(tiled.shape[-2])[:, None] // row_group
        c = jnp.arange(tiled.shape[-1])[None, :] // col_group
        return jnp.where(r == c, tiled, jnp.zeros((), BF16))

    k_blk = kmat.reshape(tc, n_sets, ns, gs, gs).transpose(0, 1, 2, 4, 3)
    k_blk = diag_expand(k_blk.reshape(tc, n_sets, ns * gs, gs), gs, gs)
    t_rows = [jnp.concatenate([jnp.zeros_like(k_blk[:j]), k_blk[:tc - j]], axis=0)
              for j in range(tc)]
    t_set = jnp.stack(t_rows).transpose(2, 0, 3, 1, 4).reshape(n_sets, tc * LANES, tc * LANES)

    def e_half(x):
        x = x[::-1].reshape(tc, n_sets, ns, p, gs).transpose(0, 1, 2, 4, 3)
        x = diag_expand(x.reshape(tc, n_sets, ns * gs, p), gs, p)
        return x.transpose(1, 0, 2, 3).reshape(n_sets, tc * LANES, ns * p)
    e_set = jnp.concatenate([e_half(w[0]), e_half(w[1])], axis=2)

    m = _cmul((c_c[0].transpose(0, 2, 1)[None], c_c[1].transpose(0, 2, 1)[None]),
              (pw[0][1:, :, :, None], pw[1][1:, :, :, None]))

    def m_half(x):
        x = diag_expand(x.reshape(tc, n_sets, ns * p, gs), p, gs)
        return x.transpose(1, 2, 0, 3).reshape(n_sets, ns * p, tc * LANES)
    m_set = jnp.concatenate([m_half(m[0]), m_half(-m[1])], axis=1)

    a_c = (pw[0][tc], pw[1][tc])
    a_pows = [a_c]
    for _ in range(SUBLANES - 1):
        a_pows.append(_cmul(a_pows[-1], a_c))
    a_set = jnp.concatenate([jnp.stack([q[0] for q in a_pows]), jnp.stack([q[1] for q in a_pows])])
    a_set = a_set.reshape(2 * SUBLANES, n_sets, ns * p).transpose(1, 0, 2)
    return t_set, e_set, m_set, a_set


def _rglru_kernel(*refs, heads, head_dim, n_in):
    xl_refs, yl_refs = refs[:n_in], refs[n_in:2 * n_in]
    (cw_ref, cb_ref, w_ref, ba_ref, bx_ref, nls_ref, o_ref,
     tail, a_scr, b_scr, h_scr) = refs[2 * n_in:]
    i = pl.program_id(0)
    tm, blk_w = xl_refs[0].shape
    per_blk = blk_w // head_dim
    seg = tm // SUBLANES

    def head_cols(blocks, hd):
        lo = (hd % per_blk) * head_dim
        return blocks[hd // per_blk], slice(lo, lo + head_dim)

    @pl.when(i == 0)
    def _():
        tail[...] = jnp.zeros_like(tail)
        h_scr[...] = jnp.zeros_like(h_scr)

    first = jnp.logical_and(lax.broadcasted_iota(jnp.int32, (tm, head_dim), 0) == 0, i == 0)
    row8 = lax.broadcasted_iota(jnp.int32, (SUBLANES, head_dim), 0)
    for hd in range(heads):
        cs = slice(hd * head_dim, (hd + 1) * head_dim)
        x_ref, xs_cols = head_cols(xl_refs, hd)
        x = x_ref[:, xs_cols]
        prev = tail[:, cs]
        xc = cb_ref[:, cs] + x * cw_ref[CONV_WIDTH - 1:CONV_WIDTH, cs]
        for s in range(1, CONV_WIDTH):
            sh = pltpu.roll(x, s, 0)
            head = jnp.where(row8 < s, pltpu.roll(prev, s, 0), sh[0:SUBLANES])
            sh = jnp.concatenate([head, sh[SUBLANES:]], axis=0)
            xc = xc + sh * cw_ref[CONV_WIDTH - 1 - s:CONV_WIDTH - s, cs]
        g = jnp.dot(xc.astype(BF16), w_ref[hd], preferred_element_type=F32)
        r = _sigmoid(g[:, :head_dim] + ba_ref[:, cs])
        gi = _sigmoid(g[:, head_dim:] + bx_ref[:, cs])
        log_a = nls_ref[:, cs] * r
        a = jnp.exp(log_a)
        v = -jnp.tanh(log_a) * (a * a + 1.0)
        mult = jnp.where(v > 0.0, v * lax.rsqrt(v), 0.0)
        mult = jnp.where(first, 1.0, mult)
        b = xc * gi * mult
        for sg in range(SUBLANES):
            rows = pl.ds(sg, seg, stride=SUBLANES)
            a_scr[hd, rows, :] = a[sg * seg:(sg + 1) * seg]
            b_scr[hd, rows, :] = b[sg * seg:(sg + 1) * seg]
        tail[:, cs] = x[tm - SUBLANES:tm]

    def body(k, carry):
        r0 = pl.multiple_of(k * SUBLANES, SUBLANES)
        new = []
        for hd in range(heads):
            h_loc, a_cum = carry[2 * hd], carry[2 * hd + 1]
            a = a_scr[hd, pl.ds(r0, SUBLANES), :]
            h_loc = a * h_loc + b_scr[hd, pl.ds(r0, SUBLANES), :]
            a_cum = a * a_cum
            b_scr[hd, pl.ds(r0, SUBLANES), :] = h_loc
            a_scr[hd, pl.ds(r0, SUBLANES), :] = a_cum
            new += [h_loc, a_cum]
        return tuple(new)

    init = (jnp.zeros((SUBLANES, head_dim), F32), jnp.ones((SUBLANES, head_dim), F32)) * heads
    ends = lax.fori_loop(0, seg, body, init)

    for hd in range(heads):
        cs = slice(hd * head_dim, (hd + 1) * head_dim)
        h_end, a_end = ends[2 * hd], ends[2 * hd + 1]
        y_ref, y_cols = head_cols(yl_refs, hd)
        state = h_scr[:, cs]
        for sg in range(SUBLANES):
            rows = pl.ds(sg, seg, stride=SUBLANES)
            h = b_scr[hd, rows, :] + a_scr[hd, rows, :] * state
            y = y_ref[sg * seg:(sg + 1) * seg, y_cols]
            o_ref[sg * seg:(sg + 1) * seg, cs] = (h * jax.nn.gelu(y)).astype(o_ref.dtype)
            state = h_end[sg:sg + 1, :] + a_end[sg:sg + 1, :] * state
        h_scr[:, cs] = state


def _rglru(proj, col, conv_w, conv_b, w_cat, b_a, b_x, nls, heads, head_dim, tm=512, blk_w=512):
    seq = proj.shape[0]
    width = heads * head_dim
    assert head_dim == LANES
    n_in, c0 = width // blk_w, col // blk_w
    assert n_in * blk_w == width and c0 * blk_w == col and blk_w % head_dim == 0
    kern = functools.partial(_rglru_kernel, heads=heads, head_dim=head_dim, n_in=n_in)
    row = lambda i: (0, 0)
    col_block = lambda c: pl.BlockSpec((tm, blk_w), lambda i: (i, c))
    return pl.pallas_call(
        kern,
        grid=(seq // tm,),
        in_specs=[col_block(c0 + c) for c in range(2 * n_in)] + [
            pl.BlockSpec((CONV_WIDTH, width), row),
            pl.BlockSpec((1, width), row),
            pl.BlockSpec((heads, head_dim, 2 * head_dim), lambda i: (0, 0, 0)),
            pl.BlockSpec((1, width), row),
            pl.BlockSpec((1, width), row),
            pl.BlockSpec((1, width), row),
        ],
        out_specs=pl.BlockSpec((tm, width), lambda i: (i, 0)),
        out_shape=jax.ShapeDtypeStruct((seq, width), BF16),
        scratch_shapes=[
            pltpu.VMEM((SUBLANES, width), F32),
            pltpu.VMEM((heads, tm, head_dim), F32),
            pltpu.VMEM((heads, tm, head_dim), F32),
            pltpu.VMEM((1, width), F32),
        ],
        compiler_params=_params(1),
        name="rglru",
    )(*([proj] * (2 * n_in)), conv_w, conv_b, w_cat, b_a, b_x, nls)


def _merge_kernel(h_ref, y_ref, ol_ref, wglu_ref, bglu_ref, wgs_ref, wgl_ref, bgs_ref, bgl_ref,
                  wps_ref, wpl_ref, o_ref, os_scr):
    j = pl.program_id(1)

    @pl.when(j == 0)
    def _():
        y = y_ref[...]
        glu = jax.nn.sigmoid(
            jnp.dot(y.astype(BF16), wglu_ref[...], preferred_element_type=F32) + bglu_ref[...])
        os_scr[...] = (y * glu).astype(BF16)

    h = h_ref[...]
    g_s5 = jax.nn.sigmoid(jnp.dot(h, wgs_ref[...], preferred_element_type=F32) + bgs_ref[...])
    g_lru = jax.nn.sigmoid(jnp.dot(h, wgl_ref[...], preferred_element_type=F32) + bgl_ref[...])
    br_s5 = jnp.dot(os_scr[...], wps_ref[...], preferred_element_type=F32)
    br_lru = jnp.dot(ol_ref[...], wpl_ref[...], preferred_element_type=F32)
    o_ref[...] = (g_s5 * br_s5 + g_lru * br_lru).astype(o_ref.dtype)


def _merge(h, y_s5, out_lru, w_glu, b_glu, w_in, gate_col, b_gate, wp_s5, wp_lru,
           tm=1024, tn=512):
    seq, d = h.shape
    s5_w, lru_w = y_s5.shape[1], out_lru.shape[1]
    const = lambda i, j: (0, 0)
    c0, nd = gate_col // tn, d // tn
    return pl.pallas_call(
        _merge_kernel,
        grid=(seq // tm, nd),
        in_specs=[
            pl.BlockSpec((tm, d), lambda i, j: (i, 0)),
            pl.BlockSpec((tm, s5_w), lambda i, j: (i, 0)),
            pl.BlockSpec((tm, lru_w), lambda i, j: (i, 0)),
            pl.BlockSpec((s5_w, s5_w), const),
            pl.BlockSpec((1, s5_w), const),
            pl.BlockSpec((d, tn), lambda i, j: (0, c0 + j)),
            pl.BlockSpec((d, tn), lambda i, j: (0, c0 + nd + j)),
            pl.BlockSpec((1, tn), lambda i, j: (0, j)),
            pl.BlockSpec((1, tn), lambda i, j: (0, nd + j)),
            pl.BlockSpec((s5_w, tn), lambda i, j: (0, j)),
            pl.BlockSpec((lru_w, tn), lambda i, j: (0, j)),
        ],
        out_specs=pl.BlockSpec((tm, tn), lambda i, j: (i, j)),
        out_shape=jax.ShapeDtypeStruct((seq, d), BF16),
        scratch_shapes=[pltpu.VMEM((tm, s5_w), BF16)],
        compiler_params=_params(2),
        name="merge",
    )(h, y_s5, out_lru, w_glu, b_glu, w_in, w_in, b_gate, b_gate, wp_s5, wp_lru)


def _n_slabs(d):
    assert d % LANES == 0
    return d // LANES


def _store_slabs(ref, x):
    for s in range(ref.shape[0]):
        ref[s] = x[:, s * LANES:(s + 1) * LANES]


def _slab_row(ref, g):
    rows = ref.shape[2]
    shift = rows.bit_length() - 1
    assert rows == 1 << shift
    return ref.at[lax.shift_right_logical(g, shift), :, pl.ds(jnp.bitwise_and(g, rows - 1), 1), :]


def _slab_rows(ref, n):
    assert n <= ref.shape[2]
    return ref.at[0, :, pl.ds(0, n), :]


def _route(logits, n_experts, n_groups):
    per_group = n_experts // n_groups
    lane = lax.broadcasted_iota(jnp.int32, logits.shape, 1)
    big = jnp.int32(LANES)
    neg = jnp.float32(-jnp.inf)
    is_g = jnp.logical_and(lane >= n_experts, lane < n_experts + n_groups)
    lg = jnp.where(is_g, logits, neg)
    g_max = jnp.max(lg, axis=-1, keepdims=True)
    g_lane = jnp.min(jnp.where(lg == g_max, lane, big), axis=-1, keepdims=True)
    g_top = 1.0 / jnp.sum(jnp.where(is_g, jnp.exp(lg - g_max), 0.0), axis=-1, keepdims=True)
    g_idx = g_lane - n_experts
    in_group = jnp.logical_and(lane >= g_idx * per_group, lane < (g_idx + 1) * per_group)
    le = jnp.where(in_group, logits, neg)
    m1 = jnp.max(le, axis=-1, keepdims=True)
    i1 = jnp.min(jnp.where(le == m1, lane, big), axis=-1, keepdims=True)
    le2 = jnp.where(lane == i1, neg, le)
    m2 = jnp.max(le2, axis=-1, keepdims=True)
    i2 = jnp.min(jnp.where(le2 == m2, lane, big), axis=-1, keepdims=True)
    r = jnp.exp(m2 - m1)
    w1 = g_top / (1.0 + r)
    w2 = g_top * r / (1.0 + r)
    return i1, i2, w1, w2


def _out_route_kernel(m_ref, w_ref, x_ref, nw_ref, wr_ref, br_ref, x2_ref, ht_ref, route_ref,
                      route_t_ref, cnt_ref, cnt_scr, *, n_experts, n_groups):
    i = pl.program_id(0)
    tm = x_ref.shape[0]

    @pl.when(i == 0)
    def _():
        cnt_scr[...] = jnp.zeros_like(cnt_scr)

    x2 = x_ref[...] + jnp.dot(m_ref[...], w_ref[...], preferred_element_type=F32)
    x2_ref[...] = x2
    ht = _rms(x2, nw_ref[...])
    _store_slabs(ht_ref.at[0], ht)
    ht_hi = ht.astype(BF16)
    ht_lo = (ht - ht_hi.astype(F32)).astype(BF16)
    both = jnp.dot(ht_hi, wr_ref[...], preferred_element_type=F32)
    logits = (both[:, :LANES] + both[:, LANES:]
              + jnp.dot(ht_lo, wr_ref[:, :LANES], preferred_element_type=F32) + br_ref[...])
    i1, i2, w1, w2 = _route(logits, n_experts, n_groups)

    lane = lax.broadcasted_iota(jnp.int32, (tm, LANES), 1)
    oh1 = (lane == i1).astype(F32)
    oh2 = (lane == i2).astype(F32)
    oh = oh1 + oh2
    earlier = (lax.broadcasted_iota(jnp.int32, (tm, tm), 0)
               > lax.broadcasted_iota(jnp.int32, (tm, tm), 1)).astype(BF16)
    before = jnp.dot(earlier, oh.astype(BF16), preferred_element_type=F32) + cnt_scr[0:1, :]
    r1 = jnp.sum(before * oh1, axis=-1, keepdims=True)
    r2 = jnp.sum(before * oh2, axis=-1, keepdims=True)
    cnt_scr[0:1, :] = cnt_scr[0:1, :] + jnp.sum(oh, axis=0, keepdims=True)
    cnt_ref[...] = cnt_scr[...]

    cols = [i1.astype(F32), i2.astype(F32), r1, r2, w1, w2]
    route = jnp.zeros((tm, LANES), F32)
    for k, v in enumerate(cols):
        route = jnp.where(lane == k, v, route)
    route_ref[...] = route
    route_t_ref[...] = route.T[0:SUBLANES, :]


def _out_route(merged, w_out, x, norm_w, w_router, b_router, n_experts, n_groups, tm):
    seq, d = x.shape
    const = lambda i: (0, 0)
    kern = functools.partial(_out_route_kernel, n_experts=n_experts, n_groups=n_groups)
    return pl.pallas_call(
        kern,
        grid=(seq // tm,),
        in_specs=[
            pl.BlockSpec((tm, d), lambda i: (i, 0)),
            pl.BlockSpec((d, d), const),
            pl.BlockSpec((tm, d), lambda i: (i, 0)),
            pl.BlockSpec((1, d), const),
            pl.BlockSpec((d, 2 * LANES), const),
            pl.BlockSpec((1, LANES), const),
        ],
        out_specs=[
            pl.BlockSpec((tm, d), lambda i: (i, 0)),
            pl.BlockSpec((1, _n_slabs(d), tm, LANES), lambda i: (i, 0, 0, 0)),
            pl.BlockSpec((tm, LANES), lambda i: (i, 0)),
            pl.BlockSpec((SUBLANES, tm), lambda i: (0, i)),
            pl.BlockSpec((SUBLANES, LANES), const),
        ],
        out_shape=[
            jax.ShapeDtypeStruct((seq, d), F32),
            jax.ShapeDtypeStruct((seq // tm, _n_slabs(d), tm, LANES), F32),
            jax.ShapeDtypeStruct((seq, LANES), F32),
            jax.ShapeDtypeStruct((SUBLANES, seq), F32),
            jax.ShapeDtypeStruct((SUBLANES, LANES), F32),
        ],
        scratch_shapes=[pltpu.VMEM((SUBLANES, LANES), F32)],
        compiler_params=_params(1),
        name="out_route",
    )(merged, w_out, x, norm_w, w_router, b_router)


DMA_UNROLL = 8


def _dispatch_kernel(npad_ref, padstart_ref, nused_ref, idx_ref, ht_ref, xs_hbm, stage, zero_scr,
                     sem, pad_sem):
    i = pl.program_id(0)
    nt = pl.num_programs(0)
    n_experts = npad_ref.shape[0]
    nb, _, bm, _ = xs_hbm.shape
    tm = stage.shape[2]
    slot = i % 2
    chunk = min(tm, bm)

    def wait_rows(dsem):
        for _ in range(2 * tm // chunk):
            pltpu.make_async_copy(stage.at[0, :, pl.ds(0, chunk), :], _slab_rows(xs_hbm, chunk),
                                  dsem).wait()

    stage[slot] = ht_ref[0]

    def body(g, carry):
        for k in range(DMA_UNROLL):
            r = g * DMA_UNROLL + k
            src = stage.at[slot, :, pl.ds(r, 1), :]
            for half in range(2):
                dst = _slab_row(xs_hbm, idx_ref[0, 0, half * tm + r])
                pltpu.make_async_copy(src, dst, sem.at[slot]).start(priority=half)
        return carry
    lax.fori_loop(0, tm // DMA_UNROLL, body, 0)

    @pl.when(i > 0)
    def _():
        wait_rows(sem.at[1 - slot])

    @pl.when(i == nt - 1)
    def _():
        wait_rows(sem.at[slot])
        zero_scr[...] = jnp.zeros_like(zero_scr)
        zero_row = zero_scr.at[:, pl.ds(0, 1), :]
        shift = bm.bit_length() - 1
        for e in range(n_experts):
            p0 = padstart_ref[e]
            n_pad = npad_ref[e]
            blk = lax.shift_right_logical(p0, shift)
            row0 = jnp.bitwise_and(p0, bm - 1)
            n_single = jnp.minimum(jnp.bitwise_and(-row0, SUBLANES - 1), n_pad)
            q0 = row0 + n_single

            def single_start(j, carry):
                pltpu.make_async_copy(zero_row, _slab_row(xs_hbm, p0 + j), pad_sem).start()
                return carry

            def single_wait(j, carry):
                pltpu.make_async_copy(zero_row, _slab_row(xs_hbm, 0), pad_sem).wait()
                return carry

            def chunks(start):
                q = q0
                size = SUBLANES
                while size < bm:
                    take = jnp.logical_and(jnp.bitwise_and(q, size) != 0, n_pad > n_single)
                    src = zero_scr.at[:, pl.ds(0, size), :]

                    @pl.when(take)
                    def _():
                        if start:
                            dst = xs_hbm.at[blk, :, pl.ds(pl.multiple_of(q, SUBLANES), size), :]
                            pltpu.make_async_copy(src, dst, pad_sem).start()
                        else:
                            pltpu.make_async_copy(src, xs_hbm.at[0, :, pl.ds(0, size), :],
                                                  pad_sem).wait()
                    q = q + jnp.where(take, size, 0)
                    size *= 2
            lax.fori_loop(0, n_single, single_start, 0)
            chunks(True)
            lax.fori_loop(0, n_single, single_wait, 0)
            chunks(False)

        def blk_start(b, carry):
            pltpu.make_async_copy(zero_scr, xs_hbm.at[b], pad_sem).start()
            return carry

        def blk_wait(b, carry):
            pltpu.make_async_copy(zero_scr, xs_hbm.at[0], pad_sem).wait()
            return carry
        lax.fori_loop(nused_ref[0], nb, blk_start, 0)
        lax.fori_loop(nused_ref[0], nb, blk_wait, 0)


def _dispatch(npad, padstart, n_used, pos, ht_slabs, nb, bm, tm):
    _, ns, tile_rows, _ = ht_slabs.shape
    seq = pos.shape[0] * tm
    per_tile = tile_rows // tm
    assert per_tile * tm == tile_rows
    grid_spec = pltpu.PrefetchScalarGridSpec(
        num_scalar_prefetch=3,
        grid=(seq // tm,),
        in_specs=[
            pl.BlockSpec((1, 1, 2 * tm), lambda i, a, b, c: (i, 0, 0), memory_space=pltpu.SMEM),
            pl.BlockSpec((1, ns, tm, LANES),
                         lambda i, a, b, c: (i // per_tile, 0, i % per_tile, 0)),
        ],
        out_specs=pl.BlockSpec(memory_space=pl.ANY),
        scratch_shapes=[
            pltpu.VMEM((2, ns, tm, LANES), F32),
            pltpu.VMEM((ns, bm, LANES), F32),
            pltpu.SemaphoreType.DMA((2,)),
            pltpu.SemaphoreType.DMA(()),
        ],
    )
    return pl.pallas_call(
        _dispatch_kernel,
        grid_spec=grid_spec,
        out_shape=jax.ShapeDtypeStruct((nb, ns, bm, LANES), F32),
        compiler_params=pltpu.CompilerParams(dimension_semantics=("arbitrary",),
                                             has_side_effects=True),
        name="dispatch",
    )(npad, padstart, n_used, pos, ht_slabs)


def _experts_kernel(be_ref, nused_ref, par_ref, nxt_ref, xs_ref, wg_hbm, wu_hbm, wd_hbm, y_ref,
                    xs_scr, wgu_scr, wd_scr, wg_buf, wu_buf, wd_buf, sem):
    b = pl.program_id(0)
    ff = wd_scr.shape[0]

    def weight_copies(e, s):
        return [pltpu.make_async_copy(src.at[e], dst.at[s], sem.at[s])
                for src, dst in ((wg_hbm, wg_buf), (wu_hbm, wu_buf), (wd_hbm, wd_buf))]

    @pl.when(b < nused_ref[0])
    def _():
        @pl.when(jnp.logical_or(b == 0, be_ref[b] != be_ref[jnp.maximum(b - 1, 0)]))
        def _():
            s = par_ref[b]

            @pl.when(b == 0)
            def _():
                for c in weight_copies(be_ref[0], 0):
                    c.start()

            @pl.when(nxt_ref[b] >= 0)
            def _():
                for c in weight_copies(nxt_ref[b], 1 - s):
                    c.start()

            for c in weight_copies(be_ref[b], s):
                c.wait()
            wgu_scr[:, :ff] = wg_buf[s].astype(BF16)
            wgu_scr[:, ff:] = wu_buf[s].astype(BF16)
            wd_scr[...] = wd_buf[s].astype(BF16)

        for s in range(xs_ref.shape[1]):
            xs_scr[:, s * LANES:(s + 1) * LANES] = xs_ref[0, s].astype(BF16)
        hid = jnp.dot(xs_scr[...], wgu_scr[...], preferred_element_type=F32)
        act = (jax.nn.silu(hid[:, :ff]) * hid[:, ff:]).astype(BF16)
        y = jnp.dot(act, wd_scr[...], preferred_element_type=F32)
        _store_slabs(y_ref.at[0], y)

    @pl.when(b >= nused_ref[0])
    def _():
        y_ref[...] = jnp.zeros_like(y_ref)


def _experts(block_expert, n_used, xs_slabs, w_gate, w_up, w_down):
    nb, ns, bm, _ = xs_slabs.shape
    n_experts, d, ff = w_gate.shape
    blocks = jnp.arange(nb, dtype=jnp.int32)
    used = blocks < n_used[0]
    change = jnp.concatenate([jnp.ones((1,), jnp.bool_), block_expert[1:] != block_expert[:-1]])
    parity = ((jnp.cumsum(change.astype(jnp.int32)) - 1) % 2).astype(jnp.int32)
    later = jnp.where(jnp.logical_and(used[None, :], block_expert[None, :] > block_expert[:, None]),
                      block_expert[None, :], n_experts)
    nxt = jnp.min(later, axis=1)
    nxt = jnp.where(nxt < n_experts, nxt, -1).astype(jnp.int32)
    anywhere = pl.BlockSpec(memory_space=pl.ANY)
    grid_spec = pltpu.PrefetchScalarGridSpec(
        num_scalar_prefetch=4,
        grid=(nb,),
        in_specs=[
            pl.BlockSpec((1, ns, bm, LANES), lambda b, *_: (b, 0, 0, 0)),
            anywhere, anywhere, anywhere,
        ],
        out_specs=pl.BlockSpec((1, ns, bm, LANES), lambda b, *_: (b, 0, 0, 0)),
        scratch_shapes=[
            pltpu.VMEM((bm, d), BF16),
            pltpu.VMEM((d, 2 * ff), BF16),
            pltpu.VMEM((ff, d), BF16),
            pltpu.VMEM((2, d, ff), F32),
            pltpu.VMEM((2, d, ff), F32),
            pltpu.VMEM((2, ff, d), F32),
            pltpu.SemaphoreType.DMA((2,)),
        ],
    )
    return pl.pallas_call(
        _experts_kernel,
        grid_spec=grid_spec,
        out_shape=jax.ShapeDtypeStruct((nb, ns, bm, LANES), F32),
        compiler_params=_params(1),
        name="experts",
    )(block_expert, n_used, parity, nxt, xs_slabs, w_gate, w_up, w_down)


def _combine_kernel(idx_ref, idx_next_ref, y_hbm, x_ref, route_ref, fw_ref, o_ref, ybuf, sem,
                    *, final_norm):
    i = pl.program_id(0)
    nt = pl.num_programs(0)
    tm = x_ref.shape[0]
    slot = i % 2

    def start_gather(ids, buf, dsem):
        def body(g, carry):
            for k in range(DMA_UNROLL):
                r = g * DMA_UNROLL + k
                pltpu.make_async_copy(_slab_row(y_hbm, ids[0, 0, r]),
                                      buf.at[:, pl.ds(r, 1), :], dsem).start(priority=k % 2)
            return carry
        lax.fori_loop(0, 2 * tm // DMA_UNROLL, body, 0)

    @pl.when(i == 0)
    def _():
        start_gather(idx_ref, ybuf.at[0], sem.at[0])

    @pl.when(i + 1 < nt)
    def _():
        start_gather(idx_next_ref, ybuf.at[1 - slot], sem.at[1 - slot])

    buf = ybuf.at[slot]
    chunk = min(tm, y_hbm.shape[2])
    for c in range(2 * tm // chunk):
        pltpu.make_async_copy(_slab_rows(y_hbm, chunk), buf.at[:, pl.ds(c * chunk, chunk), :],
                              sem.at[slot]).wait()
    w1 = route_ref[:, 4:5]
    w2 = route_ref[:, 5:6]
    ssq = jnp.zeros((tm, 1), F32)
    for s in range(buf.shape[0]):
        cols = slice(s * LANES, (s + 1) * LANES)
        v = x_ref[:, cols] + w1 * buf[s, 0:tm, :] + w2 * buf[s, tm:2 * tm, :]
        o_ref[:, cols] = v
        ssq = ssq + jnp.sum(v * v, axis=-1, keepdims=True)
    if final_norm:
        scale = lax.rsqrt(ssq / o_ref.shape[1] + NORM_EPS)
        o_ref[...] = o_ref[...] * scale * fw_ref[...]


def _combine(pos, y_rows, x2, route, final_w, final_norm, tm):
    seq, d = x2.shape
    nt = seq // tm
    kern = functools.partial(_combine_kernel, final_norm=final_norm)
    return pl.pallas_call(
        kern,
        grid=(nt,),
        in_specs=[
            pl.BlockSpec((1, 1, 2 * tm), lambda i: (i, 0, 0), memory_space=pltpu.SMEM),
            pl.BlockSpec((1, 1, 2 * tm), lambda i: (jnp.minimum(i + 1, nt - 1), 0, 0),
                         memory_space=pltpu.SMEM),
            pl.BlockSpec(memory_space=pl.ANY),
            pl.BlockSpec((tm, d), lambda i: (i, 0)),
            pl.BlockSpec((tm, LANES), lambda i: (i, 0)),
            pl.BlockSpec((1, d), lambda i: (0, 0)),
        ],
        out_specs=pl.BlockSpec((tm, d), lambda i: (i, 0)),
        out_shape=jax.ShapeDtypeStruct((seq, d), F32),
        scratch_shapes=[
            pltpu.VMEM((2, _n_slabs(d), 2 * tm, LANES), F32),
            pltpu.SemaphoreType.DMA((2,)),
        ],
        compiler_params=_params(1),
        name="combine",
    )(pos, pos, y_rows, x2, route, final_w)


def _dispatch_plan(route_t, counts, n_experts, bm):
    seq = route_t.shape[1]
    nb = (2 * seq) // bm + n_experts
    e1, e2, r1, r2 = (route_t[k].astype(jnp.int32) for k in range(4))
    cnt = counts[0, :n_experts].astype(jnp.int32)
    cnt_pad = ((cnt + bm - 1) // bm) * bm
    ends = jnp.cumsum(cnt_pad)
    starts = ends - cnt_pad
    experts = jnp.arange(n_experts, dtype=jnp.int32)

    def start_of(e):
        return jnp.sum(jnp.where(e[:, None] == experts[None, :], starts[None, :], 0), axis=1)
    pos1 = start_of(e1) + r1
    pos2 = start_of(e2) + r2
    blk_start = jnp.arange(nb, dtype=jnp.int32) * bm
    n_before = jnp.sum((ends[None, :] <= blk_start[:, None]).astype(jnp.int32), axis=1)
    block_expert = jnp.minimum(n_before, n_experts - 1)
    n_used = (ends[-1:] // bm).astype(jnp.int32)
    return pos1, pos2, cnt_pad - cnt, starts + cnt, block_expert, n_used, nb


def kernel(x, norm_mix_w, w_in, b_gate, s5_lam_re, s5_lam_im, s5_log_step, s5_b_re, s5_b_im, s5_c_re, s5_c_im, s5_d, s5_w_glu, s5_b_glu, lru_conv_w, lru_conv_b, lru_w_a, lru_b_a, lru_w_x, lru_b_x, lru_lambda, w_proj_s5, w_proj_lru, w_out, norm_ffn_w, w_router_group, b_router_group, w_router_expert, b_router_expert, w_e_gate, w_e_up, w_e_down, norm_final_w):
    depth = w_in.shape[0]
    bsz, seq, d = x.shape
    s5_w = s5_w_glu.shape[-1]
    lru_w = lru_conv_b.shape[-1]
    heads, head_dim = lru_w_a.shape[1], lru_w_a.shape[2]
    n_groups = w_router_group.shape[-1]
    n_experts = w_router_expert.shape[-1]
    o3 = s5_w + 2 * lru_w

    outs = []
    for b in range(bsz):
        xb = x[b]
        for l in range(depth):
            row = lambda v: v.astype(F32).reshape(1, -1)
            w_in_l = w_in[l].astype(BF16)
            h, proj = _in_proj(xb, row(norm_mix_w[l]), w_in_l, o3)

            t_set, e_set, m_set, a_set = _s5_matrices(
                s5_lam_re[l], s5_lam_im[l], s5_log_step[l], s5_b_re[l], s5_b_im[l],
                s5_c_re[l], s5_c_im[l])
            y_s5 = _s5(proj, 0, t_set, e_set, m_set, a_set, row(s5_d[l]))

            w_cat = jnp.concatenate([lru_w_a[l], lru_w_x[l]], axis=-1).astype(BF16)
            nls = -LRU_C * jax.nn.softplus(-lru_lambda[l].astype(F32))
            out_lru = _rglru(proj, s5_w, lru_conv_w[l].astype(F32), row(lru_conv_b[l]), w_cat,
                             row(lru_b_a[l]), row(lru_b_x[l]), row(nls), heads, head_dim)

            merged = _merge(h, y_s5, out_lru, s5_w_glu[l].astype(BF16), row(s5_b_glu[l]),
                            w_in_l, o3, row(b_gate[l]),
                            w_proj_s5[l].astype(BF16), w_proj_lru[l].astype(BF16))
            pad = LANES - n_experts - n_groups
            w_router = jnp.concatenate([w_router_expert[l].astype(F32), w_router_group[l].astype(F32),
                                        jnp.zeros((d, pad), F32)], axis=1)
            w_router_hi = w_router.astype(BF16)
            w_router_lo = (w_router - w_router_hi.astype(F32)).astype(BF16)
            b_router = jnp.concatenate([b_router_expert[l].astype(F32), b_router_group[l].astype(F32),
                                        jnp.zeros((pad,), F32)]).reshape(1, LANES)
            x2, ht_slabs, route, route_t, counts = _out_route(
                merged, w_out[l].astype(BF16), xb, row(norm_ffn_w[l]),
                jnp.concatenate([w_router_hi, w_router_lo], axis=1), b_router,
                n_experts, n_groups, tm=ROUTE_TILE)

            pos1, pos2, npad, padstart, block_expert, n_used, nb = _dispatch_plan(
                route_t, counts, n_experts, MOE_BLOCK)
            def slots_by_tile(t):
                return jnp.concatenate([pos1.reshape(seq // t, 1, t), pos2.reshape(seq // t, 1, t)],
                                       axis=2)
            xs_slabs = _dispatch(npad, padstart, n_used, slots_by_tile(DISPATCH_TILE), ht_slabs,
                                 nb, MOE_BLOCK, DISPATCH_TILE)
            y_rows = _experts(block_expert, n_used, xs_slabs, w_e_gate[l], w_e_up[l], w_e_down[l])
            xb = _combine(slots_by_tile(COMBINE_TILE), y_rows, x2, route, row(norm_final_w),
                          final_norm=(l == depth - 1), tm=COMBINE_TILE)
        outs.append(xb)
    return jnp.stack(outs)
```
